```python
import jax, jax.numpy as jnp
from jax import lax
import numpy as np

D_MODEL = 2048
BATCH = 4
SEQ = 4096
DEPTH = 2

CHUNK = 64
N_MEM = 256
CONV_W = 768
CONV_TAPS = 3
HG_HEADS = 6
HG_DK = 128
HG_W = HG_HEADS * HG_DK
XA_HEADS = 4
XA_DH = 128
XA_W = XA_HEADS * XA_DH
IN_SPLITS = (CONV_W, CONV_W, CONV_W, HG_W, HG_W, HG_W, HG_W, XA_W, D_MODEL, D_MODEL, D_MODEL)
IN_COLS = sum(IN_SPLITS)
N_EXPERTS = 64
TOP_K = 8
N_GROUPS = 8
TOPK_GROUPS = 4
EXPERT_HID = 512
SHARED_HID = 512
ROUTED_SCALE = 2.5
EXPERT_BLOCK = 128
DN_ALPHA = (2 * DEPTH) ** 0.25
DN_BETA = (8 * DEPTH) ** -0.25
LN_EPS = 1e-5
RMS_EPS = 1e-6

kernel_name = 'hybrid_conv_hgrn2_memattn_moe_deepnorm'


def _layer_norm(x, g, b):
    x32 = x.astype(jnp.float32)
    mu = jnp.mean(x32, axis=-1, keepdims=True)
    xc = x32 - mu
    var = jnp.mean(xc * xc, axis=-1, keepdims=True)
    y = xc * lax.rsqrt(var + LN_EPS) * g.astype(jnp.float32) + b.astype(jnp.float32)
    return y.astype(x.dtype)


def _causal_dwconv(u, w):
    return lax.conv_general_dilated(
        u, w[:, None, :].astype(u.dtype), window_strides=(1,),
        padding=[(CONV_TAPS - 1, 0)], dimension_numbers=('NWC', 'WIO', 'NWC'),
        feature_group_count=u.shape[-1])


def _hgrn2_chunkwise(q, f_logit, v, lb):
    bsz, seq, _ = q.shape
    n_chunks = seq // CHUNK
    f32 = jnp.float32
    lb = lb.astype(f32)
    log_f = jnp.logaddexp(jnp.log(lb), jnp.log1p(-lb) + jax.nn.log_sigmoid(f_logit.astype(f32)))
    k = -jnp.expm1(log_f)

    def to_chunks(t):
        return t.astype(f32).reshape(bsz, n_chunks, CHUNK, HG_HEADS, HG_DK).transpose(1, 0, 3, 2, 4)

    tri = jnp.tril(jnp.ones((CHUNK, CHUNK), dtype=bool))[:, :, None]

    def step(state, inp):
        qb, kb, ab, vb = inp
        b = jnp.cumsum(ab, axis=2)
        b_last = b[:, :, -1:, :]
        o_inter = jnp.einsum('bhtk,bhkv->bhtv', qb * jnp.exp(b), state)
        diff = b[:, :, :, None, :] - b[:, :, None, :, :]
        decay = jnp.where(tri, jnp.exp(jnp.where(tri, diff, 0.0)), 0.0)
        scores = jnp.einsum('bhtk,bhsk,bhtsk->bhts', qb, kb, decay)
        o = o_inter + jnp.einsum('bhts,bhsv->bhtv', scores, vb)
        state = (jnp.exp(b_last[:, :, 0, :])[..., None] * state
                 + jnp.einsum('bhsk,bhsv->bhkv', kb * jnp.exp(b_last - b), vb))
        return state, o

    state0 = jnp.zeros((bsz, HG_HEADS, HG_DK, HG_DK), f32)
    _, o = lax.scan(step, state0, (to_chunks(q), to_chunks(k), to_chunks(log_f), to_chunks(v)))
    return o.transpose(1, 0, 3, 2, 4).reshape(bsz, seq, HG_HEADS, HG_DK)


def _gated_head_rmsnorm(o, g, gain):
    bsz, seq = o.shape[:2]
    o = o * lax.rsqrt(jnp.mean(o * o, axis=-1, keepdims=True) + RMS_EPS)
    o = o.reshape(bsz, seq, HG_W) * gain.astype(jnp.float32)
    return (o * jax.nn.silu(g.astype(jnp.float32))).astype(g.dtype)


def _memory_attention(q, mem, wk, wv):
    bsz, seq, _ = q.shape
    n_mem = mem.shape[1]
    kh = (mem @ wk).reshape(bsz, n_mem, XA_HEADS, XA_DH)
    vh = (mem @ wv).reshape(bsz, n_mem, XA_HEADS, XA_DH)
    qh = q.reshape(bsz, seq, XA_HEADS, XA_DH)
    s = jnp.einsum('bshd,bmhd->bhsm', qh, kh).astype(jnp.float32) * (XA_DH ** -0.5)
    p = jax.nn.softmax(s, axis=-1).astype(q.dtype)
    return jnp.einsum('bhsm,bmhd->bshd', p, vh).reshape(bsz, seq, XA_W)


def _swiglu(h, wg, wu, wd):
    return (jax.nn.silu(h @ wg) * (h @ wu)) @ wd


def _routed_experts(h, eidx, gate, wg, wu, wd):
    n, d = h.shape
    m = n * TOP_K
    flat_e = eidx.reshape(-1)
    flat_t = jnp.broadcast_to(jnp.arange(n, dtype=jnp.int32)[:, None], (n, TOP_K)).reshape(-1)
    flat_w = gate.reshape(-1)
    order = jnp.argsort(flat_e)
    se, st, sw = flat_e[order], flat_t[order], flat_w[order]
    counts = jnp.bincount(flat_e, length=N_EXPERTS)
    padded = ((counts + EXPERT_BLOCK - 1) // EXPERT_BLOCK) * EXPERT_BLOCK
    pad_end = jnp.cumsum(padded)
    pad_start = pad_end - padded
    start = jnp.cumsum(counts) - counts
    dest = pad_start[se] + (jnp.arange(m, dtype=jnp.int32) - start[se])
    n_blocks = (m + N_EXPERTS * (EXPERT_BLOCK - 1) + EXPERT_BLOCK - 1) // EXPERT_BLOCK
    rows = n_blocks * EXPERT_BLOCK
    row_tok = jnp.full((rows,), n, dtype=jnp.int32).at[dest].set(st)
    row_w = jnp.zeros((rows,), h.dtype).at[dest].set(sw.astype(h.dtype))
    blk_start = jnp.arange(n_blocks, dtype=jnp.int32) * EXPERT_BLOCK
    blk_exp = jnp.minimum(jnp.searchsorted(pad_end, blk_start, side='right'), N_EXPERTS - 1)
    hpad = jnp.concatenate([h, jnp.zeros((1, d), h.dtype)], axis=0)

    def step(out, inp):
        e, tok, wt = inp
        xb = hpad[tok]
        y = _swiglu(xb, wg[e], wu[e], wd[e])
        return out.at[tok].add(y * wt[:, None]), None

    out, _ = lax.scan(step, jnp.zeros((n + 1, d), h.dtype),
                      (blk_exp, row_tok.reshape(n_blocks, EXPERT_BLOCK), row_w.reshape(n_blocks, EXPERT_BLOCK)))
    return out[:n]


def _moe_ffn(h, router_w, router_b, wg, wu, wd, sg, su, sd):
    n = h.shape[0]
    s = jax.nn.sigmoid((h @ router_w).astype(jnp.float32))
    sel = s + router_b.astype(jnp.float32)
    grp = sel.reshape(n, N_GROUPS, N_EXPERTS // N_GROUPS)
    grp_score = lax.top_k(grp, 2)[0].sum(-1)
    _, gidx = lax.top_k(grp_score, TOPK_GROUPS)
    gmask = jax.nn.one_hot(gidx, N_GROUPS, dtype=jnp.float32).sum(1) > 0
    emask = jnp.repeat(gmask, N_EXPERTS // N_GROUPS, axis=1)
    _, eidx = lax.top_k(jnp.where(emask, sel, -jnp.inf), TOP_K)
    w = jnp.take_along_axis(s, eidx, axis=-1)
    w = (w / jnp.sum(w, axis=-1, keepdims=True) * ROUTED_SCALE).astype(h.dtype)
    return _swiglu(h, sg, su, sd) + _routed_experts(h, eidx, w, wg, wu, wd)


def setup_inputs(seed: int = 0) -> dict:
    key = jax.random.key(seed)
    ks = jax.random.split(key, 32)

    def nrm(k, shape, scale):
        return jax.random.normal(k, shape, jnp.float32) * scale

    L, D = DEPTH, D_MODEL
    return {
        'x': nrm(ks[0], (BATCH, SEQ, D), 1.0),
        'mem': nrm(ks[1], (BATCH, N_MEM, D), 1.0),
        'w_in': nrm(ks[2], (L, D, IN_COLS), D ** -0.5),
        'conv_w': nrm(ks[3], (L, CONV_TAPS, CONV_W), CONV_TAPS ** -0.5),
        'w_conv_out': nrm(ks[4], (L, CONV_W, D), CONV_W ** -0.5 * DN_BETA),
        'hg_lb_logits': nrm(ks[5], (L, HG_W), 0.5),
        'hg_norm_g': 1.0 + nrm(ks[6], (L, HG_W), 0.02),
        'w_hg_out': nrm(ks[7], (L, HG_W, D), HG_W ** -0.5 * DN_BETA),
        'w_mem_k': nrm(ks[8], (L, D, XA_W), D ** -0.5),
        'w_mem_v': nrm(ks[9], (L, D, XA_W), D ** -0.5 * DN_BETA),
        'w_xa_out': nrm(ks[10], (L, XA_W, D), XA_W ** -0.5 * DN_BETA),
        'w_o': nrm(ks[11], (L, D, D), D ** -0.5 * DN_BETA),
        'ln1_g': 1.0 + nrm(ks[12], (L, D), 0.02),
        'ln1_b': nrm(ks[13], (L, D), 0.02),
        'router_w': nrm(ks[14], (L, D, N_EXPERTS), D ** -0.5),
        'router_b': nrm(ks[15], (L, N_EXPERTS), 0.01),
        'exp_wg': nrm(ks[16], (L, N_EXPERTS, D, EXPERT_HID), D ** -0.5),
        'exp_wu': nrm(ks[17], (L, N_EXPERTS, D, EXPERT_HID), D ** -0.5),
        'exp_wd': nrm(ks[18], (L, N_EXPERTS, EXPERT_HID, D), EXPERT_HID ** -0.5 * DN_BETA),
        'sh_wg': nrm(ks[19], (L, D, SHARED_HID), D ** -0.5),
        'sh_wu': nrm(ks[20], (L, D, SHARED_HID), D ** -0.5),
        'sh_wd': nrm(ks[21], (L, SHARED_HID, D), SHARED_HID ** -0.5 * DN_BETA),
        'ln2_g': 1.0 + nrm(ks[22], (L, D), 0.02),
        'ln2_b': nrm(ks[23], (L, D), 0.02),
    }


def reference(x, mem, w_in, conv_w, w_conv_out, hg_lb_logits, hg_norm_g, w_hg_out, w_mem_k, w_mem_v,
              w_xa_out, w_o, ln1_g, ln1_b, router_w, router_b, exp_wg, exp_wu, exp_wd,
              sh_wg, sh_wu, sh_wd, ln2_g, ln2_b):
    bsz, seq, d = x.shape
    lb_all = jnp.cumsum(jax.nn.softmax(hg_lb_logits.astype(jnp.float32), axis=0), axis=0)
    lb_all = lb_all - lb_all[0:1]
    split_pts = [int(p) for p in np.cumsum(np.array(IN_SPLITS))[:-1]]
    for l in range(DEPTH):
        proj = x @ w_in[l]
        (u_c, b_c, c_c, q_h, f_h, i_h, g_h, q_x, g_a, g_b, g_x) = jnp.split(proj, split_pts, axis=-1)
        y_a = b_c * _causal_dwconv(c_c * u_c, conv_w[l])
        o_h = _hgrn2_chunkwise(q_h, f_h, i_h, lb_all[l])
        y_b = _gated_head_rmsnorm(o_h, g_h, hg_norm_g[l])
        y_x = _memory_attention(q_x, mem, w_mem_k[l], w_mem_v[l])
        merged = (jax.nn.sigmoid(g_a) * (y_a @ w_conv_out[l])
                  + jax.nn.sigmoid(g_b) * (y_b @ w_hg_out[l])
                  + jax.nn.sigmoid(g_x) * (y_x @ w_xa_out[l]))
        x = _layer_norm(DN_ALPHA * x + merged @ w_o[l], ln1_g[l], ln1_b[l])
        h = x.reshape(bsz * seq, d)
        f2 = _moe_ffn(h, router_w[l], router_b[l], exp_wg[l], exp_wu[l], exp_wd[l],
                      sh_wg[l], sh_wu[l], sh_wd[l]).reshape(bsz, seq, d)
        x = _layer_norm(DN_ALPHA * x + f2, ln2_g[l], ln2_b[l])
    return x
```

```python
import functools

import jax
import jax.numpy as jnp
from jax import lax
from jax.experimental import pallas as pl
from jax.experimental.pallas import tpu as pltpu

F32 = jnp.float32
BF16 = jnp.bfloat16
I32 = jnp.int32
U32 = jnp.uint32

HG_DK = 128
XA_DH = 128
TOP_K = 8
N_GROUPS = 8
TOPK_GROUPS = 4
ROUTED_SCALE = 2.5
LN_EPS = 1e-5
RMS_EPS = 1e-6
HG_SUB = 16
EXPERT_ROWS = 256
NEG_BIG = -1e30

V7X_VMEM_BYTES = 64 * 1024 * 1024
VMEM_CAP = V7X_VMEM_BYTES - 8 * 1024 * 1024


def _params(semantics, vmem_bytes):
    limit = int(min(VMEM_CAP, max(32 * 1024 * 1024, vmem_bytes * 5 // 4 + (4 << 20))))
    return pltpu.CompilerParams(dimension_semantics=semantics, vmem_limit_bytes=limit)


def _nbytes(shape, dtype):
    n = 1
    for s in shape:
        n *= s
    return n * jnp.dtype(dtype).itemsize


def _tile(n, want):
    t = min(n, want)
    while n % t:
        t //= 2
    return t


def _silu(x):
    return x * (1.0 / (1.0 + jnp.exp(-x)))


def _sigmoid(x):
    return 1.0 / (1.0 + jnp.exp(-x))


def _layer_norm_rows(y, g, b):
    mu = jnp.mean(y, axis=-1, keepdims=True)
    yc = y - mu
    var = jnp.mean(yc * yc, axis=-1, keepdims=True)
    return yc * lax.rsqrt(var + LN_EPS) * g + b


LANES = 128
PAIR = 2 * LANES


def _store_packed_rows(p_ref, y):
    rows, d = y.shape
    pieces = d // PAIR
    for s in range(pieces):
        lo = lax.bitcast_convert_type(y[:, s * PAIR:s * PAIR + LANES].astype(BF16).astype(F32), U32)
        hi = lax.bitcast_convert_type(y[:, s * PAIR + LANES:(s + 1) * PAIR].astype(BF16).astype(F32), U32)
        p_ref[pl.ds(s, rows, stride=pieces), :] = (hi & jnp.uint32(0xFFFF0000)) | (lo >> 16)


def _load_packed_rows(p_ref, rows, d):
    pieces = d // PAIR
    cols = []
    for s in range(pieces):
        w = p_ref[pl.ds(s, rows, stride=pieces), :]
        cols.append(lax.bitcast_convert_type(w << 16, F32).astype(BF16))
        cols.append(lax.bitcast_convert_type(w & jnp.uint32(0xFFFF0000), F32).astype(BF16))
    return jnp.concatenate(cols, axis=1)


def _matmul_kernel(x_ref, w_ref, o_ref):
    o_ref[...] = jnp.dot(x_ref[...].astype(BF16), w_ref[...],
                         preferred_element_type=F32).astype(o_ref.dtype)


def _matmul(x, w, out_dtype, tm, tn):
    m, k = x.shape
    n = w.shape[1]
    tm, tn = _tile(m, tm), _tile(n, tn)
    vmem = 2 * (_nbytes((tm, k), x.dtype) + _nbytes((k, tn), w.dtype) + _nbytes((tm, tn), out_dtype)) \
        + _nbytes((tm, k), BF16) + _nbytes((tm, tn), F32)
    return pl.pallas_call(
        _matmul_kernel,
        out_shape=jax.ShapeDtypeStruct((m, n), out_dtype),
        grid=(m // tm, n // tn),
        in_specs=[pl.BlockSpec((tm, k), lambda i, j: (i, 0)),
                  pl.BlockSpec((k, tn), lambda i, j: (0, j))],
        out_specs=pl.BlockSpec((tm, tn), lambda i, j: (i, j)),
        compiler_params=_params(("arbitrary", "arbitrary"), vmem),
    )(x, w)


def _conv_kernel(x_ref, w_ref, cw_ref, o_ref, carry_ref, *, tiles_per_seq):
    i = pl.program_id(0)
    tm, cwid = o_ref.shape

    @pl.when(i % tiles_per_seq == 0)
    def _():
        carry_ref[...] = jnp.zeros_like(carry_ref)

    p = jnp.dot(x_ref[...].astype(BF16), w_ref[...], preferred_element_type=F32)
    u = p[:, :cwid]
    b = p[:, cwid:2 * cwid]
    c = p[:, 2 * cwid:]
    cu = c * u
    prev = carry_ref[...]
    row = lax.broadcasted_iota(I32, (tm, cwid), 0)
    cu1 = jnp.where(row == 0, prev[7:8, :], pltpu.roll(cu, 1, 0))
    cu2 = pltpu.roll(cu, 2, 0)
    cu2 = jnp.where(row == 0, prev[6:7, :], jnp.where(row == 1, prev[7:8, :], cu2))
    cw = cw_ref[...]
    y = b * (cw[0:1, :] * cu2 + cw[1:2, :] * cu1 + cw[2:3, :] * cu)
    o_ref[...] = y.astype(o_ref.dtype)
    carry_ref[...] = cu[tm - 8:, :]


def _conv_branch(x, w_conv_in, conv_w, seq):
    n, d = x.shape
    cwid = conv_w.shape[1]
    assert conv_w.shape[0] == 3
    tm = _tile(seq, 512)
    vmem = 2 * (_nbytes((tm, d), x.dtype) + _nbytes(w_conv_in.shape, BF16) + _nbytes((tm, cwid), BF16)) \
        + 6 * _nbytes((tm, 3 * cwid), F32)
    return pl.pallas_call(
        functools.partial(_conv_kernel, tiles_per_seq=seq // tm),
        out_shape=jax.ShapeDtypeStruct((n, cwid), BF16),
        grid=(n // tm,),
        in_specs=[pl.BlockSpec((tm, d), lambda i: (i, 0)),
                  pl.BlockSpec(w_conv_in.shape, lambda i: (0, 0)),
                  pl.BlockSpec(conv_w.shape, lambda i: (0, 0))],
        out_specs=pl.BlockSpec((tm, cwid), lambda i: (i, 0)),
        scratch_shapes=[pltpu.VMEM((8, cwid), F32)],
        compiler_params=_params(("arbitrary",), vmem),
    )(x, w_conv_in, conv_w)


def _head_sum(x):
    return jnp.sum(x, axis=-1, keepdims=True)


def _hgrn_kernel(q_ref, f_ref, v_ref, g_ref, c1_ref, c2_ref, gain_ref, o_ref,
                 st_ref, qd_ref, kd_ref, vb_ref, dec_ref, oi_ref, *, heads):
    tt, hw = q_ref.shape
    sub = HG_SUB
    dk = HG_DK

    @pl.when(pl.program_id(1) == 0)
    def _():
        st_ref[...] = jnp.zeros_like(st_ref)

    f = f_ref[...]
    log_sig = jnp.minimum(f, 0.0) - jnp.log1p(jnp.exp(-jnp.abs(f)))
    a1 = c1_ref[...]
    a2 = c2_ref[...] + log_sig
    log_f = jnp.maximum(a1, a2) + jnp.log1p(jnp.exp(-jnp.abs(a1 - a2)))
    kin = 1.0 - jnp.exp(log_f)

    r = lax.broadcasted_iota(I32, (tt, tt), 0)
    c = lax.broadcasted_iota(I32, (tt, tt), 1)
    same = (r // sub) == (c // sub)
    tri = jnp.where(same & (c <= r), 1.0, 0.0).astype(F32)
    blk = jnp.where(same, 1.0, 0.0).astype(F32)
    bcum = jnp.dot(tri, log_f, preferred_element_type=F32, precision=lax.Precision.HIGHEST)
    blast = jnp.dot(blk, log_f, preferred_element_type=F32, precision=lax.Precision.HIGHEST)

    q = q_ref[...]
    qd_ref[...] = (q * jnp.exp(bcum)).astype(BF16)
    kd_ref[...] = (kin * jnp.exp(blast - bcum)).astype(BF16)
    vb_ref[...] = v_ref[...].astype(BF16)
    dec_ref[...] = jnp.exp(blast)

    def chunk(j, carry):
        r0 = pl.multiple_of(j * sub, sub)
        for h in range(heads):
            ls = slice(h * dk, (h + 1) * dk)
            st = st_ref[h]
            qd = qd_ref[pl.ds(r0, sub), ls]
            oi_ref[pl.ds(r0, sub), ls] = lax.dot_general(
                qd, st.astype(BF16), (((1,), (1,)), ((), ())), preferred_element_type=F32)
            upd = lax.dot_general(vb_ref[pl.ds(r0, sub), ls], kd_ref[pl.ds(r0, sub), ls],
                                  (((0,), (0,)), ((), ())), preferred_element_type=F32)
            st_ref[h] = st * dec_ref[pl.ds(r0, 1), ls] + upd
        return carry

    lax.fori_loop(0, tt // sub, chunk, 0)

    pos = lax.broadcasted_iota(I32, (tt, dk), 0) % sub
    gain = gain_ref[...]
    for h in range(heads):
        ls = slice(h * dk, (h + 1) * dk)
        qh = q[:, ls]
        kh = kin[:, ls]
        bh = bcum[:, ls]
        vh = v_ref[:, ls]
        acc = oi_ref[:, ls] + _head_sum(qh * kh) * vh
        for d in range(1, sub):
            ok = pos >= d
            arg = jnp.where(ok, bh - pltpu.roll(bh, d, 0), NEG_BIG)
            sc = _head_sum(qh * pltpu.roll(kh, d, 0) * jnp.exp(arg))
            acc = acc + sc * pltpu.roll(vh, d, 0)
        ms = _head_sum(acc * acc) * (1.0 / dk)
        y = acc * lax.rsqrt(ms + RMS_EPS) * gain[:, ls] * _silu(g_ref[:, ls])
        o_ref[:, ls] = y.astype(o_ref.dtype)


def _hgrn_branch(qfig, c1, c2, gain, bsz, seq):
    n = qfig.shape[0]
    hw = qfig.shape[1] // 4
    heads = hw // HG_DK
    tt = _tile(seq, 256)
    nt = seq // tt
    blk = lambda col: pl.BlockSpec((tt, hw), lambda b, t, col=col: (b * nt + t, col))
    vec = pl.BlockSpec((1, hw), lambda b, t: (0, 0))
    vmem = 2 * 5 * _nbytes((tt, hw), F32) + 12 * _nbytes((tt, hw), F32) + heads * HG_DK * HG_DK * 4
    return pl.pallas_call(
        functools.partial(_hgrn_kernel, heads=heads),
        out_shape=jax.ShapeDtypeStruct((n, hw), BF16),
        grid=(bsz, nt),
        in_specs=[blk(0), blk(1), blk(2), blk(3), vec, vec, vec],
        out_specs=pl.BlockSpec((tt, hw), lambda b, t: (b * nt + t, 0)),
        scratch_shapes=[pltpu.VMEM((heads, HG_DK, HG_DK), F32),
                        pltpu.VMEM((tt, hw), BF16), pltpu.VMEM((tt, hw), BF16),
                        pltpu.VMEM((tt, hw), BF16), pltpu.VMEM((tt, hw), F32),
                        pltpu.VMEM((tt, hw), F32)],
        compiler_params=_params(("arbitrary", "arbitrary"), vmem),
    )(qfig, qfig, qfig, qfig, c1, c2, gain)


def _attn_kernel(x_ref, wq_ref, k_ref, v_ref, o_ref, *, heads):
    dh = XA_DH
    q = jnp.dot(x_ref[...].astype(BF16), wq_ref[...], preferred_element_type=F32)
    scale = dh ** -0.5
    for h in range(heads):
        ls = slice(h * dh, (h + 1) * dh)
        s = lax.dot_general(q[:, ls].astype(BF16), k_ref[:, ls], (((1,), (1,)), ((), ())),
                            preferred_element_type=F32) * scale
        e = jnp.exp(s - jnp.max(s, axis=-1, keepdims=True))
        p = e / jnp.sum(e, axis=-1, keepdims=True)
        o = jnp.dot(p.astype(BF16), v_ref[:, ls], preferred_element_type=F32)
        o_ref[:, ls] = o.astype(o_ref.dtype)


def _attn_branch(x, wq, kv, seq, n_mem):
    n, d = x.shape
    xw = wq.shape[1]
    heads = xw // XA_DH
    tm = _tile(seq, 512)
    per_seq = seq // tm
    vmem = 2 * (_nbytes((tm, d), x.dtype) + _nbytes(wq.shape, BF16) + 2 * _nbytes((n_mem, xw), BF16)
                + _nbytes((tm, xw), BF16)) + 4 * _nbytes((tm, xw), F32) + 4 * _nbytes((tm, n_mem), F32)
    return pl.pallas_call(
        functools.partial(_attn_kernel, heads=heads),
        out_shape=jax.ShapeDtypeStruct((n, xw), BF16),
        grid=(n // tm,),
        in_specs=[pl.BlockSpec((tm, d), lambda i: (i, 0)),
                  pl.BlockSpec(wq.shape, lambda i: (0, 0)),
                  pl.BlockSpec((n_mem, xw), lambda i: (i // per_seq, 0)),
                  pl.BlockSpec((n_mem, xw), lambda i: (i // per_seq, 1))],
        out_specs=pl.BlockSpec((tm, xw), lambda i: (i, 0)),
        compiler_params=_params(("arbitrary",), vmem),
    )(x, wq, kv, kv)


def _merge_kernel(x_ref, ya_ref, yb_ref, yx_ref, wga_ref, wgb_ref, wgx_ref, wa_ref, wb_ref, wx_ref,
                  o_ref, xb_ref):
    @pl.when(pl.program_id(1) == 0)
    def _():
        xb_ref[...] = x_ref[...].astype(BF16)

    xb = xb_ref[...]
    dot = functools.partial(jnp.dot, preferred_element_type=F32)
    m = _sigmoid(dot(xb, wga_ref[...])) * dot(ya_ref[...], wa_ref[...])
    m = m + _sigmoid(dot(xb, wgb_ref[...])) * dot(yb_ref[...], wb_ref[...])
    m = m + _sigmoid(dot(xb, wgx_ref[...])) * dot(yx_ref[...], wx_ref[...])
    o_ref[...] = m.astype(o_ref.dtype)


def _merge(x, ya, yb, yx, w_gates, wa, wb, wx):
    n, d = x.shape
    tm, tn = _tile(n, 1024), _tile(d, 512)
    nj = d // tn
    row = lambda a: pl.BlockSpec((tm, a.shape[1]), lambda i, j: (i, 0))
    gate = lambda g: pl.BlockSpec((d, tn), lambda i, j, g=g: (0, g * nj + j))
    outw = lambda a: pl.BlockSpec((a.shape[0], tn), lambda i, j: (0, j))
    vmem = 2 * (_nbytes((tm, d), x.dtype) + _nbytes((tm, ya.shape[1] + yb.shape[1] + yx.shape[1]), BF16)
                + 3 * _nbytes((d, tn), BF16) + _nbytes((wa.shape[0] + wb.shape[0] + wx.shape[0], tn), BF16)
                + _nbytes((tm, tn), BF16)) + _nbytes((tm, d), BF16) + 8 * _nbytes((tm, tn), F32)
    return pl.pallas_call(
        _merge_kernel,
        out_shape=jax.ShapeDtypeStruct((n, d), BF16),
        grid=(n // tm, nj),
        in_specs=[row(x), row(ya), row(yb), row(yx), gate(0), gate(1), gate(2),
                  outw(wa), outw(wb), outw(wx)],
        out_specs=pl.BlockSpec((tm, tn), lambda i, j: (i, j)),
        scratch_shapes=[pltpu.VMEM((tm, d), BF16)],
        compiler_params=_params(("arbitrary", "arbitrary"), vmem),
    )(x, ya, yb, yx, w_gates, w_gates, w_gates, wa, wb, wx)


def _oproj_ln_kernel(m_ref, w_ref, x_ref, g_ref, b_ref, o_ref, p_ref, *, alpha):
    y = alpha * x_ref[...] + jnp.dot(m_ref[...], w_ref[...], preferred_element_type=F32)
    y = _layer_norm_rows(y, g_ref[...], b_ref[...])
    o_ref[...] = y
    _store_packed_rows(p_ref, y)


def _oproj_ln(merged, w_o, x, g, b, alpha):
    n, d = x.shape
    assert d % PAIR == 0
    pieces = d // PAIR
    tm = _tile(n, 512)
    vec = pl.BlockSpec((1, d), lambda i: (0, 0))
    vmem = 2 * (_nbytes((tm, d), BF16) + _nbytes((d, d), BF16) + 2 * _nbytes((tm, d), F32)
                + _nbytes((tm, d // 2), U32)) + 4 * _nbytes((tm, d), F32)
    return pl.pallas_call(
        functools.partial(_oproj_ln_kernel, alpha=alpha),
        out_shape=(jax.ShapeDtypeStruct((n, d), F32), jax.ShapeDtypeStruct((n * pieces, LANES), U32)),
        grid=(n // tm,),
        in_specs=[pl.BlockSpec((tm, d), lambda i: (i, 0)),
                  pl.BlockSpec((d, d), lambda i: (0, 0)),
                  pl.BlockSpec((tm, d), lambda i: (i, 0)), vec, vec],
        out_specs=(pl.BlockSpec((tm, d), lambda i: (i, 0)),
                   pl.BlockSpec((tm * pieces, LANES), lambda i: (i, 0))),
        compiler_params=_params(("arbitrary",), vmem),
    )(merged, w_o, x, g, b)


def _rank_desc(vals):
    rows = vals.shape[0]
    iota = lax.broadcasted_iota(I32, vals.shape, 0)
    rank = jnp.zeros(vals.shape, I32)
    for j in range(rows):
        rowv = vals[j:j + 1, :]
        tie = jnp.where(iota > j, 1, 0)
        rank = rank + jnp.where(rowv > vals, 1, jnp.where(rowv == vals, tie, 0))
    return rank


def _router_kernel(x_ref, rw_ref, rb_ref, eid_ref, rnk_ref, gate_ref, cnt_ref, carry_ref):
    ne = rw_ref.shape[0]
    tm = x_ref.shape[0]
    gsz = ne // N_GROUPS

    @pl.when(pl.program_id(0) == 0)
    def _():
        carry_ref[...] = jnp.zeros_like(carry_ref)

    logits = lax.dot_general(rw_ref[...], x_ref[...], (((1,), (1,)), ((), ())),
                             preferred_element_type=F32, precision=lax.Precision.HIGHEST)
    s = _sigmoid(logits)
    sel = s + rb_ref[...]

    grp = sel.reshape(N_GROUPS, gsz, tm)
    sub_iota = lax.broadcasted_iota(I32, (N_GROUPS, gsz, tm), 1)
    m1 = jnp.max(grp, axis=1, keepdims=True)
    first = jnp.min(jnp.where(grp == m1, sub_iota, gsz), axis=1, keepdims=True)
    m2 = jnp.max(jnp.where(sub_iota == first, -jnp.inf, grp), axis=1, keepdims=True)
    gscore = (m1 + m2).reshape(N_GROUPS, tm)

    gsel = _rank_desc(gscore) < TOPK_GROUPS
    emask = jnp.broadcast_to(gsel.reshape(N_GROUPS, 1, tm), (N_GROUPS, gsz, tm)).reshape(ne, tm)
    masked = jnp.where(emask, sel, -jnp.inf)

    eiota = lax.broadcasted_iota(I32, (ne, tm), 0)
    chosen = (_rank_desc(masked) < TOP_K) & emask

    w = jnp.where(chosen, s, 0.0)
    gate = w / jnp.sum(w, axis=0, keepdims=True) * ROUTED_SCALE

    ch = jnp.where(chosen, 1.0, 0.0).astype(BF16)
    tr = lax.broadcasted_iota(I32, (tm, tm), 0)
    tc = lax.broadcasted_iota(I32, (tm, tm), 1)
    before = jnp.where(tr < tc, 1.0, 0.0).astype(BF16)
    rank_tok = (carry_ref[...] + jnp.dot(ch, before, preferred_element_type=F32)).astype(I32)
    carry_ref[...] = carry_ref[...] + jnp.sum(jnp.where(chosen, 1.0, 0.0), axis=1, keepdims=True)
    cnt_ref[...] = jnp.broadcast_to(carry_ref[...], cnt_ref.shape).astype(I32)

    er = lax.broadcasted_iota(I32, (ne, ne), 0)
    ec = lax.broadcasted_iota(I32, (ne, ne), 1)
    lower = jnp.where(ec < er, 1.0, 0.0).astype(BF16)
    slot = jnp.dot(lower, ch, preferred_element_type=F32).astype(I32)
    for k in range(TOP_K):
        pick = chosen & (slot == k)
        eid_ref[k:k + 1, :] = jnp.sum(jnp.where(pick, eiota, 0), axis=0, keepdims=True)
        rnk_ref[k:k + 1, :] = jnp.sum(jnp.where(pick, rank_tok, 0), axis=0, keepdims=True)
        gate_ref[k:k + 1, :] = jnp.sum(jnp.where(pick, gate, 0.0), axis=0, keepdims=True)


def _router(x, rw_t, rb):
    n, d = x.shape
    ne = rw_t.shape[0]
    tm = _tile(n, 512)
    out = lambda dt: jax.ShapeDtypeStruct((TOP_K, n), dt)
    ospec = pl.BlockSpec((TOP_K, tm), lambda i: (0, i))
    vmem = 2 * (_nbytes((tm, d), F32) + _nbytes((ne, d), F32)) + 3 * _nbytes((tm, tm), F32) \
        + 24 * _nbytes((ne, tm), F32)
    return pl.pallas_call(
        _router_kernel,
        out_shape=(out(I32), out(I32), out(F32), jax.ShapeDtypeStruct((ne, 128), I32)),
        grid=(n // tm,),
        in_specs=[pl.BlockSpec((tm, d), lambda i: (i, 0)),
                  pl.BlockSpec((ne, d), lambda i: (0, 0)),
                  pl.BlockSpec((ne, 1), lambda i: (0, 0))],
        out_specs=(ospec, ospec, ospec, pl.BlockSpec((ne, 128), lambda i: (0, 0))),
        scratch_shapes=[pltpu.VMEM((ne, 1), F32)],
        compiler_params=_params(("arbitrary",), vmem),
    )(x, rw_t, rb)


def _pad_pieces(max_pad):
    pieces = []
    p = 1
    while p <= max_pad:
        pieces.append(p)
        p *= 2
    return pieces[::-1]


def _dispatch_kernel(pad_ref, dest_hbm, xp_hbm, zero_hbm, xs_hbm, idx_ref, sem_idx, sem_row, sem_pad,
                     *, tq, ne, rp):
    i = pl.program_id(0)

    def row_copy(src_row, dst_row):
        return pltpu.make_async_copy(xp_hbm.at[pl.ds(pl.multiple_of(src_row * rp, rp), rp)],
                                     xs_hbm.at[pl.ds(pl.multiple_of(dst_row * rp, rp), rp)], sem_row)

    def pad_copies(fn):
        def per_expert(e, carry):
            row = pad_ref[0, e]
            npad = pad_ref[1, e]
            for p in _pad_pieces(EXPERT_ROWS - 1):
                @pl.when((npad & p) != 0)
                def _(row=row, p=p):
                    fn(pltpu.make_async_copy(zero_hbm.at[pl.ds(0, p * rp)],
                                             xs_hbm.at[pl.ds(pl.multiple_of(row * rp, rp), p * rp)], sem_pad))
                row = row + (npad & p)
            return carry
        lax.fori_loop(0, ne, per_expert, 0)

    @pl.when(i == 0)
    def _():
        pad_copies(lambda cp: cp.start())

    idx_cp = pltpu.make_async_copy(dest_hbm.at[i], idx_ref, sem_idx)
    idx_cp.start()
    idx_cp.wait()

    def per_token(t, carry):
        tok = i * tq + t
        for k in range(TOP_K):
            row_copy(tok, idx_ref[t * TOP_K + k]).start()
        return carry

    lax.fori_loop(0, tq, per_token, 0)

    def drain(t, carry):
        for k in range(TOP_K):
            row_copy(0, 0).wait()
        return carry

    lax.fori_loop(0, tq, drain, 0)

    @pl.when(i == 0)
    def _():
        pad_copies(lambda cp: cp.wait())


def _dispatch(pad_info, dest_tok, xp, n, rows):
    rp = xp.shape[0] // n
    ne = pad_info.shape[1]
    tq = _tile(n, 1024)
    steps = n // tq
    dest2 = dest_tok.reshape(steps, tq * TOP_K)
    zero = jnp.zeros((EXPERT_ROWS * rp, LANES), U32)
    any_spec = pl.BlockSpec(memory_space=pl.ANY)
    grid_spec = pltpu.PrefetchScalarGridSpec(
        num_scalar_prefetch=1, grid=(steps,),
        in_specs=[any_spec, any_spec, any_spec], out_specs=any_spec,
        scratch_shapes=[pltpu.SMEM((tq * TOP_K,), I32), pltpu.SemaphoreType.DMA(()),
                        pltpu.SemaphoreType.DMA(()), pltpu.SemaphoreType.DMA(())])
    return pl.pallas_call(
        functools.partial(_dispatch_kernel, tq=tq, ne=ne, rp=rp),
        out_shape=jax.ShapeDtypeStruct((rows * rp, LANES), U32),
        grid_spec=grid_spec,
        compiler_params=pltpu.CompilerParams(dimension_semantics=("arbitrary",), has_side_effects=True),
    )(pad_info, dest2, xp, zero)


def _expert_kernel(be_ref, nu_ref, xs_ref, wg_ref, wu_ref, wd_ref, ys_ref, wgb_ref, wub_ref, wdb_ref):
    b = pl.program_id(0)

    @pl.when(b < nu_ref[0])
    def _():
        prev = be_ref[jnp.maximum(b - 1, 0)]

        @pl.when((b == 0) | (be_ref[b] != prev))
        def _():
            wgb_ref[...] = wg_ref[0].astype(BF16)
            wub_ref[...] = wu_ref[0].astype(BF16)
            wdb_ref[...] = wd_ref[0].astype(BF16)

        t = EXPERT_ROWS
        d = wgb_ref.shape[0]
        xb = _load_packed_rows(xs_ref, t, d)
        dot = functools.partial(jnp.dot, preferred_element_type=F32)
        hmid = (_silu(dot(xb, wgb_ref[...])) * dot(xb, wub_ref[...])).astype(BF16)
        y = dot(hmid, wdb_ref[...])
        q = d // LANES
        for s in range(q):
            ys_ref[pl.ds(s, t, stride=q), :] = y[:, s * LANES:(s + 1) * LANES]


def _experts(blk_exp, n_used, xs, wg, wu, wd):
    ne, d, hid = wg.shape
    t = EXPERT_ROWS
    rp, rq = d // PAIR, d // LANES
    rows = xs.shape[0] // rp
    nb = rows // t
    clamp = lambda b, nu: jnp.minimum(b, jnp.maximum(nu[0] - 1, 0))
    wspec = lambda shape: pl.BlockSpec((1,) + shape, lambda b, be, nu: (be[clamp(b, nu)], 0, 0))
    grid_spec = pltpu.PrefetchScalarGridSpec(
        num_scalar_prefetch=2, grid=(nb,),
        in_specs=[pl.BlockSpec((t * rp, LANES), lambda b, be, nu: (clamp(b, nu), 0)),
                  wspec((d, hid)), wspec((d, hid)), wspec((hid, d))],
        out_specs=pl.BlockSpec((t * rq, LANES), lambda b, be, nu: (clamp(b, nu), 0)),
        scratch_shapes=[pltpu.VMEM((d, hid), BF16), pltpu.VMEM((d, hid), BF16), pltpu.VMEM((hid, d), BF16)])
    vmem = 2 * (3 * _nbytes((d, hid), F32) + _nbytes((t, d // 2), U32) + _nbytes((t, d), F32)) \
        + 3 * _nbytes((d, hid), BF16) + 4 * _nbytes((t, d), F32)
    return pl.pallas_call(
        _expert_kernel,
        out_shape=jax.ShapeDtypeStruct((rows * rq, LANES), F32),
        grid_spec=grid_spec,
        compiler_params=_params(("arbitrary",), vmem),
    )(blk_exp, n_used, xs, wg, wu, wd)


def _combine_kernel(dest_hbm, ys_hbm, x_ref, gate_ref, sg_ref, su_ref, sd_ref, g_ref, b_ref, o_ref,
                    idx_ref, buf_ref, sem_idx, sem_row, *, tc, alpha):
    i = pl.program_id(0)
    rq = x_ref.shape[1] // LANES
    idx_cp = pltpu.make_async_copy(dest_hbm.at[i], idx_ref, sem_idx)
    idx_cp.start()
    idx_cp.wait()

    def row_copy(src_row, dst_row):
        return pltpu.make_async_copy(ys_hbm.at[pl.ds(pl.multiple_of(src_row * rq, rq), rq)],
                                     buf_ref.at[pl.ds(pl.multiple_of(dst_row * rq, rq), rq)], sem_row)

    def per_token(t, carry):
        for k in range(TOP_K):
            row_copy(idx_ref[t * TOP_K + k], k * tc + t).start()
        return carry

    lax.fori_loop(0, tc, per_token, 0)

    x = x_ref[...]
    xb = x.astype(BF16)
    dot = functools.partial(jnp.dot, preferred_element_type=F32)
    hmid = (_silu(dot(xb, sg_ref[...])) * dot(xb, su_ref[...])).astype(BF16)
    shared = dot(hmid, sd_ref[...])

    def drain(t, carry):
        for k in range(TOP_K):
            row_copy(0, 0).wait()
        return carry

    lax.fori_loop(0, tc, drain, 0)

    gate = gate_ref[...]
    cols = []
    for s in range(rq):
        piece = lambda k: buf_ref[pl.ds(k * tc * rq + s, tc, stride=rq), :]
        acc = piece(0) * gate[:, 0:1]
        for k in range(1, TOP_K):
            acc = acc + piece(k) * gate[:, k:k + 1]
        cols.append(acc)
    routed = jnp.concatenate(cols, axis=1)
    o_ref[...] = _layer_norm_rows(alpha * x + (shared + routed), g_ref[...], b_ref[...])


def _combine(dest_tok, ys, x, gate_tok, sg, su, sd, g, b, alpha):
    n, d = x.shape
    hid = sg.shape[1]
    tc = _tile(n, 128)
    steps = n // tc
    dest2 = dest_tok.reshape(steps, tc * TOP_K)
    any_spec = pl.BlockSpec(memory_space=pl.ANY)
    vec = pl.BlockSpec((1, d), lambda i: (0, 0))
    full = lambda a: pl.BlockSpec(a.shape, lambda i: (0, 0))
    vmem = _nbytes((TOP_K, tc, d), F32) + 2 * (2 * _nbytes((tc, d), F32) + 3 * _nbytes((d, hid), BF16)) \
        + 6 * _nbytes((tc, d), F32)
    return pl.pallas_call(
        functools.partial(_combine_kernel, tc=tc, alpha=alpha),
        out_shape=jax.ShapeDtypeStruct((n, d), F32),
        grid=(steps,),
        in_specs=[any_spec, any_spec,
                  pl.BlockSpec((tc, d), lambda i: (i, 0)),
                  pl.BlockSpec((tc, TOP_K), lambda i: (i, 0)),
                  full(sg), full(su), full(sd), vec, vec],
        out_specs=pl.BlockSpec((tc, d), lambda i: (i, 0)),
        scratch_shapes=[pltpu.SMEM((tc * TOP_K,), I32), pltpu.VMEM((TOP_K * tc * (d // LANES), LANES), F32),
                        pltpu.SemaphoreType.DMA(()), pltpu.SemaphoreType.DMA(())],
        compiler_params=_params(("arbitrary",), vmem),
    )(dest2, ys, x, gate_tok, sg, su, sd, g, b)


def _moe(x1, x1p, rw_t, rb, wg, wu, wd, sg, su, sd, g, b, alpha):
    n, d = x1.shape
    ne = rw_t.shape[0]
    eid, rnk, gate, cnt = _router(x1, rw_t, rb)
    counts = cnt[:, 0]
    padded = (counts + EXPERT_ROWS - 1) // EXPERT_ROWS * EXPERT_ROWS
    pad_end = jnp.cumsum(padded)
    pad_start = pad_end - padded
    n_blocks = (n * TOP_K + ne * (EXPERT_ROWS - 1)) // EXPERT_ROWS
    rows = n_blocks * EXPERT_ROWS
    dest_tok = (pad_start[eid] + rnk).T.reshape(-1).astype(I32)
    pad_info = jnp.stack([pad_start + counts, padded - counts]).astype(I32)
    blk_start = jnp.arange(n_blocks, dtype=I32) * EXPERT_ROWS
    n_used = (pad_end[-1] // EXPERT_ROWS).astype(I32).reshape(1)
    blk_exp = jnp.searchsorted(pad_end, jnp.minimum(blk_start, pad_end[-1] - 1), side='right').astype(I32)
    blk_exp = jnp.minimum(blk_exp, ne - 1)
    xs = _dispatch(pad_info, dest_tok, x1p, n, rows)
    ys = _experts(blk_exp, n_used, xs, wg, wu, wd)
    return _combine(dest_tok, ys, x1, gate.T, sg, su, sd, g, b, alpha)


def kernel(x, mem, w_in, conv_w, w_conv_out, hg_lb_logits, hg_norm_g, w_hg_out, w_mem_k, w_mem_v,
           w_xa_out, w_o, ln1_g, ln1_b, router_w, router_b, exp_wg, exp_wu, exp_wd,
           sh_wg, sh_wu, sh_wd, ln2_g, ln2_b):
    bsz, seq, d = x.shape
    depth = w_in.shape[0]
    n_mem = mem.shape[1]
    cwid = conv_w.shape[2]
    hw = w_hg_out.shape[1]
    xw = w_mem_k.shape[2]
    alpha = float((2 * depth) ** 0.25)
    n = bsz * seq

    lb_all = jnp.cumsum(jax.nn.softmax(hg_lb_logits.astype(F32), axis=0), axis=0)
    lb_all = lb_all - lb_all[0:1]
    o_conv, o_hg, o_xa, o_gate = 0, 3 * cwid, 3 * cwid + 4 * hw, 3 * cwid + 4 * hw + xw

    h = x.reshape(n, d)
    memf = mem.reshape(bsz * n_mem, d)
    for l in range(depth):
        wl = w_in[l]
        w_conv_in = wl[:, o_conv:o_hg].astype(BF16)
        w_hg_in = wl[:, o_hg:o_xa].astype(BF16)
        w_xa_in = wl[:, o_xa:o_gate].astype(BF16)
        w_gates = wl[:, o_gate:].astype(BF16)
        vec = lambda a: a.astype(F32).reshape(1, -1)

        ya = _conv_branch(h, w_conv_in, conv_w[l].astype(F32), seq)
        qfig = _matmul(h, w_hg_in, F32, 1024, hw)
        lb = lb_all[l]
        yb = _hgrn_branch(qfig, vec(jnp.log(lb)), vec(jnp.log1p(-lb)), vec(hg_norm_g[l]), bsz, seq)
        w_kv = jnp.concatenate([w_mem_k[l], w_mem_v[l]], axis=1).astype(BF16)
        kv = _matmul(memf, w_kv, BF16, 1024, xw)
        yx = _attn_branch(h, w_xa_in, kv, seq, n_mem)
        merged = _merge(h, ya, yb, yx, w_gates, w_conv_out[l].astype(BF16), w_hg_out[l].astype(BF16),
                        w_xa_out[l].astype(BF16))
        x1, x1p = _oproj_ln(merged, w_o[l].astype(BF16), h, vec(ln1_g[l]), vec(ln1_b[l]), alpha)
        h = _moe(x1, x1p, router_w[l].astype(F32).T, router_b[l].astype(F32).reshape(-1, 1),
                    exp_wg[l], exp_wu[l], exp_wd[l],
                    sh_wg[l].astype(BF16), sh_wu[l].astype(BF16), sh_wd[l].astype(BF16),
                    vec(ln2_g[l]), vec(ln2_b[l]), alpha)
    return h.reshape(bsz, seq, d)
```

```python
import functools

import jax
import jax.numpy as jnp
from jax import lax
from jax.experimental import pallas as pl
from jax.experimental.pallas import tpu as pltpu

F32 = jnp.float32
BF16 = jnp.bfloat16
I32 = jnp.int32
U32 = jnp.uint32

HG_DK = 128
XA_DH = 128
TOP_K = 8
N_GROUPS = 8
TOPK_GROUPS = 4
ROUTED_SCALE = 2.5
LN_EPS = 1e-5
RMS_EPS = 1e-6
HG_SUB = 16
EXPERT_ROWS = 256
NEG_BIG = -1e30

V7X_VMEM_BYTES = 64 * 1024 * 1024
VMEM_CAP = V7X_VMEM_BYTES - 8 * 1024 * 1024


def _params(semantics, vmem_bytes):
    limit = int(min(VMEM_CAP, max(32 * 1024 * 1024, vmem_bytes * 5 // 4 + (4 << 20))))
    return pltpu.CompilerParams(dimension_semantics=semantics, vmem_limit_bytes=limit)


def _nbytes(shape, dtype):
    n = 1
    for s in shape:
        n *= s
    return n * jnp.dtype(dtype).itemsize


def _tile(n, want):
    t = min(n, want)
    while n % t:
        t //= 2
    return t


def _silu(x):
    return x * (1.0 / (1.0 + jnp.exp(-x)))


def _sigmoid(x):
    return 1.0 / (1.0 + jnp.exp(-x))


def _layer_norm_rows(y, g, b):
    mu = jnp.mean(y, axis=-1, keepdims=True)
    yc = y - mu
    var = jnp.mean(yc * yc, axis=-1, keepdims=True)
    return yc * lax.rsqrt(var + LN_EPS) * g + b


LANES = 128
PAIR = 2 * LANES


def _store_packed_rows(p_ref, y):
    rows, d = y.shape
    pieces = d // PAIR
    for s in range(pieces):
        lo = lax.bitcast_convert_type(y[:, s * PAIR:s * PAIR + LANES].astype(BF16).astype(F32), U32)
        hi = lax.bitcast_convert_type(y[:, s * PAIR + LANES:(s + 1) * PAIR].astype(BF16).astype(F32), U32)
        p_ref[pl.ds(s, rows, stride=pieces), :] = (hi & jnp.uint32(0xFFFF0000)) | (lo >> 16)


def _load_packed_rows(p_ref, rows, d):
    pieces = d // PAIR
    cols = []
    for s in range(pieces):
        w = p_ref[pl.ds(s, rows, stride=pieces), :]
        cols.append(lax.bitcast_convert_type(w << 16, F32).astype(BF16))
        cols.append(lax.bitcast_convert_type(w & jnp.uint32(0xFFFF0000), F32).astype(BF16))
    return jnp.concatenate(cols, axis=1)


def _matmul_kernel(x_ref, w_ref, o_ref):
    o_ref[...] = jnp.dot(x_ref[...].astype(BF16), w_ref[...],
                         preferred_element_type=F32).astype(o_ref.dtype)


def _matmul(x, w, out_dtype, tm, tn):
    m, k = x.shape
    n = w.shape[1]
    tm, tn = _tile(m, tm), _tile(n, tn)
    vmem = 2 * (_nbytes((tm, k), x.dtype) + _nbytes((k, tn), w.dtype) + _nbytes((tm, tn), out_dtype)) \
        + _nbytes((tm, k), BF16) + _nbytes((tm, tn), F32)
    return pl.pallas_call(
        _matmul_kernel,
        out_shape=jax.ShapeDtypeStruct((m, n), out_dtype),
        grid=(m // tm, n // tn),
        in_specs=[pl.BlockSpec((tm, k), lambda i, j: (i, 0)),
                  pl.BlockSpec((k, tn), lambda i, j: (0, j))],
        out_specs=pl.BlockSpec((tm, tn), lambda i, j: (i, j)),
        compiler_params=_params(("arbitrary", "arbitrary"), vmem),
    )(x, w)


def _conv_kernel(x_ref, w_ref, cw_ref, o_ref, carry_ref, *, tiles_per_seq):
    i = pl.program_id(0)
    tm, cwid = o_ref.shape

    @pl.when(i % tiles_per_seq == 0)
    def _():
        carry_ref[...] = jnp.zeros_like(carry_ref)

    p = jnp.dot(x_ref[...].astype(BF16), w_ref[...], preferred_element_type=F32)
    u = p[:, :cwid]
    b = p[:, cwid:2 * cwid]
    c = p[:, 2 * cwid:]
    cu = c * u
    prev = carry_ref[...]
    row = lax.broadcasted_iota(I32, (tm, cwid), 0)
    cu1 = jnp.where(row == 0, prev[7:8, :], pltpu.roll(cu, 1, 0))
    cu2 = pltpu.roll(cu, 2, 0)
    cu2 = jnp.where(row == 0, prev[6:7, :], jnp.where(row == 1, prev[7:8, :], cu2))
    cw = cw_ref[...]
    y = b * (cw[0:1, :] * cu2 + cw[1:2, :] * cu1 + cw[2:3, :] * cu)
    o_ref[...] = y.astype(o_ref.dtype)
    carry_ref[...] = cu[tm - 8:, :]


def _conv_branch(x, w_conv_in, conv_w, seq):
    n, d = x.shape
    cwid = conv_w.shape[1]
    assert conv_w.shape[0] == 3
    tm = _tile(seq, 512)
    vmem = 2 * (_nbytes((tm, d), x.dtype) + _nbytes(w_conv_in.shape, BF16) + _nbytes((tm, cwid), BF16)) \
        + 6 * _nbytes((tm, 3 * cwid), F32)
    return pl.pallas_call(
        functools.partial(_conv_kernel, tiles_per_seq=seq // tm),
        out_shape=jax.ShapeDtypeStruct((n, cwid), BF16),
        grid=(n // tm,),
        in_specs=[pl.BlockSpec((tm, d), lambda i: (i, 0)),
                  pl.BlockSpec(w_conv_in.shape, lambda i: (0, 0)),
                  pl.BlockSpec(conv_w.shape, lambda i: (0, 0))],
        out_specs=pl.BlockSpec((tm, cwid), lambda i: (i, 0)),
        scratch_shapes=[pltpu.VMEM((8, cwid), F32)],
        compiler_params=_params(("arbitrary",), vmem),
    )(x, w_conv_in, conv_w)


def _head_sum(x):
    return jnp.sum(x, axis=-1, keepdims=True)


def _hgrn_kernel(q_ref, f_ref, v_ref, g_ref, c1_ref, c2_ref, gain_ref, o_ref,
                 st_ref, qd_ref, kd_ref, vb_ref, dec_ref, oi_ref, *, heads):
    tt, hw = q_ref.shape
    sub = HG_SUB
    dk = HG_DK

    @pl.when(pl.program_id(1) == 0)
    def _():
        st_ref[...] = jnp.zeros_like(st_ref)

    f = f_ref[...]
    log_sig = jnp.minimum(f, 0.0) - jnp.log1p(jnp.exp(-jnp.abs(f)))
    a1 = c1_ref[...]
    a2 = c2_ref[...] + log_sig
    log_f = jnp.maximum(a1, a2) + jnp.log1p(jnp.exp(-jnp.abs(a1 - a2)))
    kin = 1.0 - jnp.exp(log_f)

    r = lax.broadcasted_iota(I32, (tt, tt), 0)
    c = lax.broadcasted_iota(I32, (tt, tt), 1)
    same = (r // sub) == (c // sub)
    tri = jnp.where(same & (c <= r), 1.0, 0.0).astype(F32)
    blk = jnp.where(same, 1.0, 0.0).astype(F32)
    bcum = jnp.dot(tri, log_f, preferred_element_type=F32, precision=lax.Precision.HIGHEST)
    blast = jnp.dot(blk, log_f, preferred_element_type=F32, precision=lax.Precision.HIGHEST)

    q = q_ref[...]
    qd_ref[...] = (q * jnp.exp(bcum)).astype(BF16)
    kd_ref[...] = (kin * jnp.exp(blast - bcum)).astype(BF16)
    vb_ref[...] = v_ref[...].astype(BF16)
    dec_ref[...] = jnp.exp(blast)

    def chunk(j, carry):
        r0 = pl.multiple_of(j * sub, sub)
        for h in range(heads):
            ls = slice(h * dk, (h + 1) * dk)
            st = st_ref[h]
            qd = qd_ref[pl.ds(r0, sub), ls]
            oi_ref[pl.ds(r0, sub), ls] = lax.dot_general(
                qd, st.astype(BF16), (((1,), (1,)), ((), ())), preferred_element_type=F32)
            upd = lax.dot_general(vb_ref[pl.ds(r0, sub), ls], kd_ref[pl.ds(r0, sub), ls],
                                  (((0,), (0,)), ((), ())), preferred_element_type=F32)
            st_ref[h] = st * dec_ref[pl.ds(r0, 1), ls] + upd
        return carry

    lax.fori_loop(0, tt // sub, chunk, 0)

    pos = lax.broadcasted_iota(I32, (tt, dk), 0) % sub
    gain = gain_ref[...]
    for h in range(heads):
        ls = slice(h * dk, (h + 1) * dk)
        qh = q[:, ls]
        kh = kin[:, ls]
        bh = bcum[:, ls]
        vh = v_ref[:, ls]
        acc = oi_ref[:, ls] + _head_sum(qh * kh) * vh
        for d in range(1, sub):
            ok = pos >= d
            arg = jnp.where(ok, bh - pltpu.roll(bh, d, 0), NEG_BIG)
            sc = _head_sum(qh * pltpu.roll(kh, d, 0) * jnp.exp(arg))
            acc = acc + sc * pltpu.roll(vh, d, 0)
        ms = _head_sum(acc * acc) * (1.0 / dk)
        y = acc * lax.rsqrt(ms + RMS_EPS) * gain[:, ls] * _silu(g_ref[:, ls])
        o_ref[:, ls] = y.astype(o_ref.dtype)


def _hgrn_branch(qfig, c1, c2, gain, bsz, seq):
    n = qfig.shape[0]
    hw = qfig.shape[1] // 4
    heads = hw // HG_DK
    tt = _tile(seq, 256)
    nt = seq // tt
    blk = lambda col: pl.BlockSpec((tt, hw), lambda b, t, col=col: (b * nt + t, col))
    vec = pl.BlockSpec((1, hw), lambda b, t: (0, 0))
    vmem = 2 * 5 * _nbytes((tt, hw), F32) + 12 * _nbytes((tt, hw), F32) + heads * HG_DK * HG_DK * 4
    return pl.pallas_call(
        functools.partial(_hgrn_kernel, heads=heads),
        out_shape=jax.ShapeDtypeStruct((n, hw), BF16),
        grid=(bsz, nt),
        in_specs=[blk(0), blk(1), blk(2), blk(3), vec, vec, vec],
        out_specs=pl.BlockSpec((tt, hw), lambda b, t: (b * nt + t, 0)),
        scratch_shapes=[pltpu.VMEM((heads, HG_DK, HG_DK), F32),
                        pltpu.VMEM((tt, hw), BF16), pltpu.VMEM((tt, hw), BF16),
                        pltpu.VMEM((tt, hw), BF16), pltpu.VMEM((tt, hw), F32),
                        pltpu.VMEM((tt, hw), F32)],
        compiler_params=_params(("arbitrary", "arbitrary"), vmem),
    )(qfig, qfig, qfig, qfig, c1, c2, gain)


def _attn_kernel(x_ref, wq_ref, k_ref, v_ref, o_ref, *, heads):
    dh = XA_DH
    q = jnp.dot(x_ref[...].astype(BF16), wq_ref[...], preferred_element_type=F32)
    scale = dh ** -0.5
    for h in range(heads):
        ls = slice(h * dh, (h + 1) * dh)
        s = lax.dot_general(q[:, ls].astype(BF16), k_ref[:, ls], (((1,), (1,)), ((), ())),
                            preferred_element_type=F32) * scale
        e = jnp.exp(s - jnp.max(s, axis=-1, keepdims=True))
        p = e / jnp.sum(e, axis=-1, keepdims=True)
        o = jnp.dot(p.astype(BF16), v_ref[:, ls], preferred_element_type=F32)
        o_ref[:, ls] = o.astype(o_ref.dtype)


def _attn_branch(x, wq, kv, seq, n_mem):
    n, d = x.shape
    xw = wq.shape[1]
    heads = xw // XA_DH
    tm = _tile(seq, 512)
    per_seq = seq // tm
    vmem = 2 * (_nbytes((tm, d), x.dtype) + _nbytes(wq.shape, BF16) + 2 * _nbytes((n_mem, xw), BF16)
                + _nbytes((tm, xw), BF16)) + 4 * _nbytes((tm, xw), F32) + 4 * _nbytes((tm, n_mem), F32)
    return pl.pallas_call(
        functools.partial(_attn_kernel, heads=heads),
        out_shape=jax.ShapeDtypeStruct((n, xw), BF16),
        grid=(n // tm,),
        in_specs=[pl.BlockSpec((tm, d), lambda i: (i, 0)),
                  pl.BlockSpec(wq.shape, lambda i: (0, 0)),
                  pl.BlockSpec((n_mem, xw), lambda i: (i // per_seq, 0)),
                  pl.BlockSpec((n_mem, xw), lambda i: (i // per_seq, 1))],
        out_specs=pl.BlockSpec((tm, xw), lambda i: (i, 0)),
        compiler_params=_params(("arbitrary",), vmem),
    )(x, wq, kv, kv)


def _merge_kernel(x_ref, ya_ref, yb_ref, yx_ref, wga_ref, wgb_ref, wgx_ref, wa_ref, wb_ref, wx_ref,
                  o_ref, xb_ref):
    @pl.when(pl.program_id(1) == 0)
    def _():
        xb_ref[...] = x_ref[...].astype(BF16)

    xb = xb_ref[...]
    dot = functools.partial(jnp.dot, preferred_element_type=F32)
    m = _sigmoid(dot(xb, wga_ref[...])) * dot(ya_ref[...], wa_ref[...])
    m = m + _sigmoid(dot(xb, wgb_ref[...])) * dot(yb_ref[...], wb_ref[...])
    m = m + _sigmoid(dot(xb, wgx_ref[...])) * dot(yx_ref[...], wx_ref[...])
    o_ref[...] = m.astype(o_ref.dtype)


def _merge(x, ya, yb, yx, w_gates, wa, wb, wx):
    n, d = x.shape
    tm, tn = _tile(n, 1024), _tile(d, 512)
    nj = d // tn
    row = lambda a: pl.BlockSpec((tm, a.shape[1]), lambda i, j: (i, 0))
    gate = lambda g: pl.BlockSpec((d, tn), lambda i, j, g=g: (0, g * nj + j))
    outw = lambda a: pl.BlockSpec((a.shape[0], tn), lambda i, j: (0, j))
    vmem = 2 * (_nbytes((tm, d), x.dtype) + _nbytes((tm, ya.shape[1] + yb.shape[1] + yx.shape[1]), BF16)
                + 3 * _nbytes((d, tn), BF16) + _nbytes((wa.shape[0] + wb.shape[0] + wx.shape[0], tn), BF16)
                + _nbytes((tm, tn), BF16)) + _nbytes((tm, d), BF16) + 8 * _nbytes((tm, tn), F32)
    return pl.pallas_call(
        _merge_kernel,
        out_shape=jax.ShapeDtypeStruct((n, d), BF16),
        grid=(n // tm, nj),
        in_specs=[row(x), row(ya), row(yb), row(yx), gate(0), gate(1), gate(2),
                  outw(wa), outw(wb), outw(wx)],
        out_specs=pl.BlockSpec((tm, tn), lambda i, j: (i, j)),
        scratch_shapes=[pltpu.VMEM((tm, d), BF16)],
        compiler_params=_params(("arbitrary", "arbitrary"), vmem),
    )(x, ya, yb, yx, w_gates, w_gates, w_gates, wa, wb, wx)


def _oproj_ln_kernel(m_ref, w_ref, x_ref, g_ref, b_ref, o_ref, p_ref, *, alpha):
    y = alpha * x_ref[...] + jnp.dot(m_ref[...], w_ref[...], preferred_element_type=F32)
    y = _layer_norm_rows(y, g_ref[...], b_ref[...])
    o_ref[...] = y
    _store_packed_rows(p_ref, y)


def _oproj_ln(merged, w_o, x, g, b, alpha):
    n, d = x.shape
    assert d % PAIR == 0
    pieces = d // PAIR
    tm = _tile(n, 512)
    vec = pl.BlockSpec((1, d), lambda i: (0, 0))
    vmem = 2 * (_nbytes((tm, d), BF16) + _nbytes((d, d), BF16) + 2 * _nbytes((tm, d), F32)
                + _nbytes((tm, d // 2), U32)) + 4 * _nbytes((tm, d), F32)
    return pl.pallas_call(
        functools.partial(_oproj_ln_kernel, alpha=alpha),
        out_shape=(jax.ShapeDtypeStruct((n, d), F32), jax.ShapeDtypeStruct((n * pieces, LANES), U32)),
        grid=(n // tm,),
        in_specs=[pl.BlockSpec((tm, d), lambda i: (i, 0)),
                  pl.BlockSpec((d, d), lambda i: (0, 0)),
                  pl.BlockSpec((tm, d), lambda i: (i, 0)), vec, vec],
        out_specs=(pl.BlockSpec((tm, d), lambda i: (i, 0)),
                   pl.BlockSpec((tm * pieces, LANES), lambda i: (i, 0))),
        compiler_params=_params(("arbitrary",), vmem),
    )(merged, w_o, x, g, b)


def _rank_desc(vals):
    rows = vals.shape[0]
    iota = lax.broadcasted_iota(I32, vals.shape, 0)
    rank = jnp.zeros(vals.shape, I32)
    for j in range(rows):
        rowv = vals[j:j + 1, :]
        tie = jnp.where(iota > j, 1, 0)
        rank = rank + jnp.where(rowv > vals, 1, jnp.where(rowv == vals, tie, 0))
    return rank


def _router_kernel(x_ref, rw_ref, rb_ref, eid_ref, rnk_ref, gate_ref, cnt_ref, carry_ref):
    ne = rw_ref.shape[0]
    tm = x_ref.shape[0]
    gsz = ne // N_GROUPS

    @pl.when(pl.program_id(0) == 0)
    def _():
        carry_ref[...] = jnp.zeros_like(carry_ref)

    logits = lax.dot_general(rw_ref[...], x_ref[...], (((1,), (1,)), ((), ())),
                             preferred_element_type=F32, precision=lax.Precision.HIGHEST)
    s = _sigmoid(logits)
    sel = s + rb_ref[...]

    grp = sel.reshape(N_GROUPS, gsz, tm)
    sub_iota = lax.broadcasted_iota(I32, (N_GROUPS, gsz, tm), 1)
    m1 = jnp.max(grp, axis=1, keepdims=True)
    first = jnp.min(jnp.where(grp == m1, sub_iota, gsz), axis=1, keepdims=True)
    m2 = jnp.max(jnp.where(sub_iota == first, -jnp.inf, grp), axis=1, keepdims=True)
    gscore = (m1 + m2).reshape(N_GROUPS, tm)

    gsel = _rank_desc(gscore) < TOPK_GROUPS
    emask = jnp.broadcast_to(gsel.reshape(N_GROUPS, 1, tm), (N_GROUPS, gsz, tm)).reshape(ne, tm)
    masked = jnp.where(emask, sel, -jnp.inf)

    eiota = lax.broadcasted_iota(I32, (ne, tm), 0)
    chosen = (_rank_desc(masked) < TOP_K) & emask

    w = jnp.where(chosen, s, 0.0)
    gate = w / jnp.sum(w, axis=0, keepdims=True) * ROUTED_SCALE

    ch = jnp.where(chosen, 1.0, 0.0).astype(BF16)
    tr = lax.broadcasted_iota(I32, (tm, tm), 0)
    tc = lax.broadcasted_iota(I32, (tm, tm), 1)
    before = jnp.where(tr < tc, 1.0, 0.0).astype(BF16)
    rank_tok = (carry_ref[...] + jnp.dot(ch, before, preferred_element_type=F32)).astype(I32)
    carry_ref[...] = carry_ref[...] + jnp.sum(jnp.where(chosen, 1.0, 0.0), axis=1, keepdims=True)
    cnt_ref[...] = jnp.broadcast_to(carry_ref[...], cnt_ref.shape).astype(I32)

    er = lax.broadcasted_iota(I32, (ne, ne), 0)
    ec = lax.broadcasted_iota(I32, (ne, ne), 1)
    lower = jnp.where(ec < er, 1.0, 0.0).astype(BF16)
    slot = jnp.dot(lower, ch, preferred_element_type=F32).astype(I32)
    for k in range(TOP_K):
        pick = chosen & (slot == k)
        eid_ref[k:k + 1, :] = jnp.sum(jnp.where(pick, eiota, 0), axis=0, keepdims=True)
        rnk_ref[k:k + 1, :] = jnp.sum(jnp.where(pick, rank_tok, 0), axis=0, keepdims=True)
        gate_ref[k:k + 1, :] = jnp.sum(jnp.where(pick, gate, 0.0), axis=0, keepdims=True)


def _router(x, rw_t, rb):
    n, d = x.shape
    ne = rw_t.shape[0]
    tm = _tile(n, 512)
    out = lambda dt: jax.ShapeDtypeStruct((TOP_K, n), dt)
    ospec = pl.BlockSpec((TOP_K, tm), lambda i: (0, i))
    vmem = 2 * (_nbytes((tm, d), F32) + _nbytes((ne, d), F32)) + 3 * _nbytes((tm, tm), F32) \
        + 24 * _nbytes((ne, tm), F32)
    return pl.pallas_call(
        _router_kernel,
        out_shape=(out(I32), out(I32), out(F32), jax.ShapeDtypeStruct((ne, 128), I32)),
        grid=(n // tm,),
        in_specs=[pl.BlockSpec((tm, d), lambda i: (i, 0)),
                  pl.BlockSpec((ne, d), lambda i: (0, 0)),
                  pl.BlockSpec((ne, 1), lambda i: (0, 0))],
        out_specs=(ospec, ospec, ospec, pl.BlockSpec((ne, 128), lambda i: (0, 0))),
        scratch_shapes=[pltpu.VMEM((ne, 1), F32)],
        compiler_params=_params(("arbitrary",), vmem),
    )(x, rw_t, rb)


def _pad_pieces(max_pad):
    pieces = []
    p = 1
    while p <= max_pad:
        pieces.append(p)
        p *= 2
    return pieces[::-1]


def _dispatch_kernel(pad_ref, dest_hbm, xp_ref, zero_hbm, xs_hbm, idx_ref, sem_idx, sem_row, sem_pad,
                     *, tq, ne, rp):
    i = pl.program_id(0)

    def row_copy(src_row, dst_row):
        return pltpu.make_async_copy(xp_ref.at[pl.ds(pl.multiple_of(src_row * rp, rp), rp)],
                                     xs_hbm.at[pl.ds(pl.multiple_of(dst_row * rp, rp), rp)], sem_row)

    def pad_copies(fn):
        def per_expert(e, carry):
            row = pad_ref[0, e]
            npad = pad_ref[1, e]
            for p in _pad_pieces(EXPERT_ROWS - 1):
                @pl.when((npad & p) != 0)
                def _(row=row, p=p):
                    fn(pltpu.make_async_copy(zero_hbm.at[pl.ds(0, p * rp)],
                                             xs_hbm.at[pl.ds(pl.multiple_of(row * rp, rp), p * rp)], sem_pad))
                row = row + (npad & p)
            return carry
        lax.fori_loop(0, ne, per_expert, 0)

    @pl.when(i == 0)
    def _():
        pad_copies(lambda cp: cp.start())

    idx_cp = pltpu.make_async_copy(dest_hbm.at[i], idx_ref, sem_idx)
    idx_cp.start()
    idx_cp.wait()

    def per_token(t, carry):
        for k in range(TOP_K):
            row_copy(t, idx_ref[t * TOP_K + k]).start()
        return carry

    lax.fori_loop(0, tq, per_token, 0)

    def drain(t, carry):
        for k in range(TOP_K):
            row_copy(0, 0).wait()
        return carry

    lax.fori_loop(0, tq, drain, 0)

    @pl.when(i == 0)
    def _():
        pad_copies(lambda cp: cp.wait())


def _dispatch(pad_info, dest_tok, xp, n, rows):
    rp = xp.shape[0] // n
    ne = pad_info.shape[1]
    tq = _tile(n, 1024)
    steps = n // tq
    dest2 = dest_tok.reshape(steps, tq * TOP_K)
    zero = jnp.zeros((EXPERT_ROWS * rp, LANES), U32)
    any_spec = pl.BlockSpec(memory_space=pl.ANY)
    grid_spec = pltpu.PrefetchScalarGridSpec(
        num_scalar_prefetch=1, grid=(steps,),
        in_specs=[any_spec, pl.BlockSpec((tq * rp, LANES), lambda i, pad: (i, 0)), any_spec],
        out_specs=any_spec,
        scratch_shapes=[pltpu.SMEM((tq * TOP_K,), I32), pltpu.SemaphoreType.DMA(()),
                        pltpu.SemaphoreType.DMA(()), pltpu.SemaphoreType.DMA(())])
    return pl.pallas_call(
        functools.partial(_dispatch_kernel, tq=tq, ne=ne, rp=rp),
        out_shape=jax.ShapeDtypeStruct((rows * rp, LANES), U32),
        grid_spec=grid_spec,
        compiler_params=_params(("arbitrary",), 2 * _nbytes((tq * rp, LANES), U32)),
    )(pad_info, dest2, xp, zero)


def _expert_kernel(be_ref, nu_ref, xs_ref, wg_ref, wu_ref, wd_ref, ys_ref, wgb_ref, wub_ref, wdb_ref):
    b = pl.program_id(0)

    @pl.when(b < nu_ref[0])
    def _():
        prev = be_ref[jnp.maximum(b - 1, 0)]

        @pl.when((b == 0) | (be_ref[b] != prev))
        def _():
            wgb_ref[...] = wg_ref[...].astype(BF16)
            wub_ref[...] = wu_ref[...].astype(BF16)
            wdb_ref[...] = wd_ref[...].astype(BF16)

        t = EXPERT_ROWS
        d = wgb_ref.shape[0]
        xb = _load_packed_rows(xs_ref, t, d)
        dot = functools.partial(jnp.dot, preferred_element_type=F32)
        hmid = (_silu(dot(xb, wgb_ref[...])) * dot(xb, wub_ref[...])).astype(BF16)
        y = dot(hmid, wdb_ref[...])
        q = d // LANES
        for s in range(q):
            ys_ref[pl.ds(s, t, stride=q), :] = y[:, s * LANES:(s + 1) * LANES]


def _experts(blk_exp, n_used, xs, wg, wu, wd, layer):
    _, ne, d, hid = wg.shape
    t = EXPERT_ROWS
    rp, rq = d // PAIR, d // LANES
    rows = xs.shape[0] // rp
    nb = rows // t
    clamp = lambda b, nu: jnp.minimum(b, jnp.maximum(nu[0] - 1, 0))
    wspec = lambda shape: pl.BlockSpec((None, None) + shape,
                                       lambda b, be, nu: (layer, be[clamp(b, nu)], 0, 0))
    grid_spec = pltpu.PrefetchScalarGridSpec(
        num_scalar_prefetch=2, grid=(nb,),
        in_specs=[pl.BlockSpec((t * rp, LANES), lambda b, be, nu: (clamp(b, nu), 0)),
                  wspec((d, hid)), wspec((d, hid)), wspec((hid, d))],
        out_specs=pl.BlockSpec((t * rq, LANES), lambda b, be, nu: (clamp(b, nu), 0)),
        scratch_shapes=[pltpu.VMEM((d, hid), BF16), pltpu.VMEM((d, hid), BF16), pltpu.VMEM((hid, d), BF16)])
    vmem = 2 * (3 * _nbytes((d, hid), F32) + _nbytes((t, d // 2), U32) + _nbytes((t, d), F32)) \
        + 3 * _nbytes((d, hid), BF16) + 4 * _nbytes((t, d), F32)
    return pl.pallas_call(
        _expert_kernel,
        out_shape=jax.ShapeDtypeStruct((rows * rq, LANES), F32),
        grid_spec=grid_spec,
        compiler_params=_params(("arbitrary",), vmem),
    )(blk_exp, n_used, xs, wg, wu, wd)


def _combine_kernel(dest_hbm, ys_hbm, x_ref, gate_ref, sg_ref, su_ref, sd_ref, g_ref, b_ref, o_ref,
                    idx_ref, buf_ref, sem_idx, sem_row, *, tc, alpha):
    i = pl.program_id(0)
    rq = x_ref.shape[1] // LANES
    idx_cp = pltpu.make_async_copy(dest_hbm.at[i], idx_ref, sem_idx)
    idx_cp.start()
    idx_cp.wait()

    def row_copy(src_row, dst_row):
        return pltpu.make_async_copy(ys_hbm.at[pl.ds(pl.multiple_of(src_row * rq, rq), rq)],
                                     buf_ref.at[pl.ds(pl.multiple_of(dst_row * rq, rq), rq)], sem_row)

    def per_token(t, carry):
        for k in range(TOP_K):
            row_copy(idx_ref[t * TOP_K + k], k * tc + t).start()
        return carry

    lax.fori_loop(0, tc, per_token, 0)

    x = x_ref[...]
    xb = x.astype(BF16)
    dot = functools.partial(jnp.dot, preferred_element_type=F32)
    hmid = (_silu(dot(xb, sg_ref[...])) * dot(xb, su_ref[...])).astype(BF16)
    shared = dot(hmid, sd_ref[...])

    def drain(t, carry):
        for k in range(TOP_K):
            row_copy(0, 0).wait()
        return carry

    lax.fori_loop(0, tc, drain, 0)

    gate = gate_ref[...]
    cols = []
    for s in range(rq):
        piece = lambda k: buf_ref[pl.ds(k * tc * rq + s, tc, stride=rq), :]
        acc = piece(0) * gate[:, 0:1]
        for k in range(1, TOP_K):
            acc = acc + piece(k) * gate[:, k:k + 1]
        cols.append(acc)
    routed = jnp.concatenate(cols, axis=1)
    o_ref[...] = _layer_norm_rows(alpha * x + (shared + routed), g_ref[...], b_ref[...])


def _combine(dest_tok, ys, x, gate_tok, sg, su, sd, g, b, alpha):
    n, d = x.shape
    hid = sg.shape[1]
    tc = _tile(n, 128)
    steps = n // tc
    dest2 = dest_tok.reshape(steps, tc * TOP_K)
    any_spec = pl.BlockSpec(memory_space=pl.ANY)
    vec = pl.BlockSpec((1, d), lambda i: (0, 0))
    full = lambda a: pl.BlockSpec(a.shape, lambda i: (0, 0))
    vmem = _nbytes((TOP_K, tc, d), F32) + 2 * (2 * _nbytes((tc, d), F32) + 3 * _nbytes((d, hid), BF16)) \
        + 6 * _nbytes((tc, d), F32)
    return pl.pallas_call(
        functools.partial(_combine_kernel, tc=tc, alpha=alpha),
        out_shape=jax.ShapeDtypeStruct((n, d), F32),
        grid=(steps,),
        in_specs=[any_spec, any_spec,
                  pl.BlockSpec((tc, d), lambda i: (i, 0)),
                  pl.BlockSpec((tc, TOP_K), lambda i: (i, 0)),
                  full(sg), full(su), full(sd), vec, vec],
        out_specs=pl.BlockSpec((tc, d), lambda i: (i, 0)),
        scratch_shapes=[pltpu.SMEM((tc * TOP_K,), I32), pltpu.VMEM((TOP_K * tc * (d // LANES), LANES), F32),
                        pltpu.SemaphoreType.DMA(()), pltpu.SemaphoreType.DMA(())],
        compiler_params=_params(("arbitrary",), vmem),
    )(dest2, ys, x, gate_tok, sg, su, sd, g, b)


def _moe(x1, x1p, rw_t, rb, wg, wu, wd, layer, sg, su, sd, g, b, alpha):
    n, d = x1.shape
    ne = rw_t.shape[0]
    eid, rnk, gate, cnt = _router(x1, rw_t, rb)
    counts = cnt[:, 0]
    padded = (counts + EXPERT_ROWS - 1) // EXPERT_ROWS * EXPERT_ROWS
    eidx = jnp.arange(ne, dtype=I32)
    pad_end = jnp.sum(jnp.where(eidx[None, :] <= eidx[:, None], padded[None, :], 0), axis=1)
    pad_start = pad_end - padded
    n_blocks = (n * TOP_K + ne * (EXPERT_ROWS - 1)) // EXPERT_ROWS
    rows = n_blocks * EXPERT_ROWS
    start_of = jnp.sum(jnp.where(eid[:, :, None] == eidx, pad_start, 0), axis=-1)
    dest_tok = (start_of + rnk).T.reshape(-1).astype(I32)
    pad_info = jnp.stack([pad_start + counts, padded - counts]).astype(I32)
    total = pad_end[ne - 1]
    n_used = (total // EXPERT_ROWS).astype(I32).reshape(1)
    blk_start = jnp.minimum(jnp.arange(n_blocks, dtype=I32) * EXPERT_ROWS, total - 1)
    blk_exp = jnp.sum((pad_end[None, :] <= blk_start[:, None]).astype(I32), axis=1)
    blk_exp = jnp.minimum(blk_exp, ne - 1).astype(I32)
    xs = _dispatch(pad_info, dest_tok, x1p, n, rows)
    ys = _experts(blk_exp, n_used, xs, wg, wu, wd, layer)
    return _combine(dest_tok, ys, x1, gate.T, sg, su, sd, g, b, alpha)


def kernel(x, mem, w_in, conv_w, w_conv_out, hg_lb_logits, hg_norm_g, w_hg_out, w_mem_k, w_mem_v,
           w_xa_out, w_o, ln1_g, ln1_b, router_w, router_b, exp_wg, exp_wu, exp_wd,
           sh_wg, sh_wu, sh_wd, ln2_g, ln2_b):
    bsz, seq, d = x.shape
    depth = w_in.shape[0]
    n_mem = mem.shape[1]
    cwid = conv_w.shape[2]
    hw = w_hg_out.shape[1]
    xw = w_mem_k.shape[2]
    alpha = float((2 * depth) ** 0.25)
    n = bsz * seq

    lb_all = jnp.cumsum(jax.nn.softmax(hg_lb_logits.astype(F32), axis=0), axis=0)
    lb_all = lb_all - lb_all[0:1]
    o_conv, o_hg, o_xa, o_gate = 0, 3 * cwid, 3 * cwid + 4 * hw, 3 * cwid + 4 * hw + xw

    h = x.reshape(n, d)
    memf = mem.reshape(bsz * n_mem, d)
    for l in range(depth):
        wl = w_in[l]
        w_conv_in = wl[:, o_conv:o_hg].astype(BF16)
        w_hg_in = wl[:, o_hg:o_xa].astype(BF16)
        w_xa_in = wl[:, o_xa:o_gate].astype(BF16)
        w_gates = wl[:, o_gate:].astype(BF16)
        vec = lambda a: a.astype(F32).reshape(1, -1)

        ya = _conv_branch(h, w_conv_in, conv_w[l].astype(F32), seq)
        qfig = _matmul(h, w_hg_in, F32, 1024, hw)
        lb = lb_all[l]
        yb = _hgrn_branch(qfig, vec(jnp.log(lb)), vec(jnp.log1p(-lb)), vec(hg_norm_g[l]), bsz, seq)
        w_kv = jnp.concatenate([w_mem_k[l], w_mem_v[l]], axis=1).astype(BF16)
        kv = _matmul(memf, w_kv, BF16, 1024, xw)
        yx = _attn_branch(h, w_xa_in, kv, seq, n_mem)
        merged = _merge(h, ya, yb, yx, w_gates, w_conv_out[l].astype(BF16), w_hg_out[l].astype(BF16),
                        w_xa_out[l].astype(BF16))
        x1, x1p = _oproj_ln(merged, w_o[l].astype(BF16), h, vec(ln1_g[l]), vec(ln1_b[l]), alpha)
        h = _moe(x1, x1p, router_w[l].astype(F32).T, router_b[l].astype(F32).reshape(-1, 1),
                    exp_wg, exp_wu, exp_wd, l,
                    sh_wg[l].astype(BF16), sh_wu[l].astype(BF16), sh_wd[l].astype(BF16),
                    vec(ln2_g[l]), vec(ln2_b[l]), alpha)
    return h.reshape(bsz, seq, d)
```

```python
import functools

import jax
import jax.numpy as jnp
from jax import lax
from jax.experimental import pallas as pl
from jax.experimental.pallas import tpu as pltpu

F32 = jnp.float32
BF16 = jnp.bfloat16
I32 = jnp.int32
U32 = jnp.uint32

HG_DK = 128
XA_DH = 128
TOP_K = 8
N_GROUPS = 8
TOPK_GROUPS = 4
ROUTED_SCALE = 2.5
LN_EPS = 1e-5
RMS_EPS = 1e-6
HG_SUB = 16
EXPERT_ROWS = 256
NEG_BIG = -1e30

V7X_VMEM_BYTES = 64 * 1024 * 1024
VMEM_CAP = V7X_VMEM_BYTES - 8 * 1024 * 1024


def _params(semantics, vmem_bytes):
    limit = int(min(VMEM_CAP, max(32 * 1024 * 1024, vmem_bytes * 5 // 4 + (4 << 20))))
    return pltpu.CompilerParams(dimension_semantics=semantics, vmem_limit_bytes=limit)


def _nbytes(shape, dtype):
    n = 1
    for s in shape:
        n *= s
    return n * jnp.dtype(dtype).itemsize


def _tile(n, want):
    t = min(n, want)
    while n % t:
        t //= 2
    return t


def _silu(x):
    return x * (1.0 / (1.0 + jnp.exp(-x)))


def _sigmoid(x):
    return 1.0 / (1.0 + jnp.exp(-x))


def _layer_norm_rows(y, g, b):
    mu = jnp.mean(y, axis=-1, keepdims=True)
    yc = y - mu
    var = jnp.mean(yc * yc, axis=-1, keepdims=True)
    return yc * lax.rsqrt(var + LN_EPS) * g + b


LANES = 128
PAIR = 2 * LANES


def _store_packed_rows(p_ref, y):
    rows, d = y.shape
    pieces = d // PAIR
    for s in range(pieces):
        lo = lax.bitcast_convert_type(y[:, s * PAIR:s * PAIR + LANES].astype(BF16).astype(F32), U32)
        hi = lax.bitcast_convert_type(y[:, s * PAIR + LANES:(s + 1) * PAIR].astype(BF16).astype(F32), U32)
        p_ref[pl.ds(s, rows, stride=pieces), :] = (hi & jnp.uint32(0xFFFF0000)) | (lo >> 16)


def _load_packed_rows(p_ref, rows, d):
    pieces = d // PAIR
    cols = []
    for s in range(pieces):
        w = p_ref[pl.ds(s, rows, stride=pieces), :]
        cols.append(lax.bitcast_convert_type(w << 16, F32).astype(BF16))
        cols.append(lax.bitcast_convert_type(w & jnp.uint32(0xFFFF0000), F32).astype(BF16))
    return jnp.concatenate(cols, axis=1)


def _matmul_kernel(x_ref, w_ref, o_ref):
    o_ref[...] = jnp.dot(x_ref[...].astype(BF16), w_ref[...],
                         preferred_element_type=F32).astype(o_ref.dtype)


def _matmul(x, w, out_dtype, tm, tn):
    m, k = x.shape
    n = w.shape[1]
    tm, tn = _tile(m, tm), _tile(n, tn)
    vmem = 2 * (_nbytes((tm, k), x.dtype) + _nbytes((k, tn), w.dtype) + _nbytes((tm, tn), out_dtype)) \
        + _nbytes((tm, k), BF16) + _nbytes((tm, tn), F32)
    return pl.pallas_call(
        _matmul_kernel,
        out_shape=jax.ShapeDtypeStruct((m, n), out_dtype),
        grid=(m // tm, n // tn),
        in_specs=[pl.BlockSpec((tm, k), lambda i, j: (i, 0)),
                  pl.BlockSpec((k, tn), lambda i, j: (0, j))],
        out_specs=pl.BlockSpec((tm, tn), lambda i, j: (i, j)),
        compiler_params=_params(("arbitrary", "arbitrary"), vmem),
    )(x, w)


def _conv_kernel(x_ref, w_ref, cw_ref, o_ref, carry_ref, *, tiles_per_seq):
    i = pl.program_id(0)
    tm, cwid = o_ref.shape

    @pl.when(i % tiles_per_seq == 0)
    def _():
        carry_ref[...] = jnp.zeros_like(carry_ref)

    p = jnp.dot(x_ref[...].astype(BF16), w_ref[...], preferred_element_type=F32)
    u = p[:, :cwid]
    b = p[:, cwid:2 * cwid]
    c = p[:, 2 * cwid:]
    cu = c * u
    prev = carry_ref[...]
    row = lax.broadcasted_iota(I32, (tm, cwid), 0)
    cu1 = jnp.where(row == 0, prev[7:8, :], pltpu.roll(cu, 1, 0))
    cu2 = pltpu.roll(cu, 2, 0)
    cu2 = jnp.where(row == 0, prev[6:7, :], jnp.where(row == 1, prev[7:8, :], cu2))
    cw = cw_ref[...]
    y = b * (cw[0:1, :] * cu2 + cw[1:2, :] * cu1 + cw[2:3, :] * cu)
    o_ref[...] = y.astype(o_ref.dtype)
    carry_ref[...] = cu[tm - 8:, :]


def _conv_branch(x, w_conv_in, conv_w, seq):
    n, d = x.shape
    cwid = conv_w.shape[1]
    assert conv_w.shape[0] == 3
    tm = _tile(seq, 512)
    vmem = 2 * (_nbytes((tm, d), x.dtype) + _nbytes(w_conv_in.shape, BF16) + _nbytes((tm, cwid), BF16)) \
        + 6 * _nbytes((tm, 3 * cwid), F32)
    return pl.pallas_call(
        functools.partial(_conv_kernel, tiles_per_seq=seq // tm),
        out_shape=jax.ShapeDtypeStruct((n, cwid), BF16),
        grid=(n // tm,),
        in_specs=[pl.BlockSpec((tm, d), lambda i: (i, 0)),
                  pl.BlockSpec(w_conv_in.shape, lambda i: (0, 0)),
                  pl.BlockSpec(conv_w.shape, lambda i: (0, 0))],
        out_specs=pl.BlockSpec((tm, cwid), lambda i: (i, 0)),
        scratch_shapes=[pltpu.VMEM((8, cwid), F32)],
        compiler_params=_params(("arbitrary",), vmem),
    )(x, w_conv_in, conv_w)


def _head_sum(x):
    return jnp.sum(x, axis=-1, keepdims=True)


def _hgrn_kernel(q_ref, f_ref, v_ref, g_ref, c1_ref, c2_ref, gain_ref, o_ref,
                 st_ref, qd_ref, kd_ref, vb_ref, dec_ref, oi_ref, *, heads):
    tt, hw = q_ref.shape
    sub = HG_SUB
    dk = HG_DK

    @pl.when(pl.program_id(1) == 0)
    def _():
        st_ref[...] = jnp.zeros_like(st_ref)

    f = f_ref[...]
    log_sig = jnp.minimum(f, 0.0) - jnp.log1p(jnp.exp(-jnp.abs(f)))
    a1 = c1_ref[...]
    a2 = c2_ref[...] + log_sig
    log_f = jnp.maximum(a1, a2) + jnp.log1p(jnp.exp(-jnp.abs(a1 - a2)))
    kin = 1.0 - jnp.exp(log_f)

    r = lax.broadcasted_iota(I32, (tt, tt), 0)
    c = lax.broadcasted_iota(I32, (tt, tt), 1)
    tri = jnp.where(((r // sub) == (c // sub)) & (c <= r), 1.0, 0.0).astype(BF16)
    t_hi = log_f.astype(BF16)
    rem = log_f - t_hi.astype(F32)
    t_mid = rem.astype(BF16)
    t_lo = (rem - t_mid.astype(F32)).astype(BF16)
    bcum = (jnp.dot(tri, t_hi, preferred_element_type=F32) + jnp.dot(tri, t_mid, preferred_element_type=F32)
            + jnp.dot(tri, t_lo, preferred_element_type=F32))
    b3 = bcum.reshape(tt // sub, sub, hw)
    blast = jnp.broadcast_to(b3[:, sub - 1:sub, :], b3.shape).reshape(tt, hw)

    q = q_ref[...]
    qd_ref[...] = (q * jnp.exp(bcum)).astype(BF16)
    kd_ref[...] = (kin * jnp.exp(blast - bcum)).astype(BF16)
    vb_ref[...] = v_ref[...].astype(BF16)
    dec_ref[...] = jnp.exp(blast)

    def chunk(j, carry):
        r0 = pl.multiple_of(j * sub, sub)
        for h in range(heads):
            ls = slice(h * dk, (h + 1) * dk)
            st = st_ref[h]
            qd = qd_ref[pl.ds(r0, sub), ls]
            oi_ref[pl.ds(r0, sub), ls] = lax.dot_general(
                qd, st.astype(BF16), (((1,), (1,)), ((), ())), preferred_element_type=F32)
            upd = lax.dot_general(vb_ref[pl.ds(r0, sub), ls], kd_ref[pl.ds(r0, sub), ls],
                                  (((0,), (0,)), ((), ())), preferred_element_type=F32)
            st_ref[h] = st * dec_ref[pl.ds(r0, 1), ls] + upd
        return carry

    lax.fori_loop(0, tt // sub, chunk, 0)

    pos = lax.broadcasted_iota(I32, (tt, dk), 0) % sub
    gain = gain_ref[...]
    for h in range(heads):
        ls = slice(h * dk, (h + 1) * dk)
        qh = q[:, ls]
        kh = kin[:, ls]
        bh = bcum[:, ls]
        vh = v_ref[:, ls]
        acc = oi_ref[:, ls] + _head_sum(qh * kh) * vh
        for d in range(1, sub):
            ok = pos >= d
            arg = jnp.where(ok, bh - pltpu.roll(bh, d, 0), NEG_BIG)
            sc = _head_sum(qh * pltpu.roll(kh, d, 0) * jnp.exp(arg))
            acc = acc + sc * pltpu.roll(vh, d, 0)
        ms = _head_sum(acc * acc) * (1.0 / dk)
        y = acc * lax.rsqrt(ms + RMS_EPS) * gain[:, ls] * _silu(g_ref[:, ls])
        o_ref[:, ls] = y.astype(o_ref.dtype)


def _hgrn_branch(qfig, c1, c2, gain, bsz, seq):
    n = qfig.shape[0]
    hw = qfig.shape[1] // 4
    heads = hw // HG_DK
    tt = _tile(seq, 256)
    nt = seq // tt
    blk = lambda col: pl.BlockSpec((tt, hw), lambda b, t, col=col: (b * nt + t, col))
    vec = pl.BlockSpec((1, hw), lambda b, t: (0, 0))
    vmem = 2 * 5 * _nbytes((tt, hw), F32) + 12 * _nbytes((tt, hw), F32) + heads * HG_DK * HG_DK * 4
    return pl.pallas_call(
        functools.partial(_hgrn_kernel, heads=heads),
        out_shape=jax.ShapeDtypeStruct((n, hw), BF16),
        grid=(bsz, nt),
        in_specs=[blk(0), blk(1), blk(2), blk(3), vec, vec, vec],
        out_specs=pl.BlockSpec((tt, hw), lambda b, t: (b * nt + t, 0)),
        scratch_shapes=[pltpu.VMEM((heads, HG_DK, HG_DK), F32),
                        pltpu.VMEM((tt, hw), BF16), pltpu.VMEM((tt, hw), BF16),
                        pltpu.VMEM((tt, hw), BF16), pltpu.VMEM((tt, hw), F32),
                        pltpu.VMEM((tt, hw), F32)],
        compiler_params=_params(("arbitrary", "arbitrary"), vmem),
    )(qfig, qfig, qfig, qfig, c1, c2, gain)


def _attn_kernel(x_ref, wq_ref, k_ref, v_ref, o_ref, *, heads):
    dh = XA_DH
    q = jnp.dot(x_ref[...].astype(BF16), wq_ref[...], preferred_element_type=F32)
    scale = dh ** -0.5
    for h in range(heads):
        ls = slice(h * dh, (h + 1) * dh)
        s = lax.dot_general(q[:, ls].astype(BF16), k_ref[:, ls], (((1,), (1,)), ((), ())),
                            preferred_element_type=F32) * scale
        e = jnp.exp(s - jnp.max(s, axis=-1, keepdims=True))
        p = e / jnp.sum(e, axis=-1, keepdims=True)
        o = jnp.dot(p.astype(BF16), v_ref[:, ls], preferred_element_type=F32)
        o_ref[:, ls] = o.astype(o_ref.dtype)


def _attn_branch(x, wq, kv, seq, n_mem):
    n, d = x.shape
    xw = wq.shape[1]
    heads = xw // XA_DH
    tm = _tile(seq, 512)
    per_seq = seq // tm
    vmem = 2 * (_nbytes((tm, d), x.dtype) + _nbytes(wq.shape, BF16) + 2 * _nbytes((n_mem, xw), BF16)
                + _nbytes((tm, xw), BF16)) + 4 * _nbytes((tm, xw), F32) + 4 * _nbytes((tm, n_mem), F32)
    return pl.pallas_call(
        functools.partial(_attn_kernel, heads=heads),
        out_shape=jax.ShapeDtypeStruct((n, xw), BF16),
        grid=(n // tm,),
        in_specs=[pl.BlockSpec((tm, d), lambda i: (i, 0)),
                  pl.BlockSpec(wq.shape, lambda i: (0, 0)),
                  pl.BlockSpec((n_mem, xw), lambda i: (i // per_seq, 0)),
                  pl.BlockSpec((n_mem, xw), lambda i: (i // per_seq, 1))],
        out_specs=pl.BlockSpec((tm, xw), lambda i: (i, 0)),
        compiler_params=_params(("arbitrary",), vmem),
    )(x, wq, kv, kv)


def _merge_kernel(x_ref, ya_ref, yb_ref, yx_ref, wga_ref, wgb_ref, wgx_ref, wa_ref, wb_ref, wx_ref,
                  o_ref, xb_ref):
    @pl.when(pl.program_id(1) == 0)
    def _():
        xb_ref[...] = x_ref[...].astype(BF16)

    xb = xb_ref[...]
    dot = functools.partial(jnp.dot, preferred_element_type=F32)
    m = _sigmoid(dot(xb, wga_ref[...])) * dot(ya_ref[...], wa_ref[...])
    m = m + _sigmoid(dot(xb, wgb_ref[...])) * dot(yb_ref[...], wb_ref[...])
    m = m + _sigmoid(dot(xb, wgx_ref[...])) * dot(yx_ref[...], wx_ref[...])
    o_ref[...] = m.astype(o_ref.dtype)


def _merge(x, ya, yb, yx, w_gates, wa, wb, wx):
    n, d = x.shape
    tm, tn = _tile(n, 1024), _tile(d, 512)
    nj = d // tn
    row = lambda a: pl.BlockSpec((tm, a.shape[1]), lambda i, j: (i, 0))
    gate = lambda g: pl.BlockSpec((d, tn), lambda i, j, g=g: (0, g * nj + j))
    outw = lambda a: pl.BlockSpec((a.shape[0], tn), lambda i, j: (0, j))
    vmem = 2 * (_nbytes((tm, d), x.dtype) + _nbytes((tm, ya.shape[1] + yb.shape[1] + yx.shape[1]), BF16)
                + 3 * _nbytes((d, tn), BF16) + _nbytes((wa.shape[0] + wb.shape[0] + wx.shape[0], tn), BF16)
                + _nbytes((tm, tn), BF16)) + _nbytes((tm, d), BF16) + 8 * _nbytes((tm, tn), F32)
    return pl.pallas_call(
        _merge_kernel,
        out_shape=jax.ShapeDtypeStruct((n, d), BF16),
        grid=(n // tm, nj),
        in_specs=[row(x), row(ya), row(yb), row(yx), gate(0), gate(1), gate(2),
                  outw(wa), outw(wb), outw(wx)],
        out_specs=pl.BlockSpec((tm, tn), lambda i, j: (i, j)),
        scratch_shapes=[pltpu.VMEM((tm, d), BF16)],
        compiler_params=_params(("arbitrary", "arbitrary"), vmem),
    )(x, ya, yb, yx, w_gates, w_gates, w_gates, wa, wb, wx)


def _oproj_ln_kernel(m_ref, w_ref, x_ref, g_ref, b_ref, o_ref, p_ref, *, alpha):
    y = alpha * x_ref[...] + jnp.dot(m_ref[...], w_ref[...], preferred_element_type=F32)
    y = _layer_norm_rows(y, g_ref[...], b_ref[...])
    o_ref[...] = y
    _store_packed_rows(p_ref, y)


def _oproj_ln(merged, w_o, x, g, b, alpha):
    n, d = x.shape
    assert d % PAIR == 0
    pieces = d // PAIR
    tm = _tile(n, 512)
    vec = pl.BlockSpec((1, d), lambda i: (0, 0))
    vmem = 2 * (_nbytes((tm, d), BF16) + _nbytes((d, d), BF16) + 2 * _nbytes((tm, d), F32)
                + _nbytes((tm, d // 2), U32)) + 4 * _nbytes((tm, d), F32)
    return pl.pallas_call(
        functools.partial(_oproj_ln_kernel, alpha=alpha),
        out_shape=(jax.ShapeDtypeStruct((n, d), F32), jax.ShapeDtypeStruct((n * pieces, LANES), U32)),
        grid=(n // tm,),
        in_specs=[pl.BlockSpec((tm, d), lambda i: (i, 0)),
                  pl.BlockSpec((d, d), lambda i: (0, 0)),
                  pl.BlockSpec((tm, d), lambda i: (i, 0)), vec, vec],
        out_specs=(pl.BlockSpec((tm, d), lambda i: (i, 0)),
                   pl.BlockSpec((tm * pieces, LANES), lambda i: (i, 0))),
        compiler_params=_params(("arbitrary",), vmem),
    )(merged, w_o, x, g, b)


def _rank_desc(vals):
    rows = vals.shape[0]
    iota = lax.broadcasted_iota(I32, vals.shape, 0)
    rank = jnp.zeros(vals.shape, I32)
    for j in range(rows):
        rowv = vals[j:j + 1, :]
        tie = jnp.where(iota > j, 1, 0)
        rank = rank + jnp.where(rowv > vals, 1, jnp.where(rowv == vals, tie, 0))
    return rank


def _router_kernel(x_ref, rw_ref, rb_ref, eid_ref, rnk_ref, gate_ref, cnt_ref, carry_ref):
    ne = rw_ref.shape[0]
    tm = x_ref.shape[0]
    gsz = ne // N_GROUPS

    @pl.when(pl.program_id(0) == 0)
    def _():
        carry_ref[...] = jnp.zeros_like(carry_ref)

    logits = lax.dot_general(rw_ref[...], x_ref[...], (((1,), (1,)), ((), ())),
                             preferred_element_type=F32, precision=lax.Precision.HIGHEST)
    s = _sigmoid(logits)
    sel = s + rb_ref[...]

    grp = sel.reshape(N_GROUPS, gsz, tm)
    sub_iota = lax.broadcasted_iota(I32, (N_GROUPS, gsz, tm), 1)
    m1 = jnp.max(grp, axis=1, keepdims=True)
    first = jnp.min(jnp.where(grp == m1, sub_iota, gsz), axis=1, keepdims=True)
    m2 = jnp.max(jnp.where(sub_iota == first, -jnp.inf, grp), axis=1, keepdims=True)
    gscore = (m1 + m2).reshape(N_GROUPS, tm)

    gsel = _rank_desc(gscore) < TOPK_GROUPS
    emask = jnp.broadcast_to(gsel.reshape(N_GROUPS, 1, tm), (N_GROUPS, gsz, tm)).reshape(ne, tm)
    masked = jnp.where(emask, sel, -jnp.inf)

    eiota = lax.broadcasted_iota(I32, (ne, tm), 0)
    chosen = (_rank_desc(masked) < TOP_K) & emask

    w = jnp.where(chosen, s, 0.0)
    gate = w / jnp.sum(w, axis=0, keepdims=True) * ROUTED_SCALE

    ch = jnp.where(chosen, 1.0, 0.0).astype(BF16)
    tr = lax.broadcasted_iota(I32, (tm, tm), 0)
    tc = lax.broadcasted_iota(I32, (tm, tm), 1)
    before = jnp.where(tr < tc, 1.0, 0.0).astype(BF16)
    rank_tok = (carry_ref[...] + jnp.dot(ch, before, preferred_element_type=F32)).astype(I32)
    carry_ref[...] = carry_ref[...] + jnp.sum(jnp.where(chosen, 1.0, 0.0), axis=1, keepdims=True)
    cnt_ref[...] = jnp.broadcast_to(carry_ref[...], cnt_ref.shape).astype(I32)

    er = lax.broadcasted_iota(I32, (ne, ne), 0)
    ec = lax.broadcasted_iota(I32, (ne, ne), 1)
    lower = jnp.where(ec < er, 1.0, 0.0).astype(BF16)
    slot = jnp.dot(lower, ch, preferred_element_type=F32).astype(I32)
    for k in range(TOP_K):
        pick = chosen & (slot == k)
        eid_ref[k:k + 1, :] = jnp.sum(jnp.where(pick, eiota, 0), axis=0, keepdims=True)
        rnk_ref[k:k + 1, :] = jnp.sum(jnp.where(pick, rank_tok, 0), axis=0, keepdims=True)
        gate_ref[k:k + 1, :] = jnp.sum(jnp.where(pick, gate, 0.0), axis=0, keepdims=True)


def _router(x, rw_t, rb):
    n, d = x.shape
    ne = rw_t.shape[0]
    tm = _tile(n, 512)
    out = lambda dt: jax.ShapeDtypeStruct((TOP_K, n), dt)
    ospec = pl.BlockSpec((TOP_K, tm), lambda i: (0, i))
    vmem = 2 * (_nbytes((tm, d), F32) + _nbytes((ne, d), F32)) + 3 * _nbytes((tm, tm), F32) \
        + 24 * _nbytes((ne, tm), F32)
    return pl.pallas_call(
        _router_kernel,
        out_shape=(out(I32), out(I32), out(F32), jax.ShapeDtypeStruct((ne, 128), I32)),
        grid=(n // tm,),
        in_specs=[pl.BlockSpec((tm, d), lambda i: (i, 0)),
                  pl.BlockSpec((ne, d), lambda i: (0, 0)),
                  pl.BlockSpec((ne, 1), lambda i: (0, 0))],
        out_specs=(ospec, ospec, ospec, pl.BlockSpec((ne, 128), lambda i: (0, 0))),
        scratch_shapes=[pltpu.VMEM((ne, 1), F32)],
        compiler_params=_params(("arbitrary",), vmem),
    )(x, rw_t, rb)


def _pad_pieces(max_pad):
    pieces = []
    p = 1
    while p <= max_pad:
        pieces.append(p)
        p *= 2
    return pieces[::-1]


def _dispatch_kernel(pad_ref, dest_hbm, xp_ref, zero_hbm, xs_hbm, idx_ref, sem_idx, sem_row, sem_pad,
                     *, tq, ne, rp):
    i = pl.program_id(0)

    def row_copy(src_row, dst_row):
        return pltpu.make_async_copy(xp_ref.at[pl.ds(pl.multiple_of(src_row * rp, rp), rp)],
                                     xs_hbm.at[pl.ds(pl.multiple_of(dst_row * rp, rp), rp)], sem_row)

    def pad_copies(fn):
        def per_expert(e, carry):
            row = pad_ref[0, e]
            npad = pad_ref[1, e]
            for p in _pad_pieces(EXPERT_ROWS - 1):
                @pl.when((npad & p) != 0)
                def _(row=row, p=p):
                    fn(pltpu.make_async_copy(zero_hbm.at[pl.ds(0, p * rp)],
                                             xs_hbm.at[pl.ds(pl.multiple_of(row * rp, rp), p * rp)], sem_pad))
                row = row + (npad & p)
            return carry
        lax.fori_loop(0, ne, per_expert, 0)

    @pl.when(i == 0)
    def _():
        pad_copies(lambda cp: cp.start())

    idx_cp = pltpu.make_async_copy(dest_hbm.at[i], idx_ref, sem_idx)
    idx_cp.start()
    idx_cp.wait()

    def per_token(t, carry):
        for k in range(TOP_K):
            row_copy(t, idx_ref[t * TOP_K + k]).start(priority=k % 2)
        return carry

    lax.fori_loop(0, tq, per_token, 0)

    def drain(t, carry):
        for k in range(TOP_K):
            row_copy(0, 0).wait()
        return carry

    lax.fori_loop(0, tq, drain, 0)

    @pl.when(i == 0)
    def _():
        pad_copies(lambda cp: cp.wait())


def _dispatch(pad_info, dest_tok, xp, n, rows):
    rp = xp.shape[0] // n
    ne = pad_info.shape[1]
    tq = _tile(n, 1024)
    steps = n // tq
    dest2 = dest_tok.reshape(steps, tq * TOP_K)
    zero = jnp.zeros((EXPERT_ROWS * rp, LANES), U32)
    any_spec = pl.BlockSpec(memory_space=pl.ANY)
    grid_spec = pltpu.PrefetchScalarGridSpec(
        num_scalar_prefetch=1, grid=(steps,),
        in_specs=[any_spec, pl.BlockSpec((tq * rp, LANES), lambda i, pad: (i, 0)), any_spec],
        out_specs=any_spec,
        scratch_shapes=[pltpu.SMEM((tq * TOP_K,), I32), pltpu.SemaphoreType.DMA(()),
                        pltpu.SemaphoreType.DMA(()), pltpu.SemaphoreType.DMA(())])
    return pl.pallas_call(
        functools.partial(_dispatch_kernel, tq=tq, ne=ne, rp=rp),
        out_shape=jax.ShapeDtypeStruct((rows * rp, LANES), U32),
        grid_spec=grid_spec,
        compiler_params=_params(("arbitrary",), 2 * _nbytes((tq * rp, LANES), U32)),
    )(pad_info, dest2, xp, zero)


def _expert_kernel(be_ref, nu_ref, xs_ref, wg_ref, wu_ref, wd_ref, ys_ref, wgb_ref, wub_ref, wdb_ref):
    b = pl.program_id(0)

    @pl.when(b < nu_ref[0])
    def _():
        prev = be_ref[jnp.maximum(b - 1, 0)]

        @pl.when((b == 0) | (be_ref[b] != prev))
        def _():
            wgb_ref[...] = wg_ref[...].astype(BF16)
            wub_ref[...] = wu_ref[...].astype(BF16)
            wdb_ref[...] = wd_ref[...].astype(BF16)

        t = EXPERT_ROWS
        d = wgb_ref.shape[0]
        xb = _load_packed_rows(xs_ref, t, d)
        dot = functools.partial(jnp.dot, preferred_element_type=F32)
        hmid = (_silu(dot(xb, wgb_ref[...])) * dot(xb, wub_ref[...])).astype(BF16)
        _store_packed_rows(ys_ref, dot(hmid, wdb_ref[...]))


def _experts(blk_exp, n_used, xs, wg, wu, wd, layer):
    _, ne, d, hid = wg.shape
    t = EXPERT_ROWS
    rp = d // PAIR
    rows = xs.shape[0] // rp
    nb = rows // t
    clamp = lambda b, nu: jnp.minimum(b, jnp.maximum(nu[0] - 1, 0))
    wspec = lambda shape: pl.BlockSpec((None, None) + shape,
                                       lambda b, be, nu: (layer, be[clamp(b, nu)], 0, 0))
    grid_spec = pltpu.PrefetchScalarGridSpec(
        num_scalar_prefetch=2, grid=(nb,),
        in_specs=[pl.BlockSpec((t * rp, LANES), lambda b, be, nu: (clamp(b, nu), 0)),
                  wspec((d, hid)), wspec((d, hid)), wspec((hid, d))],
        out_specs=pl.BlockSpec((t * rp, LANES), lambda b, be, nu: (clamp(b, nu), 0)),
        scratch_shapes=[pltpu.VMEM((d, hid), BF16), pltpu.VMEM((d, hid), BF16), pltpu.VMEM((hid, d), BF16)])
    vmem = 2 * (3 * _nbytes((d, hid), F32) + _nbytes((t, d // 2), U32) + _nbytes((t, d), F32)) \
        + 3 * _nbytes((d, hid), BF16) + 4 * _nbytes((t, d), F32)
    return pl.pallas_call(
        _expert_kernel,
        out_shape=jax.ShapeDtypeStruct((rows * rp, LANES), U32),
        grid_spec=grid_spec,
        compiler_params=_params(("arbitrary",), vmem),
    )(blk_exp, n_used, xs, wg, wu, wd)


def _combine_kernel(dest_ref, ys_hbm, x_ref, gate_ref, sg_ref, su_ref, sd_ref, g_ref, b_ref, o_ref,
                    buf_ref, sem_ref, *, tc, alpha):
    i = pl.program_id(0)
    rp = x_ref.shape[1] // PAIR
    half = i % 2

    def row_copy(src_row, dst_row, h):
        return pltpu.make_async_copy(ys_hbm.at[pl.ds(pl.multiple_of(src_row * rp, rp), rp)],
                                     buf_ref.at[pl.ds(pl.multiple_of(dst_row * rp, rp), rp)], sem_ref.at[h])

    def fetch(step, h):
        def per_token(t, carry):
            for k in range(TOP_K):
                src = dest_ref[(step * tc + t) * TOP_K + k]
                row_copy(src, (h * TOP_K + k) * tc + t, h).start(priority=k % 2)
            return carry
        lax.fori_loop(0, tc, per_token, 0)

    @pl.when(i == 0)
    def _():
        fetch(0, 0)

    @pl.when(i + 1 < pl.num_programs(0))
    def _():
        fetch(i + 1, 1 - half)

    x = x_ref[...]
    xb = x.astype(BF16)
    dot = functools.partial(jnp.dot, preferred_element_type=F32)
    hmid = (_silu(dot(xb, sg_ref[...])) * dot(xb, su_ref[...])).astype(BF16)
    shared = dot(hmid, sd_ref[...])

    def drain(t, carry):
        for k in range(TOP_K):
            row_copy(0, 0, half).wait()
        return carry

    lax.fori_loop(0, tc, drain, 0)

    gate = gate_ref[...]
    base = half * (TOP_K * tc * rp)
    cols = []
    for s in range(rp):
        lo = hi = None
        for k in range(TOP_K):
            w = buf_ref[pl.ds(base + k * tc * rp + s, tc, stride=rp), :]
            gk = gate[:, k:k + 1]
            lo_k = lax.bitcast_convert_type(w << 16, F32) * gk
            hi_k = lax.bitcast_convert_type(w & jnp.uint32(0xFFFF0000), F32) * gk
            lo = lo_k if lo is None else lo + lo_k
            hi = hi_k if hi is None else hi + hi_k
        cols += [lo, hi]
    routed = jnp.concatenate(cols, axis=1)
    o_ref[...] = _layer_norm_rows(alpha * x + (shared + routed), g_ref[...], b_ref[...])


def _combine(dest_tok, ys, x, gate_tok, sg, su, sd, g, b, alpha):
    n, d = x.shape
    hid = sg.shape[1]
    tc = _tile(n, 128)
    steps = n // tc
    rp = d // PAIR
    vec = pl.BlockSpec((1, d), lambda i, dest: (0, 0))
    full = lambda a: pl.BlockSpec(a.shape, lambda i, dest: (0, 0))
    buf_rows = 2 * TOP_K * tc * rp
    vmem = _nbytes((buf_rows, LANES), U32) + 2 * (2 * _nbytes((tc, d), F32) + 3 * _nbytes((d, hid), BF16)) \
        + 6 * _nbytes((tc, d), F32)
    grid_spec = pltpu.PrefetchScalarGridSpec(
        num_scalar_prefetch=1, grid=(steps,),
        in_specs=[pl.BlockSpec(memory_space=pl.ANY),
                  pl.BlockSpec((tc, d), lambda i, dest: (i, 0)),
                  pl.BlockSpec((tc, TOP_K), lambda i, dest: (i, 0)),
                  full(sg), full(su), full(sd), vec, vec],
        out_specs=pl.BlockSpec((tc, d), lambda i, dest: (i, 0)),
        scratch_shapes=[pltpu.VMEM((buf_rows, LANES), U32), pltpu.SemaphoreType.DMA((2,))])
    return pl.pallas_call(
        functools.partial(_combine_kernel, tc=tc, alpha=alpha),
        out_shape=jax.ShapeDtypeStruct((n, d), F32),
        grid_spec=grid_spec,
        compiler_params=_params(("arbitrary",), vmem),
    )(dest_tok, ys, x, gate_tok, sg, su, sd, g, b)


def _moe(x1, x1p, rw_t, rb, wg, wu, wd, layer, sg, su, sd, g, b, alpha):
    n, d = x1.shape
    ne = rw_t.shape[0]
    eid, rnk, gate, cnt = _router(x1, rw_t, rb)
    counts = cnt[:, 0]
    padded = (counts + EXPERT_ROWS - 1) // EXPERT_ROWS * EXPERT_ROWS
    eidx = jnp.arange(ne, dtype=I32)
    pad_end = jnp.sum(jnp.where(eidx[None, :] <= eidx[:, None], padded[None, :], 0), axis=1)
    pad_start = pad_end - padded
    n_blocks = (n * TOP_K + ne * (EXPERT_ROWS - 1)) // EXPERT_ROWS
    rows = n_blocks * EXPERT_ROWS
    start_of = jnp.sum(jnp.where(eid[:, :, None] == eidx, pad_start, 0), axis=-1)
    dest_tok = (start_of + rnk).T.reshape(-1).astype(I32)
    pad_info = jnp.stack([pad_start + counts, padded - counts]).astype(I32)
    total = pad_end[ne - 1]
    n_used = (total // EXPERT_ROWS).astype(I32).reshape(1)
    blk_start = jnp.minimum(jnp.arange(n_blocks, dtype=I32) * EXPERT_ROWS, total - 1)
    blk_exp = jnp.sum((pad_end[None, :] <= blk_start[:, None]).astype(I32), axis=1)
    blk_exp = jnp.minimum(blk_exp, ne - 1).astype(I32)
    xs = _dispatch(pad_info, dest_tok, x1p, n, rows)
    ys = _experts(blk_exp, n_used, xs, wg, wu, wd, layer)
    return _combine(dest_tok, ys, x1, gate.T, sg, su, sd, g, b, alpha)


def kernel(x, mem, w_in, conv_w, w_conv_out, hg_lb_logits, hg_norm_g, w_hg_out, w_mem_k, w_mem_v,
           w_xa_out, w_o, ln1_g, ln1_b, router_w, router_b, exp_wg, exp_wu, exp_wd,
           sh_wg, sh_wu, sh_wd, ln2_g, ln2_b):
    bsz, seq, d = x.shape
    depth = w_in.shape[0]
    n_mem = mem.shape[1]
    cwid = conv_w.shape[2]
    hw = w_hg_out.shape[1]
    xw = w_mem_k.shape[2]
    alpha = float((2 * depth) ** 0.25)
    n = bsz * seq

    lb_all = jnp.cumsum(jax.nn.softmax(hg_lb_logits.astype(F32), axis=0), axis=0)
    lb_all = lb_all - lb_all[0:1]
    o_conv, o_hg, o_xa, o_gate = 0, 3 * cwid, 3 * cwid + 4 * hw, 3 * cwid + 4 * hw + xw

    h = x.reshape(n, d)
    memf = mem.reshape(bsz * n_mem, d)
    for l in range(depth):
        wl = w_in[l]
        w_conv_in = wl[:, o_conv:o_hg].astype(BF16)
        w_hg_in = wl[:, o_hg:o_xa].astype(BF16)
        w_xa_in = wl[:, o_xa:o_gate].astype(BF16)
        w_gates = wl[:, o_gate:].astype(BF16)
        vec = lambda a: a.astype(F32).reshape(1, -1)

        ya = _conv_branch(h, w_conv_in, conv_w[l].astype(F32), seq)
        qfig = _matmul(h, w_hg_in, F32, 1024, hw)
        lb = lb_all[l]
        yb = _hgrn_branch(qfig, vec(jnp.log(lb)), vec(jnp.log1p(-lb)), vec(hg_norm_g[l]), bsz, seq)
        w_kv = jnp.concatenate([w_mem_k[l], w_mem_v[l]], axis=1).astype(BF16)
        kv = _matmul(memf, w_kv, BF16, 1024, xw)
        yx = _attn_branch(h, w_xa_in, kv, seq, n_mem)
        merged = _merge(h, ya, yb, yx, w_gates, w_conv_out[l].astype(BF16), w_hg_out[l].astype(BF16),
                        w_xa_out[l].astype(BF16))
        x1, x1p = _oproj_ln(merged, w_o[l].astype(BF16), h, vec(ln1_g[l]), vec(ln1_b[l]), alpha)
        h = _moe(x1, x1p, router_w[l].astype(F32).T, router_b[l].astype(F32).reshape(-1, 1),
                    exp_wg, exp_wu, exp_wd, l,
                    sh_wg[l].astype(BF16), sh_wu[l].astype(BF16), sh_wd[l].astype(BF16),
                    vec(ln2_g[l]), vec(ln2_b[l]), alpha)
    return h.reshape(bsz, seq, d)
```

```python
import functools

import jax
import jax.numpy as jnp
from jax import lax
from jax.experimental import pallas as pl
from jax.experimental.pallas import tpu as pltpu

F32 = jnp.float32
BF16 = jnp.bfloat16
I32 = jnp.int32
U32 = jnp.uint32

HG_DK = 128
XA_DH = 128
TOP_K = 8
N_GROUPS = 8
TOPK_GROUPS = 4
ROUTED_SCALE = 2.5
LN_EPS = 1e-5
RMS_EPS = 1e-6
HG_SUB = 16
EXPERT_ROWS = 256
NEG_BIG = -1e30

V7X_VMEM_BYTES = 64 * 1024 * 1024
VMEM_CAP = V7X_VMEM_BYTES - 8 * 1024 * 1024


def _params(semantics, vmem_bytes):
    limit = int(min(VMEM_CAP, max(32 * 1024 * 1024, vmem_bytes * 5 // 4 + (4 << 20))))
    return pltpu.CompilerParams(dimension_semantics=semantics, vmem_limit_bytes=limit)


def _nbytes(shape, dtype):
    n = 1
    for s in shape:
        n *= s
    return n * jnp.dtype(dtype).itemsize


def _tile(n, want):
    t = min(n, want)
    while n % t:
        t //= 2
    return t


def _silu(x):
    return x * (1.0 / (1.0 + jnp.exp(-x)))


def _sigmoid(x):
    return 1.0 / (1.0 + jnp.exp(-x))


def _layer_norm_rows(y, g, b):
    mu = jnp.mean(y, axis=-1, keepdims=True)
    yc = y - mu
    var = jnp.mean(yc * yc, axis=-1, keepdims=True)
    return yc * lax.rsqrt(var + LN_EPS) * g + b


LANES = 128
PAIR = 2 * LANES


def _store_packed_rows(p_ref, y):
    rows, d = y.shape
    pieces = d // PAIR
    for s in range(pieces):
        lo = lax.bitcast_convert_type(y[:, s * PAIR:s * PAIR + LANES].astype(BF16).astype(F32), U32)
        hi = lax.bitcast_convert_type(y[:, s * PAIR + LANES:(s + 1) * PAIR].astype(BF16).astype(F32), U32)
        p_ref[pl.ds(s, rows, stride=pieces), :] = (hi & jnp.uint32(0xFFFF0000)) | (lo >> 16)


def _load_packed_rows(p_ref, rows, d):
    pieces = d // PAIR
    cols = []
    for s in range(pieces):
        w = p_ref[pl.ds(s, rows, stride=pieces), :]
        cols.append(lax.bitcast_convert_type(w << 16, F32).astype(BF16))
        cols.append(lax.bitcast_convert_type(w & jnp.uint32(0xFFFF0000), F32).astype(BF16))
    return jnp.concatenate(cols, axis=1)


def _matmul_kernel(x_ref, w_ref, o_ref):
    o_ref[...] = jnp.dot(x_ref[...].astype(BF16), w_ref[...],
                         preferred_element_type=F32).astype(o_ref.dtype)


def _matmul(x, w, out_dtype, tm, tn):
    m, k = x.shape
    n = w.shape[1]
    tm, tn = _tile(m, tm), _tile(n, tn)
    vmem = 2 * (_nbytes((tm, k), x.dtype) + _nbytes((k, tn), w.dtype) + _nbytes((tm, tn), out_dtype)) \
        + _nbytes((tm, k), BF16) + _nbytes((tm, tn), F32)
    return pl.pallas_call(
        _matmul_kernel,
        out_shape=jax.ShapeDtypeStruct((m, n), out_dtype),
        grid=(m // tm, n // tn),
        in_specs=[pl.BlockSpec((tm, k), lambda i, j: (i, 0)),
                  pl.BlockSpec((k, tn), lambda i, j: (0, j))],
        out_specs=pl.BlockSpec((tm, tn), lambda i, j: (i, j)),
        compiler_params=_params(("arbitrary", "arbitrary"), vmem),
    )(x, w)


def _conv_kernel(x_ref, w_ref, cw_ref, o_ref, carry_ref, *, tiles_per_seq):
    i = pl.program_id(0)
    tm, cwid = o_ref.shape

    @pl.when(i % tiles_per_seq == 0)
    def _():
        carry_ref[...] = jnp.zeros_like(carry_ref)

    p = jnp.dot(x_ref[...].astype(BF16), w_ref[...], preferred_element_type=F32)
    u = p[:, :cwid]
    b = p[:, cwid:2 * cwid]
    c = p[:, 2 * cwid:]
    cu = c * u
    prev = carry_ref[...]
    row = lax.broadcasted_iota(I32, (tm, cwid), 0)
    cu1 = jnp.where(row == 0, prev[7:8, :], pltpu.roll(cu, 1, 0))
    cu2 = pltpu.roll(cu, 2, 0)
    cu2 = jnp.where(row == 0, prev[6:7, :], jnp.where(row == 1, prev[7:8, :], cu2))
    cw = cw_ref[...]
    y = b * (cw[0:1, :] * cu2 + cw[1:2, :] * cu1 + cw[2:3, :] * cu)
    o_ref[...] = y.astype(o_ref.dtype)
    carry_ref[...] = cu[tm - 8:, :]


def _conv_branch(x, w_conv_in, conv_w, seq):
    n, d = x.shape
    cwid = conv_w.shape[1]
    assert conv_w.shape[0] == 3
    tm = _tile(seq, 512)
    vmem = 2 * (_nbytes((tm, d), x.dtype) + _nbytes(w_conv_in.shape, BF16) + _nbytes((tm, cwid), BF16)) \
        + 6 * _nbytes((tm, 3 * cwid), F32)
    return pl.pallas_call(
        functools.partial(_conv_kernel, tiles_per_seq=seq // tm),
        out_shape=jax.ShapeDtypeStruct((n, cwid), BF16),
        grid=(n // tm,),
        in_specs=[pl.BlockSpec((tm, d), lambda i: (i, 0)),
                  pl.BlockSpec(w_conv_in.shape, lambda i: (0, 0)),
                  pl.BlockSpec(conv_w.shape, lambda i: (0, 0))],
        out_specs=pl.BlockSpec((tm, cwid), lambda i: (i, 0)),
        scratch_shapes=[pltpu.VMEM((8, cwid), F32)],
        compiler_params=_params(("arbitrary",), vmem),
    )(x, w_conv_in, conv_w)


def _head_sum(x):
    return jnp.sum(x, axis=-1, keepdims=True)


def _hgrn_kernel(x_ref, w_ref, c1_ref, c2_ref, gain_ref, o_ref,
                 st_ref, p_ref, qd_ref, kd_ref, vb_ref, dec_ref, oi_ref, *, heads):
    tt, hw = o_ref.shape
    sub = HG_SUB
    dk = HG_DK

    @pl.when(pl.program_id(1) == 0)
    def _():
        st_ref[...] = jnp.zeros_like(st_ref)

    p_ref[...] = jnp.dot(x_ref[...].astype(BF16), w_ref[...], preferred_element_type=F32)
    q_ref, f_ref, v_ref, g_ref = (p_ref.at[:, pl.ds(j * hw, hw)] for j in range(4))

    f = f_ref[...]
    log_sig = jnp.minimum(f, 0.0) - jnp.log1p(jnp.exp(-jnp.abs(f)))
    a1 = c1_ref[...]
    a2 = c2_ref[...] + log_sig
    log_f = jnp.maximum(a1, a2) + jnp.log1p(jnp.exp(-jnp.abs(a1 - a2)))
    kin = 1.0 - jnp.exp(log_f)

    r = lax.broadcasted_iota(I32, (tt, tt), 0)
    c = lax.broadcasted_iota(I32, (tt, tt), 1)
    tri = jnp.where(((r // sub) == (c // sub)) & (c <= r), 1.0, 0.0).astype(BF16)
    t_hi = log_f.astype(BF16)
    rem = log_f - t_hi.astype(F32)
    t_mid = rem.astype(BF16)
    t_lo = (rem - t_mid.astype(F32)).astype(BF16)
    bcum = (jnp.dot(tri, t_hi, preferred_element_type=F32) + jnp.dot(tri, t_mid, preferred_element_type=F32)
            + jnp.dot(tri, t_lo, preferred_element_type=F32))
    b3 = bcum.reshape(tt // sub, sub, hw)
    blast = jnp.broadcast_to(b3[:, sub - 1:sub, :], b3.shape).reshape(tt, hw)

    q = q_ref[...]
    qd_ref[...] = (q * jnp.exp(bcum)).astype(BF16)
    kd_ref[...] = (kin * jnp.exp(blast - bcum)).astype(BF16)
    vb_ref[...] = v_ref[...].astype(BF16)
    dec_ref[...] = jnp.exp(blast)

    def chunk(j, carry):
        r0 = pl.multiple_of(j * sub, sub)
        for h in range(heads):
            ls = slice(h * dk, (h + 1) * dk)
            st = st_ref[h]
            qd = qd_ref[pl.ds(r0, sub), ls]
            oi_ref[pl.ds(r0, sub), ls] = lax.dot_general(
                qd, st.astype(BF16), (((1,), (1,)), ((), ())), preferred_element_type=F32)
            upd = lax.dot_general(vb_ref[pl.ds(r0, sub), ls], kd_ref[pl.ds(r0, sub), ls],
                                  (((0,), (0,)), ((), ())), preferred_element_type=F32)
            st_ref[h] = st * dec_ref[pl.ds(r0, 1), ls] + upd
        return carry

    lax.fori_loop(0, tt // sub, chunk, 0)

    pos = lax.broadcasted_iota(I32, (tt, dk), 0) % sub
    gain = gain_ref[...]
    for h in range(heads):
        ls = slice(h * dk, (h + 1) * dk)
        qh = q[:, ls]
        kh = kin[:, ls]
        bh = bcum[:, ls]
        vh = v_ref[:, ls]
        acc = oi_ref[:, ls] + _head_sum(qh * kh) * vh
        for d in range(1, sub):
            ok = pos >= d
            arg = jnp.where(ok, bh - pltpu.roll(bh, d, 0), NEG_BIG)
            sc = _head_sum(qh * pltpu.roll(kh, d, 0) * jnp.exp(arg))
            acc = acc + sc * pltpu.roll(vh, d, 0)
        ms = _head_sum(acc * acc) * (1.0 / dk)
        y = acc * lax.rsqrt(ms + RMS_EPS) * gain[:, ls] * _silu(g_ref[:, ls])
        o_ref[:, ls] = y.astype(o_ref.dtype)


def _hgrn_branch(x, w_hg_in, c1, c2, gain, bsz, seq):
    n, d = x.shape
    hw = w_hg_in.shape[1] // 4
    heads = hw // HG_DK
    tt = _tile(seq, 256)
    nt = seq // tt
    vec = pl.BlockSpec((1, hw), lambda b, t: (0, 0))
    vmem = 2 * (_nbytes((tt, d), x.dtype) + _nbytes(w_hg_in.shape, BF16) + _nbytes((tt, hw), BF16)) \
        + _nbytes((tt, d), BF16) + 18 * _nbytes((tt, hw), F32) + heads * HG_DK * HG_DK * 4
    return pl.pallas_call(
        functools.partial(_hgrn_kernel, heads=heads),
        out_shape=jax.ShapeDtypeStruct((n, hw), BF16),
        grid=(bsz, nt),
        in_specs=[pl.BlockSpec((tt, d), lambda b, t: (b * nt + t, 0)),
                  pl.BlockSpec(w_hg_in.shape, lambda b, t: (0, 0)), vec, vec, vec],
        out_specs=pl.BlockSpec((tt, hw), lambda b, t: (b * nt + t, 0)),
        scratch_shapes=[pltpu.VMEM((heads, HG_DK, HG_DK), F32), pltpu.VMEM((tt, 4 * hw), F32),
                        pltpu.VMEM((tt, hw), BF16), pltpu.VMEM((tt, hw), BF16),
                        pltpu.VMEM((tt, hw), BF16), pltpu.VMEM((tt, hw), F32),
                        pltpu.VMEM((tt, hw), F32)],
        compiler_params=_params(("arbitrary", "arbitrary"), vmem),
    )(x, w_hg_in, c1, c2, gain)


def _attn_kernel(x_ref, wq_ref, k_ref, v_ref, o_ref, *, heads):
    dh = XA_DH
    q = jnp.dot(x_ref[...].astype(BF16), wq_ref[...], preferred_element_type=F32)
    scale = dh ** -0.5
    for h in range(heads):
        ls = slice(h * dh, (h + 1) * dh)
        s = lax.dot_general(q[:, ls].astype(BF16), k_ref[:, ls], (((1,), (1,)), ((), ())),
                            preferred_element_type=F32) * scale
        e = jnp.exp(s - jnp.max(s, axis=-1, keepdims=True))
        p = e / jnp.sum(e, axis=-1, keepdims=True)
        o = jnp.dot(p.astype(BF16), v_ref[:, ls], preferred_element_type=F32)
        o_ref[:, ls] = o.astype(o_ref.dtype)


def _attn_branch(x, wq, kv, seq, n_mem):
    n, d = x.shape
    xw = wq.shape[1]
    heads = xw // XA_DH
    tm = _tile(seq, 512)
    per_seq = seq // tm
    vmem = 2 * (_nbytes((tm, d), x.dtype) + _nbytes(wq.shape, BF16) + 2 * _nbytes((n_mem, xw), BF16)
                + _nbytes((tm, xw), BF16)) + 4 * _nbytes((tm, xw), F32) + 4 * _nbytes((tm, n_mem), F32)
    return pl.pallas_call(
        functools.partial(_attn_kernel, heads=heads),
        out_shape=jax.ShapeDtypeStruct((n, xw), BF16),
        grid=(n // tm,),
        in_specs=[pl.BlockSpec((tm, d), lambda i: (i, 0)),
                  pl.BlockSpec(wq.shape, lambda i: (0, 0)),
                  pl.BlockSpec((n_mem, xw), lambda i: (i // per_seq, 0)),
                  pl.BlockSpec((n_mem, xw), lambda i: (i // per_seq, 1))],
        out_specs=pl.BlockSpec((tm, xw), lambda i: (i, 0)),
        compiler_params=_params(("arbitrary",), vmem),
    )(x, wq, kv, kv)


def _merge_kernel(x_ref, ya_ref, yb_ref, yx_ref, wga_ref, wgb_ref, wgx_ref, wa_ref, wb_ref, wx_ref,
                  o_ref, xb_ref):
    @pl.when(pl.program_id(1) == 0)
    def _():
        xb_ref[...] = x_ref[...].astype(BF16)

    xb = xb_ref[...]
    dot = functools.partial(jnp.dot, preferred_element_type=F32)
    m = _sigmoid(dot(xb, wga_ref[...])) * dot(ya_ref[...], wa_ref[...])
    m = m + _sigmoid(dot(xb, wgb_ref[...])) * dot(yb_ref[...], wb_ref[...])
    m = m + _sigmoid(dot(xb, wgx_ref[...])) * dot(yx_ref[...], wx_ref[...])
    o_ref[...] = m.astype(o_ref.dtype)


def _merge(x, ya, yb, yx, w_gates, wa, wb, wx):
    n, d = x.shape
    tm, tn = _tile(n, 1024), _tile(d, 512)
    nj = d // tn
    row = lambda a: pl.BlockSpec((tm, a.shape[1]), lambda i, j: (i, 0))
    gate = lambda g: pl.BlockSpec((d, tn), lambda i, j, g=g: (0, g * nj + j))
    outw = lambda a: pl.BlockSpec((a.shape[0], tn), lambda i, j: (0, j))
    vmem = 2 * (_nbytes((tm, d), x.dtype) + _nbytes((tm, ya.shape[1] + yb.shape[1] + yx.shape[1]), BF16)
                + 3 * _nbytes((d, tn), BF16) + _nbytes((wa.shape[0] + wb.shape[0] + wx.shape[0], tn), BF16)
                + _nbytes((tm, tn), BF16)) + _nbytes((tm, d), BF16) + 8 * _nbytes((tm, tn), F32)
    return pl.pallas_call(
        _merge_kernel,
        out_shape=jax.ShapeDtypeStruct((n, d), BF16),
        grid=(n // tm, nj),
        in_specs=[row(x), row(ya), row(yb), row(yx), gate(0), gate(1), gate(2),
                  outw(wa), outw(wb), outw(wx)],
        out_specs=pl.BlockSpec((tm, tn), lambda i, j: (i, j)),
        scratch_shapes=[pltpu.VMEM((tm, d), BF16)],
        compiler_params=_params(("arbitrary", "arbitrary"), vmem),
    )(x, ya, yb, yx, w_gates, w_gates, w_gates, wa, wb, wx)


def _oproj_ln_kernel(m_ref, w_ref, x_ref, g_ref, b_ref, rw_ref, rb_ref,
                     o_ref, p_ref, eid_ref, rnk_ref, gate_ref, cnt_ref, carry_ref, *, alpha):
    y = alpha * x_ref[...] + jnp.dot(m_ref[...], w_ref[...], preferred_element_type=F32)
    y = _layer_norm_rows(y, g_ref[...], b_ref[...])
    o_ref[...] = y
    _store_packed_rows(p_ref, y)
    _route(y, rw_ref, rb_ref, eid_ref, rnk_ref, gate_ref, cnt_ref, carry_ref)


def _oproj_ln_route(merged, w_o, x, g, b, rw_t, rb, alpha):
    n, d = x.shape
    assert d % PAIR == 0
    pieces = d // PAIR
    ne = rw_t.shape[0]
    tm = _tile(n, 512)
    vec = pl.BlockSpec((1, d), lambda i: (0, 0))
    topk = lambda dt: jax.ShapeDtypeStruct((TOP_K, n), dt)
    kspec = pl.BlockSpec((TOP_K, tm), lambda i: (0, i))
    vmem = 2 * (_nbytes((tm, d), BF16) + _nbytes((d, d), BF16) + 2 * _nbytes((tm, d), F32)
                + _nbytes((tm, d // 2), U32) + _nbytes((ne, d), F32)) + 4 * _nbytes((tm, d), F32) \
        + 3 * _nbytes((tm, tm), F32) + 24 * _nbytes((ne, tm), F32)
    return pl.pallas_call(
        functools.partial(_oproj_ln_kernel, alpha=alpha),
        out_shape=(jax.ShapeDtypeStruct((n, d), F32), jax.ShapeDtypeStruct((n * pieces, LANES), U32),
                   topk(I32), topk(I32), topk(F32), jax.ShapeDtypeStruct((ne, 128), I32)),
        grid=(n // tm,),
        in_specs=[pl.BlockSpec((tm, d), lambda i: (i, 0)),
                  pl.BlockSpec((d, d), lambda i: (0, 0)),
                  pl.BlockSpec((tm, d), lambda i: (i, 0)), vec, vec,
                  pl.BlockSpec((ne, d), lambda i: (0, 0)),
                  pl.BlockSpec((ne, 1), lambda i: (0, 0))],
        out_specs=(pl.BlockSpec((tm, d), lambda i: (i, 0)),
                   pl.BlockSpec((tm * pieces, LANES), lambda i: (i, 0)),
                   kspec, kspec, kspec, pl.BlockSpec((ne, 128), lambda i: (0, 0))),
        scratch_shapes=[pltpu.VMEM((ne, 1), F32)],
        compiler_params=_params(("arbitrary",), vmem),
    )(merged, w_o, x, g, b, rw_t, rb)


def _rank_desc(vals):
    rows = vals.shape[0]
    iota = lax.broadcasted_iota(I32, vals.shape, 0)
    rank = jnp.zeros(vals.shape, I32)
    for j in range(rows):
        rowv = vals[j:j + 1, :]
        tie = jnp.where(iota > j, 1, 0)
        rank = rank + jnp.where(rowv > vals, 1, jnp.where(rowv == vals, tie, 0))
    return rank


def _route(x, rw_ref, rb_ref, eid_ref, rnk_ref, gate_ref, cnt_ref, carry_ref):
    ne = rw_ref.shape[0]
    tm = x.shape[0]
    gsz = ne // N_GROUPS

    @pl.when(pl.program_id(0) == 0)
    def _():
        carry_ref[...] = jnp.zeros_like(carry_ref)

    logits = lax.dot_general(rw_ref[...], x, (((1,), (1,)), ((), ())),
                             preferred_element_type=F32, precision=lax.Precision.HIGHEST)
    s = _sigmoid(logits)
    sel = s + rb_ref[...]

    grp = sel.reshape(N_GROUPS, gsz, tm)
    sub_iota = lax.broadcasted_iota(I32, (N_GROUPS, gsz, tm), 1)
    m1 = jnp.max(grp, axis=1, keepdims=True)
    first = jnp.min(jnp.where(grp == m1, sub_iota, gsz), axis=1, keepdims=True)
    m2 = jnp.max(jnp.where(sub_iota == first, -jnp.inf, grp), axis=1, keepdims=True)
    gscore = (m1 + m2).reshape(N_GROUPS, tm)

    gsel = _rank_desc(gscore) < TOPK_GROUPS
    emask = jnp.broadcast_to(gsel.reshape(N_GROUPS, 1, tm), (N_GROUPS, gsz, tm)).reshape(ne, tm)
    masked = jnp.where(emask, sel, -jnp.inf)

    eiota = lax.broadcasted_iota(I32, (ne, tm), 0)
    chosen = (_rank_desc(masked) < TOP_K) & emask

    w = jnp.where(chosen, s, 0.0)
    gate = w / jnp.sum(w, axis=0, keepdims=True) * ROUTED_SCALE

    ch = jnp.where(chosen, 1.0, 0.0).astype(BF16)
    tr = lax.broadcasted_iota(I32, (tm, tm), 0)
    tc = lax.broadcasted_iota(I32, (tm, tm), 1)
    before = jnp.where(tr < tc, 1.0, 0.0).astype(BF16)
    rank_tok = (carry_ref[...] + jnp.dot(ch, before, preferred_element_type=F32)).astype(I32)
    carry_ref[...] = carry_ref[...] + jnp.sum(jnp.where(chosen, 1.0, 0.0), axis=1, keepdims=True)
    cnt_ref[...] = jnp.broadcast_to(carry_ref[...], cnt_ref.shape).astype(I32)

    er = lax.broadcasted_iota(I32, (ne, ne), 0)
    ec = lax.broadcasted_iota(I32, (ne, ne), 1)
    lower = jnp.where(ec < er, 1.0, 0.0).astype(BF16)
    slot = jnp.dot(lower, ch, preferred_element_type=F32).astype(I32)
    for k in range(TOP_K):
        pick = chosen & (slot == k)
        eid_ref[k:k + 1, :] = jnp.sum(jnp.where(pick, eiota, 0), axis=0, keepdims=True)
        rnk_ref[k:k + 1, :] = jnp.sum(jnp.where(pick, rank_tok, 0), axis=0, keepdims=True)
        gate_ref[k:k + 1, :] = jnp.sum(jnp.where(pick, gate, 0.0), axis=0, keepdims=True)


def _pad_pieces(max_pad):
    pieces = []
    p = 1
    while p <= max_pad:
        pieces.append(p)
        p *= 2
    return pieces[::-1]


def _dispatch_kernel(pad_ref, dest_hbm, xp_ref, zero_hbm, xs_hbm, idx_ref, sem_idx, sem_row, sem_pad,
                     *, tq, ne, rp):
    i = pl.program_id(0)

    def row_copy(src_row, dst_row):
        return pltpu.make_async_copy(xp_ref.at[pl.ds(pl.multiple_of(src_row * rp, rp), rp)],
                                     xs_hbm.at[pl.ds(pl.multiple_of(dst_row * rp, rp), rp)], sem_row)

    def pad_copies(fn):
        def per_expert(e, carry):
            row = pad_ref[0, e]
            npad = pad_ref[1, e]
            for p in _pad_pieces(EXPERT_ROWS - 1):
                @pl.when((npad & p) != 0)
                def _(row=row, p=p):
                    fn(pltpu.make_async_copy(zero_hbm.at[pl.ds(0, p * rp)],
                                             xs_hbm.at[pl.ds(pl.multiple_of(row * rp, rp), p * rp)], sem_pad))
                row = row + (npad & p)
            return carry
        lax.fori_loop(0, ne, per_expert, 0)

    @pl.when(i == 0)
    def _():
        pad_copies(lambda cp: cp.start())

    idx_cp = pltpu.make_async_copy(dest_hbm.at[i], idx_ref, sem_idx)
    idx_cp.start()
    idx_cp.wait()

    def per_token(t, carry):
        for k in range(TOP_K):
            row_copy(t, idx_ref[t * TOP_K + k]).start(priority=k % 2)
        return carry

    lax.fori_loop(0, tq, per_token, 0)

    def drain(t, carry):
        for k in range(TOP_K):
            row_copy(0, 0).wait()
        return carry

    lax.fori_loop(0, tq, drain, 0)

    @pl.when(i == 0)
    def _():
        pad_copies(lambda cp: cp.wait())


def _dispatch(pad_info, dest_tok, xp, n, rows):
    rp = xp.shape[0] // n
    ne = pad_info.shape[1]
    tq = _tile(n, 1024)
    steps = n // tq
    dest2 = dest_tok.reshape(steps, tq * TOP_K)
    zero = jnp.zeros((EXPERT_ROWS * rp, LANES), U32)
    any_spec = pl.BlockSpec(memory_space=pl.ANY)
    grid_spec = pltpu.PrefetchScalarGridSpec(
        num_scalar_prefetch=1, grid=(steps,),
        in_specs=[any_spec, pl.BlockSpec((tq * rp, LANES), lambda i, pad: (i, 0)), any_spec],
        out_specs=any_spec,
        scratch_shapes=[pltpu.SMEM((tq * TOP_K,), I32), pltpu.SemaphoreType.DMA(()),
                        pltpu.SemaphoreType.DMA(()), pltpu.SemaphoreType.DMA(())])
    return pl.pallas_call(
        functools.partial(_dispatch_kernel, tq=tq, ne=ne, rp=rp),
        out_shape=jax.ShapeDtypeStruct((rows * rp, LANES), U32),
        grid_spec=grid_spec,
        compiler_params=_params(("arbitrary",), 2 * _nbytes((tq * rp, LANES), U32)),
    )(pad_info, dest2, xp, zero)


def _expert_kernel(be_ref, nu_ref, xs_ref, wg_ref, wu_ref, wd_ref, ys_ref, wgb_ref, wub_ref, wdb_ref):
    b = pl.program_id(0)

    @pl.when(b < nu_ref[0])
    def _():
        prev = be_ref[jnp.maximum(b - 1, 0)]

        @pl.when((b == 0) | (be_ref[b] != prev))
        def _():
            wgb_ref[...] = wg_ref[...].astype(BF16)
            wub_ref[...] = wu_ref[...].astype(BF16)
            wdb_ref[...] = wd_ref[...].astype(BF16)

        t = EXPERT_ROWS
        d = wgb_ref.shape[0]
        xb = _load_packed_rows(xs_ref, t, d)
        dot = functools.partial(jnp.dot, preferred_element_type=F32)
        hmid = (_silu(dot(xb, wgb_ref[...])) * dot(xb, wub_ref[...])).astype(BF16)
        _store_packed_rows(ys_ref, dot(hmid, wdb_ref[...]))


def _experts(blk_exp, n_used, xs, wg, wu, wd, layer):
    _, ne, d, hid = wg.shape
    t = EXPERT_ROWS
    rp = d // PAIR
    rows = xs.shape[0] // rp
    nb = rows // t
    clamp = lambda b, nu: jnp.minimum(b, jnp.maximum(nu[0] - 1, 0))
    wspec = lambda shape: pl.BlockSpec((None, None) + shape,
                                       lambda b, be, nu: (layer, be[clamp(b, nu)], 0, 0))
    grid_spec = pltpu.PrefetchScalarGridSpec(
        num_scalar_prefetch=2, grid=(nb,),
        in_specs=[pl.BlockSpec((t * rp, LANES), lambda b, be, nu: (clamp(b, nu), 0)),
                  wspec((d, hid)), wspec((d, hid)), wspec((hid, d))],
        out_specs=pl.BlockSpec((t * rp, LANES), lambda b, be, nu: (clamp(b, nu), 0)),
        scratch_shapes=[pltpu.VMEM((d, hid), BF16), pltpu.VMEM((d, hid), BF16), pltpu.VMEM((hid, d), BF16)])
    vmem = 2 * (3 * _nbytes((d, hid), F32) + _nbytes((t, d // 2), U32) + _nbytes((t, d), F32)) \
        + 3 * _nbytes((d, hid), BF16) + 4 * _nbytes((t, d), F32)
    return pl.pallas_call(
        _expert_kernel,
        out_shape=jax.ShapeDtypeStruct((rows * rp, LANES), U32),
        grid_spec=grid_spec,
        compiler_params=_params(("arbitrary",), vmem),
    )(blk_exp, n_used, xs, wg, wu, wd)


def _combine_kernel(dest_ref, ys_hbm, x_ref, gate_ref, sg_ref, su_ref, sd_ref, g_ref, b_ref, o_ref,
                    buf_ref, sem_ref, *, tc, alpha):
    i = pl.program_id(0)
    rp = x_ref.shape[1] // PAIR
    half = i % 2

    def row_copy(src_row, dst_row, h):
        return pltpu.make_async_copy(ys_hbm.at[pl.ds(pl.multiple_of(src_row * rp, rp), rp)],
                                     buf_ref.at[pl.ds(pl.multiple_of(dst_row * rp, rp), rp)], sem_ref.at[h])

    def fetch(step, h):
        def per_token(t, carry):
            for k in range(TOP_K):
                src = dest_ref[(step * tc + t) * TOP_K + k]
                row_copy(src, (h * TOP_K + k) * tc + t, h).start(priority=k % 2)
            return carry
        lax.fori_loop(0, tc, per_token, 0)

    @pl.when(i == 0)
    def _():
        fetch(0, 0)

    @pl.when(i + 1 < pl.num_programs(0))
    def _():
        fetch(i + 1, 1 - half)

    x = x_ref[...]
    xb = x.astype(BF16)
    dot = functools.partial(jnp.dot, preferred_element_type=F32)
    hmid = (_silu(dot(xb, sg_ref[...])) * dot(xb, su_ref[...])).astype(BF16)
    shared = dot(hmid, sd_ref[...])

    def drain(t, carry):
        for k in range(TOP_K):
            row_copy(0, 0, half).wait()
        return carry

    lax.fori_loop(0, tc, drain, 0)

    gate = gate_ref[...]
    base = half * (TOP_K * tc * rp)
    cols = []
    for s in range(rp):
        lo = hi = None
        for k in range(TOP_K):
            w = buf_ref[pl.ds(base + k * tc * rp + s, tc, stride=rp), :]
            gk = gate[:, k:k + 1]
            lo_k = lax.bitcast_convert_type(w << 16, F32) * gk
            hi_k = lax.bitcast_convert_type(w & jnp.uint32(0xFFFF0000), F32) * gk
            lo = lo_k if lo is None else lo + lo_k
            hi = hi_k if hi is None else hi + hi_k
        cols += [lo, hi]
    routed = jnp.concatenate(cols, axis=1)
    o_ref[...] = _layer_norm_rows(alpha * x + (shared + routed), g_ref[...], b_ref[...])


def _combine(dest_tok, ys, x, gate_tok, sg, su, sd, g, b, alpha):
    n, d = x.shape
    hid = sg.shape[1]
    tc = _tile(n, 128)
    steps = n // tc
    rp = d // PAIR
    vec = pl.BlockSpec((1, d), lambda i, dest: (0, 0))
    full = lambda a: pl.BlockSpec(a.shape, lambda i, dest: (0, 0))
    buf_rows = 2 * TOP_K * tc * rp
    vmem = _nbytes((buf_rows, LANES), U32) + 2 * (2 * _nbytes((tc, d), F32) + 3 * _nbytes((d, hid), BF16)) \
        + 6 * _nbytes((tc, d), F32)
    grid_spec = pltpu.PrefetchScalarGridSpec(
        num_scalar_prefetch=1, grid=(steps,),
        in_specs=[pl.BlockSpec(memory_space=pl.ANY),
                  pl.BlockSpec((tc, d), lambda i, dest: (i, 0)),
                  pl.BlockSpec((tc, TOP_K), lambda i, dest: (i, 0)),
                  full(sg), full(su), full(sd), vec, vec],
        out_specs=pl.BlockSpec((tc, d), lambda i, dest: (i, 0)),
        scratch_shapes=[pltpu.VMEM((buf_rows, LANES), U32), pltpu.SemaphoreType.DMA((2,))])
    return pl.pallas_call(
        functools.partial(_combine_kernel, tc=tc, alpha=alpha),
        out_shape=jax.ShapeDtypeStruct((n, d), F32),
        grid_spec=grid_spec,
        compiler_params=_params(("arbitrary",), vmem),
    )(dest_tok, ys, x, gate_tok, sg, su, sd, g, b)


def _moe(x1, x1p, eid, rnk, gate, cnt, wg, wu, wd, layer, sg, su, sd, g, b, alpha):
    n, d = x1.shape
    ne = cnt.shape[0]
    counts = cnt[:, 0]
    padded = (counts + EXPERT_ROWS - 1) // EXPERT_ROWS * EXPERT_ROWS
    eidx = jnp.arange(ne, dtype=I32)
    pad_end = jnp.sum(jnp.where(eidx[None, :] <= eidx[:, None], padded[None, :], 0), axis=1)
    pad_start = pad_end - padded
    n_blocks = (n * TOP_K + ne * (EXPERT_ROWS - 1)) // EXPERT_ROWS
    rows = n_blocks * EXPERT_ROWS
    start_of = jnp.sum(jnp.where(eid[:, :, None] == eidx, pad_start, 0), axis=-1)
    dest_tok = (start_of + rnk).T.reshape(-1).astype(I32)
    pad_info = jnp.stack([pad_start + counts, padded - counts]).astype(I32)
    total = pad_end[ne - 1]
    n_used = (total // EXPERT_ROWS).astype(I32).reshape(1)
    blk_start = jnp.minimum(jnp.arange(n_blocks, dtype=I32) * EXPERT_ROWS, total - 1)
    blk_exp = jnp.sum((pad_end[None, :] <= blk_start[:, None]).astype(I32), axis=1)
    blk_exp = jnp.minimum(blk_exp, ne - 1).astype(I32)
    xs = _dispatch(pad_info, dest_tok, x1p, n, rows)
    ys = _experts(blk_exp, n_used, xs, wg, wu, wd, layer)
    return _combine(dest_tok, ys, x1, gate.T, sg, su, sd, g, b, alpha)


def kernel(x, mem, w_in, conv_w, w_conv_out, hg_lb_logits, hg_norm_g, w_hg_out, w_mem_k, w_mem_v,
           w_xa_out, w_o, ln1_g, ln1_b, router_w, router_b, exp_wg, exp_wu, exp_wd,
           sh_wg, sh_wu, sh_wd, ln2_g, ln2_b):
    bsz, seq, d = x.shape
    depth = w_in.shape[0]
    n_mem = mem.shape[1]
    cwid = conv_w.shape[2]
    hw = w_hg_out.shape[1]
    xw = w_mem_k.shape[2]
    alpha = float((2 * depth) ** 0.25)
    n = bsz * seq

    lb_all = jnp.cumsum(jax.nn.softmax(hg_lb_logits.astype(F32), axis=0), axis=0)
    lb_all = lb_all - lb_all[0:1]
    o_conv, o_hg, o_xa, o_gate = 0, 3 * cwid, 3 * cwid + 4 * hw, 3 * cwid + 4 * hw + xw

    h = x.reshape(n, d)
    memf = mem.reshape(bsz * n_mem, d)
    for l in range(depth):
        wl = w_in[l]
        w_conv_in = wl[:, o_conv:o_hg].astype(BF16)
        w_hg_in = wl[:, o_hg:o_xa].astype(BF16)
        w_xa_in = wl[:, o_xa:o_gate].astype(BF16)
        w_gates = wl[:, o_gate:].astype(BF16)
        vec = lambda a: a.astype(F32).reshape(1, -1)

        ya = _conv_branch(h, w_conv_in, conv_w[l].astype(F32), seq)
        lb = lb_all[l]
        yb = _hgrn_branch(h, w_hg_in, vec(jnp.log(lb)), vec(jnp.log1p(-lb)), vec(hg_norm_g[l]), bsz, seq)
        w_kv = jnp.concatenate([w_mem_k[l], w_mem_v[l]], axis=1).astype(BF16)
        kv = _matmul(memf, w_kv, BF16, 1024, xw)
        yx = _attn_branch(h, w_xa_in, kv, seq, n_mem)
        merged = _merge(h, ya, yb, yx, w_gates, w_conv_out[l].astype(BF16), w_hg_out[l].astype(BF16),
                        w_xa_out[l].astype(BF16))
        x1, x1p, eid, rnk, gate, cnt = _oproj_ln_route(
            merged, w_o[l].astype(BF16), h, vec(ln1_g[l]), vec(ln1_b[l]),
            router_w[l].astype(F32).T, router_b[l].astype(F32).reshape(-1, 1), alpha)
        h = _moe(x1, x1p, eid, rnk, gate, cnt, exp_wg, exp_wu, exp_wd, l,
                    sh_wg[l].astype(BF16), sh_wu[l].astype(BF16), sh_wd[l].astype(BF16),
                    vec(ln2_g[l]), vec(ln2_b[l]), alpha)
    return h.reshape(bsz, seq, d)
```

```python
import functools

import jax
import jax.numpy as jnp
import numpy as np
from jax import lax
from jax.experimental import pallas as pl
from jax.experimental.pallas import tpu as pltpu

F32 = jnp.float32
BF16 = jnp.bfloat16
I32 = jnp.int32
U32 = jnp.uint32

HG_DK = 128
XA_DH = 128
TOP_K = 8
N_GROUPS = 8
TOPK_GROUPS = 4
ROUTED_SCALE = 2.5
LN_EPS = 1e-5
RMS_EPS = 1e-6
HG_CHUNK = 64
HG_LEVELS = (1, 2, 4, 8, 16, 32)
EXPERT_ROWS = 256

V7X_VMEM_BYTES = 64 * 1024 * 1024
VMEM_CAP = V7X_VMEM_BYTES - 8 * 1024 * 1024


def _params(semantics, vmem_bytes):
    limit = int(min(VMEM_CAP, max(32 * 1024 * 1024, vmem_bytes * 5 // 4 + (4 << 20))))
    return pltpu.CompilerParams(dimension_semantics=semantics, vmem_limit_bytes=limit)


def _nbytes(shape, dtype):
    n = 1
    for s in shape:
        n *= s
    return n * jnp.dtype(dtype).itemsize


def _tile(n, want):
    t = min(n, want)
    while n % t:
        t //= 2
    return t


def _silu(x):
    return x * (1.0 / (1.0 + jnp.exp(-x)))


def _sigmoid(x):
    return 1.0 / (1.0 + jnp.exp(-x))


def _layer_norm_rows(y, g, b):
    mu = jnp.mean(y, axis=-1, keepdims=True)
    yc = y - mu
    var = jnp.mean(yc * yc, axis=-1, keepdims=True)
    return yc * lax.rsqrt(var + LN_EPS) * g + b


LANES = 128
PAIR = 2 * LANES


def _store_packed_rows(p_ref, y):
    rows, d = y.shape
    pieces = d // PAIR
    for s in range(pieces):
        lo = lax.bitcast_convert_type(y[:, s * PAIR:s * PAIR + LANES].astype(BF16).astype(F32), U32)
        hi = lax.bitcast_convert_type(y[:, s * PAIR + LANES:(s + 1) * PAIR].astype(BF16).astype(F32), U32)
        p_ref[pl.ds(s, rows, stride=pieces), :] = (hi & jnp.uint32(0xFFFF0000)) | (lo >> 16)


def _load_packed_rows(p_ref, rows, d):
    pieces = d // PAIR
    cols = []
    for s in range(pieces):
        w = p_ref[pl.ds(s, rows, stride=pieces), :]
        cols.append(lax.bitcast_convert_type(w << 16, F32).astype(BF16))
        cols.append(lax.bitcast_convert_type(w & jnp.uint32(0xFFFF0000), F32).astype(BF16))
    return jnp.concatenate(cols, axis=1)


def _matmul_kernel(x_ref, w_ref, o_ref):
    o_ref[...] = jnp.dot(x_ref[...].astype(BF16), w_ref[...],
                         preferred_element_type=F32).astype(o_ref.dtype)


def _matmul(x, w, out_dtype, tm, tn):
    m, k = x.shape
    n = w.shape[1]
    tm, tn = _tile(m, tm), _tile(n, tn)
    vmem = 2 * (_nbytes((tm, k), x.dtype) + _nbytes((k, tn), w.dtype) + _nbytes((tm, tn), out_dtype)) \
        + _nbytes((tm, k), BF16) + _nbytes((tm, tn), F32)
    return pl.pallas_call(
        _matmul_kernel,
        out_shape=jax.ShapeDtypeStruct((m, n), out_dtype),
        grid=(m // tm, n // tn),
        in_specs=[pl.BlockSpec((tm, k), lambda i, j: (i, 0)),
                  pl.BlockSpec((k, tn), lambda i, j: (0, j))],
        out_specs=pl.BlockSpec((tm, tn), lambda i, j: (i, j)),
        compiler_params=_params(("arbitrary", "arbitrary"), vmem),
    )(x, w)


def _conv_kernel(x_ref, w_ref, cw_ref, o_ref, carry_ref, *, tiles_per_seq):
    i = pl.program_id(0)
    tm, cwid = o_ref.shape

    @pl.when(i % tiles_per_seq == 0)
    def _():
        carry_ref[...] = jnp.zeros_like(carry_ref)

    p = jnp.dot(x_ref[...].astype(BF16), w_ref[...], preferred_element_type=F32)
    u = p[:, :cwid]
    b = p[:, cwid:2 * cwid]
    c = p[:, 2 * cwid:]
    cu = c * u
    prev = carry_ref[...]
    row = lax.broadcasted_iota(I32, (tm, cwid), 0)
    cu1 = jnp.where(row == 0, prev[7:8, :], pltpu.roll(cu, 1, 0))
    cu2 = pltpu.roll(cu, 2, 0)
    cu2 = jnp.where(row == 0, prev[6:7, :], jnp.where(row == 1, prev[7:8, :], cu2))
    cw = cw_ref[...]
    y = b * (cw[0:1, :] * cu2 + cw[1:2, :] * cu1 + cw[2:3, :] * cu)
    o_ref[...] = y.astype(o_ref.dtype)
    carry_ref[...] = cu[tm - 8:, :]


def _conv_branch(x, w_conv_in, conv_w, seq):
    n, d = x.shape
    cwid = conv_w.shape[1]
    assert conv_w.shape[0] == 3
    tm = _tile(seq, 512)
    vmem = 2 * (_nbytes((tm, d), x.dtype) + _nbytes(w_conv_in.shape, BF16) + _nbytes((tm, cwid), BF16)) \
        + 6 * _nbytes((tm, 3 * cwid), F32)
    return pl.pallas_call(
        functools.partial(_conv_kernel, tiles_per_seq=seq // tm),
        out_shape=jax.ShapeDtypeStruct((n, cwid), BF16),
        grid=(n // tm,),
        in_specs=[pl.BlockSpec((tm, d), lambda i: (i, 0)),
                  pl.BlockSpec(w_conv_in.shape, lambda i: (0, 0)),
                  pl.BlockSpec(conv_w.shape, lambda i: (0, 0))],
        out_specs=pl.BlockSpec((tm, cwid), lambda i: (i, 0)),
        scratch_shapes=[pltpu.VMEM((8, cwid), F32)],
        compiler_params=_params(("arbitrary",), vmem),
    )(x, w_conv_in, conv_w)


def _head_sum(x):
    return jnp.sum(x, axis=-1, keepdims=True)


def _hgrn_tables(tt):
    t = np.arange(tt)[:, None]
    j = np.arange(tt)[None, :]
    tri = ((t // HG_CHUNK) == (j // HG_CHUNK)) & (j <= t)
    level = np.zeros((tt, tt), np.int32)
    for i, h in enumerate(HG_LEVELS):
        level[((t // (2 * h)) == (j // (2 * h))) & ((t % (2 * h)) >= h) & ((j % (2 * h)) < h)] = i + 1
    return tri.astype(np.float32), level


def _hgrn_kernel(x_ref, w_ref, tri_ref, lvl_ref, c1_ref, c2_ref, gain_ref, o_ref,
                 st_ref, p_ref, u_ref, qd_ref, kd_ref, vb_ref, dec_ref, *, heads):
    tt, hw = o_ref.shape
    dk = HG_DK
    ch = HG_CHUNK
    nlev = len(HG_LEVELS)

    @pl.when(pl.program_id(1) == 0)
    def _():
        st_ref[...] = jnp.zeros_like(st_ref)

    p_ref[...] = jnp.dot(x_ref[...].astype(BF16), w_ref[...], preferred_element_type=F32)
    q_ref, f_ref, v_ref, g_ref = (p_ref.at[:, pl.ds(j * hw, hw)] for j in range(4))

    f = f_ref[...]
    log_sig = jnp.minimum(f, 0.0) - jnp.log1p(jnp.exp(-jnp.abs(f)))
    a1 = c1_ref[...]
    a2 = c2_ref[...] + log_sig
    log_f = jnp.maximum(a1, a2) + jnp.log1p(jnp.exp(-jnp.abs(a1 - a2)))
    kin = 1.0 - jnp.exp(log_f)

    t_hi = log_f.astype(BF16)
    rem = log_f - t_hi.astype(F32)
    t_mid = rem.astype(BF16)
    t_lo = (rem - t_mid.astype(F32)).astype(BF16)

    tri = tri_ref[...]
    dot = functools.partial(jnp.dot, preferred_element_type=F32)
    bcum = dot(tri, t_hi) + dot(tri, t_mid) + dot(tri, t_lo)

    def group_row(a, size, idx):
        a3 = a.reshape(tt // size, size, hw)
        return jnp.broadcast_to(a3[:, idx:idx + 1, :], a3.shape).reshape(tt, hw)

    blast = group_row(bcum, ch, ch - 1)
    row = lax.broadcasted_iota(I32, (tt, hw), 0)

    def level_decay(h):
        upper = (row % (2 * h)) >= h
        if h == 1:
            return upper, jnp.where(upper, 1.0 - kin, 1.0)
        if h == 2:
            pos = row % 4
            nxt = pltpu.roll(log_f, tt - 1, 0)
            prv = pltpu.roll(log_f, 1, 0)
            z = jnp.where(pos == 0, nxt, jnp.where(pos == 1, 0.0, jnp.where(pos == 2, log_f, log_f + prv)))
            return upper, jnp.exp(z)
        refb = group_row(bcum, 2 * h, h - 1)
        return upper, jnp.exp(jnp.where(upper, bcum - refb, refb - bcum))

    q = q_ref[...]
    qd_ref[...] = (q * jnp.exp(bcum)).astype(BF16)
    kd_ref[...] = (kin * jnp.exp(blast - bcum)).astype(BF16)
    vb_ref[...] = v_ref[...].astype(BF16)
    dec_ref[...] = jnp.exp(blast)

    for i, h in enumerate(HG_LEVELS):
        upper, decay = level_decay(h)
        u_ref[i] = (jnp.where(upper, q, kin) * decay).astype(BF16)

    lvl = lvl_ref[...]
    gain = gain_ref[...]
    nt_dims = (((1,), (1,)), ((), ()))
    for h in range(heads):
        ls = slice(h * dk, (h + 1) * dk)
        scores = jnp.zeros((tt, tt), F32)
        for i in range(nlev):
            u = u_ref[i, :, ls]
            gram = lax.dot_general(u, u, nt_dims, preferred_element_type=F32)
            scores = scores + jnp.where(lvl == i + 1, gram, 0.0)
        acc = jnp.dot(scores.astype(BF16), vb_ref[:, ls], preferred_element_type=F32)
        acc = acc + _head_sum(q[:, ls] * kin[:, ls]) * v_ref[:, ls]
        inter = []
        for c in range(tt // ch):
            rs = slice(c * ch, (c + 1) * ch)
            st = st_ref[h]
            inter.append(lax.dot_general(qd_ref[rs, ls], st.astype(BF16), nt_dims, preferred_element_type=F32))
            upd = lax.dot_general(vb_ref[rs, ls], kd_ref[rs, ls], (((0,), (0,)), ((), ())),
                                  preferred_element_type=F32)
            st_ref[h] = st * dec_ref[c * ch:c * ch + 1, ls] + upd
        acc = acc + jnp.concatenate(inter, axis=0)
        ms = _head_sum(acc * acc) * (1.0 / dk)
        y = acc * lax.rsqrt(ms + RMS_EPS) * gain[:, ls] * _silu(g_ref[:, ls])
        o_ref[:, ls] = y.astype(o_ref.dtype)


def _hgrn_branch(x, w_hg_in, c1, c2, gain, bsz, seq):
    n, d = x.shape
    hw = w_hg_in.shape[1] // 4
    heads = hw // HG_DK
    tt = _tile(seq, 256)
    assert tt % HG_CHUNK == 0
    nt = seq // tt
    nlev = len(HG_LEVELS)
    tri, level = _hgrn_tables(tt)
    tri = jnp.asarray(tri, BF16)
    level = jnp.asarray(level, I32)
    vec = pl.BlockSpec((1, hw), lambda b, t: (0, 0))
    vmem = 2 * (_nbytes((tt, d), x.dtype) + _nbytes(w_hg_in.shape, BF16) + _nbytes((tt, hw), BF16)
                + _nbytes(tri.shape, BF16) + _nbytes(level.shape, I32)) \
        + _nbytes((tt, d), BF16) + 16 * _nbytes((tt, hw), F32) + nlev * _nbytes((tt, hw), BF16) \
        + 4 * _nbytes((tt, tt), F32) + heads * HG_DK * HG_DK * 4
    return pl.pallas_call(
        functools.partial(_hgrn_kernel, heads=heads),
        out_shape=jax.ShapeDtypeStruct((n, hw), BF16),
        grid=(bsz, nt),
        in_specs=[pl.BlockSpec((tt, d), lambda b, t: (b * nt + t, 0)),
                  pl.BlockSpec(w_hg_in.shape, lambda b, t: (0, 0)),
                  pl.BlockSpec(tri.shape, lambda b, t: (0, 0)),
                  pl.BlockSpec(level.shape, lambda b, t: (0, 0)), vec, vec, vec],
        out_specs=pl.BlockSpec((tt, hw), lambda b, t: (b * nt + t, 0)),
        scratch_shapes=[pltpu.VMEM((heads, HG_DK, HG_DK), F32), pltpu.VMEM((tt, 4 * hw), F32),
                        pltpu.VMEM((nlev, tt, hw), BF16),
                        pltpu.VMEM((tt, hw), BF16), pltpu.VMEM((tt, hw), BF16),
                        pltpu.VMEM((tt, hw), BF16), pltpu.VMEM((tt, hw), F32)],
        compiler_params=_params(("arbitrary", "arbitrary"), vmem),
    )(x, w_hg_in, tri, level, c1, c2, gain)


def _attn_kernel(x_ref, wq_ref, k_ref, v_ref, o_ref, *, heads):
    dh = XA_DH
    q = jnp.dot(x_ref[...].astype(BF16), wq_ref[...], preferred_element_type=F32)
    scale = dh ** -0.5
    for h in range(heads):
        ls = slice(h * dh, (h + 1) * dh)
        s = lax.dot_general(q[:, ls].astype(BF16), k_ref[:, ls], (((1,), (1,)), ((), ())),
                            preferred_element_type=F32) * scale
        e = jnp.exp(s - jnp.max(s, axis=-1, keepdims=True))
        p = e / jnp.sum(e, axis=-1, keepdims=True)
        o = jnp.dot(p.astype(BF16), v_ref[:, ls], preferred_element_type=F32)
        o_ref[:, ls] = o.astype(o_ref.dtype)


def _attn_branch(x, wq, kv, seq, n_mem):
    n, d = x.shape
    xw = wq.shape[1]
    heads = xw // XA_DH
    tm = _tile(seq, 512)
    per_seq = seq // tm
    vmem = 2 * (_nbytes((tm, d), x.dtype) + _nbytes(wq.shape, BF16) + 2 * _nbytes((n_mem, xw), BF16)
                + _nbytes((tm, xw), BF16)) + 4 * _nbytes((tm, xw), F32) + 4 * _nbytes((tm, n_mem), F32)
    return pl.pallas_call(
        functools.partial(_attn_kernel, heads=heads),
        out_shape=jax.ShapeDtypeStruct((n, xw), BF16),
        grid=(n // tm,),
        in_specs=[pl.BlockSpec((tm, d), lambda i: (i, 0)),
                  pl.BlockSpec(wq.shape, lambda i: (0, 0)),
                  pl.BlockSpec((n_mem, xw), lambda i: (i // per_seq, 0)),
                  pl.BlockSpec((n_mem, xw), lambda i: (i // per_seq, 1))],
        out_specs=pl.BlockSpec((tm, xw), lambda i: (i, 0)),
        compiler_params=_params(("arbitrary",), vmem),
    )(x, wq, kv, kv)


def _merge_kernel(x_ref, ya_ref, yb_ref, yx_ref, wga_ref, wgb_ref, wgx_ref, wa_ref, wb_ref, wx_ref,
                  o_ref, xb_ref):
    @pl.when(pl.program_id(1) == 0)
    def _():
        xb_ref[...] = x_ref[...].astype(BF16)

    xb = xb_ref[...]
    dot = functools.partial(jnp.dot, preferred_element_type=F32)
    m = _sigmoid(dot(xb, wga_ref[...])) * dot(ya_ref[...], wa_ref[...])
    m = m + _sigmoid(dot(xb, wgb_ref[...])) * dot(yb_ref[...], wb_ref[...])
    m = m + _sigmoid(dot(xb, wgx_ref[...])) * dot(yx_ref[...], wx_ref[...])
    o_ref[...] = m.astype(o_ref.dtype)


def _merge(x, ya, yb, yx, w_gates, wa, wb, wx):
    n, d = x.shape
    tm, tn = _tile(n, 1024), _tile(d, 512)
    nj = d // tn
    row = lambda a: pl.BlockSpec((tm, a.shape[1]), lambda i, j: (i, 0))
    gate = lambda g: pl.BlockSpec((d, tn), lambda i, j, g=g: (0, g * nj + j))
    outw = lambda a: pl.BlockSpec((a.shape[0], tn), lambda i, j: (0, j))
    vmem = 2 * (_nbytes((tm, d), x.dtype) + _nbytes((tm, ya.shape[1] + yb.shape[1] + yx.shape[1]), BF16)
                + 3 * _nbytes((d, tn), BF16) + _nbytes((wa.shape[0] + wb.shape[0] + wx.shape[0], tn), BF16)
                + _nbytes((tm, tn), BF16)) + _nbytes((tm, d), BF16) + 8 * _nbytes((tm, tn), F32)
    return pl.pallas_call(
        _merge_kernel,
        out_shape=jax.ShapeDtypeStruct((n, d), BF16),
        grid=(n // tm, nj),
        in_specs=[row(x), row(ya), row(yb), row(yx), gate(0), gate(1), gate(2),
                  outw(wa), outw(wb), outw(wx)],
        out_specs=pl.BlockSpec((tm, tn), lambda i, j: (i, j)),
        scratch_shapes=[pltpu.VMEM((tm, d), BF16)],
        compiler_params=_params(("arbitrary", "arbitrary"), vmem),
    )(x, ya, yb, yx, w_gates, w_gates, w_gates, wa, wb, wx)


def _oproj_ln_kernel(m_ref, w_ref, x_ref, g_ref, b_ref, rw_ref, rb_ref,
                     o_ref, p_ref, eid_ref, rnk_ref, gate_ref, cnt_ref, carry_ref, *, alpha):
    y = alpha * x_ref[...] + jnp.dot(m_ref[...], w_ref[...], preferred_element_type=F32)
    y = _layer_norm_rows(y, g_ref[...], b_ref[...])
    o_ref[...] = y
    _store_packed_rows(p_ref, y)
    _route(y, rw_ref, rb_ref, eid_ref, rnk_ref, gate_ref, cnt_ref, carry_ref)


def _oproj_ln_route(merged, w_o, x, g, b, rw_t, rb, alpha):
    n, d = x.shape
    assert d % PAIR == 0
    pieces = d // PAIR
    ne = rw_t.shape[0]
    tm = _tile(n, 512)
    vec = pl.BlockSpec((1, d), lambda i: (0, 0))
    topk = lambda dt: jax.ShapeDtypeStruct((TOP_K, n), dt)
    kspec = pl.BlockSpec((TOP_K, tm), lambda i: (0, i))
    vmem = 2 * (_nbytes((tm, d), BF16) + _nbytes((d, d), BF16) + 2 * _nbytes((tm, d), F32)
                + _nbytes((tm, d // 2), U32) + _nbytes((ne, d), F32)) + 4 * _nbytes((tm, d), F32) \
        + 3 * _nbytes((tm, tm), F32) + 24 * _nbytes((ne, tm), F32)
    return pl.pallas_call(
        functools.partial(_oproj_ln_kernel, alpha=alpha),
        out_shape=(jax.ShapeDtypeStruct((n, d), F32), jax.ShapeDtypeStruct((n * pieces, LANES), U32),
                   topk(I32), topk(I32), topk(F32), jax.ShapeDtypeStruct((ne, 128), I32)),
        grid=(n // tm,),
        in_specs=[pl.BlockSpec((tm, d), lambda i: (i, 0)),
                  pl.BlockSpec((d, d), lambda i: (0, 0)),
                  pl.BlockSpec((tm, d), lambda i: (i, 0)), vec, vec,
                  pl.BlockSpec((ne, d), lambda i: (0, 0)),
                  pl.BlockSpec((ne, 1), lambda i: (0, 0))],
        out_specs=(pl.BlockSpec((tm, d), lambda i: (i, 0)),
                   pl.BlockSpec((tm * pieces, LANES), lambda i: (i, 0)),
                   kspec, kspec, kspec, pl.BlockSpec((ne, 128), lambda i: (0, 0))),
        scratch_shapes=[pltpu.VMEM((ne, 1), F32)],
        compiler_params=_params(("arbitrary",), vmem),
    )(merged, w_o, x, g, b, rw_t, rb)


def _rank_desc(vals):
    rows = vals.shape[0]
    iota = lax.broadcasted_iota(I32, vals.shape, 0)
    rank = jnp.zeros(vals.shape, I32)
    for j in range(rows):
        rowv = vals[j:j + 1, :]
        tie = jnp.where(iota > j, 1, 0)
        rank = rank + jnp.where(rowv > vals, 1, jnp.where(rowv == vals, tie, 0))
    return rank


def _route(x, rw_ref, rb_ref, eid_ref, rnk_ref, gate_ref, cnt_ref, carry_ref):
    ne = rw_ref.shape[0]
    tm = x.shape[0]
    gsz = ne // N_GROUPS

    @pl.when(pl.program_id(0) == 0)
    def _():
        carry_ref[...] = jnp.zeros_like(carry_ref)

    logits = lax.dot_general(rw_ref[...], x, (((1,), (1,)), ((), ())),
                             preferred_element_type=F32, precision=lax.Precision.HIGHEST)
    s = _sigmoid(logits)
    sel = s + rb_ref[...]

    grp = sel.reshape(N_GROUPS, gsz, tm)
    sub_iota = lax.broadcasted_iota(I32, (N_GROUPS, gsz, tm), 1)
    m1 = jnp.max(grp, axis=1, keepdims=True)
    first = jnp.min(jnp.where(grp == m1, sub_iota, gsz), axis=1, keepdims=True)
    m2 = jnp.max(jnp.where(sub_iota == first, -jnp.inf, grp), axis=1, keepdims=True)
    gscore = (m1 + m2).reshape(N_GROUPS, tm)

    gsel = _rank_desc(gscore) < TOPK_GROUPS
    emask = jnp.broadcast_to(gsel.reshape(N_GROUPS, 1, tm), (N_GROUPS, gsz, tm)).reshape(ne, tm)
    masked = jnp.where(emask, sel, -jnp.inf)

    eiota = lax.broadcasted_iota(I32, (ne, tm), 0)
    chosen = (_rank_desc(masked) < TOP_K) & emask

    w = jnp.where(chosen, s, 0.0)
    gate = w / jnp.sum(w, axis=0, keepdims=True) * ROUTED_SCALE

    ch = jnp.where(chosen, 1.0, 0.0).astype(BF16)
    tr = lax.broadcasted_iota(I32, (tm, tm), 0)
    tc = lax.broadcasted_iota(I32, (tm, tm), 1)
    before = jnp.where(tr < tc, 1.0, 0.0).astype(BF16)
    rank_tok = (carry_ref[...] + jnp.dot(ch, before, preferred_element_type=F32)).astype(I32)
    carry_ref[...] = carry_ref[...] + jnp.sum(jnp.where(chosen, 1.0, 0.0), axis=1, keepdims=True)
    cnt_ref[...] = jnp.broadcast_to(carry_ref[...], cnt_ref.shape).astype(I32)

    er = lax.broadcasted_iota(I32, (ne, ne), 0)
    ec = lax.broadcasted_iota(I32, (ne, ne), 1)
    lower = jnp.where(ec < er, 1.0, 0.0).astype(BF16)
    slot = jnp.dot(lower, ch, preferred_element_type=F32).astype(I32)
    for k in range(TOP_K):
        pick = chosen & (slot == k)
        eid_ref[k:k + 1, :] = jnp.sum(jnp.where(pick, eiota, 0), axis=0, keepdims=True)
        rnk_ref[k:k + 1, :] = jnp.sum(jnp.where(pick, rank_tok, 0), axis=0, keepdims=True)
        gate_ref[k:k + 1, :] = jnp.sum(jnp.where(pick, gate, 0.0), axis=0, keepdims=True)


def _pad_pieces(max_pad):
    pieces = []
    p = 1
    while p <= max_pad:
        pieces.append(p)
        p *= 2
    return pieces[::-1]


def _dispatch_kernel(pad_ref, dest_hbm, xp_ref, zero_hbm, xs_hbm, idx_ref, sem_idx, sem_row, sem_pad,
                     *, tq, ne, rp):
    i = pl.program_id(0)

    def row_copy(src_row, dst_row):
        return pltpu.make_async_copy(xp_ref.at[pl.ds(pl.multiple_of(src_row * rp, rp), rp)],
                                     xs_hbm.at[pl.ds(pl.multiple_of(dst_row * rp, rp), rp)], sem_row)

    def pad_copies(fn):
        def per_expert(e, carry):
            row = pad_ref[0, e]
            npad = pad_ref[1, e]
            for p in _pad_pieces(EXPERT_ROWS - 1):
                @pl.when((npad & p) != 0)
                def _(row=row, p=p):
                    fn(pltpu.make_async_copy(zero_hbm.at[pl.ds(0, p * rp)],
                                             xs_hbm.at[pl.ds(pl.multiple_of(row * rp, rp), p * rp)], sem_pad))
                row = row + (npad & p)
            return carry
        lax.fori_loop(0, ne, per_expert, 0)

    @pl.when(i == 0)
    def _():
        pad_copies(lambda cp: cp.start())

    idx_cp = pltpu.make_async_copy(dest_hbm.at[i], idx_ref, sem_idx)
    idx_cp.start()
    idx_cp.wait()

    def per_token(t, carry):
        for k in range(TOP_K):
            row_copy(t, idx_ref[t * TOP_K + k]).start(priority=k % 2)
        return carry

    lax.fori_loop(0, tq, per_token, 0)

    def drain(t, carry):
        for k in range(TOP_K):
            row_copy(0, 0).wait()
        return carry

    lax.fori_loop(0, tq, drain, 0)

    @pl.when(i == 0)
    def _():
        pad_copies(lambda cp: cp.wait())


def _dispatch(pad_info, dest_tok, xp, n, rows):
    rp = xp.shape[0] // n
    ne = pad_info.shape[1]
    tq = _tile(n, 1024)
    steps = n // tq
    dest2 = dest_tok.reshape(steps, tq * TOP_K)
    zero = jnp.zeros((EXPERT_ROWS * rp, LANES), U32)
    any_spec = pl.BlockSpec(memory_space=pl.ANY)
    grid_spec = pltpu.PrefetchScalarGridSpec(
        num_scalar_prefetch=1, grid=(steps,),
        in_specs=[any_spec, pl.BlockSpec((tq * rp, LANES), lambda i, pad: (i, 0)), any_spec],
        out_specs=any_spec,
        scratch_shapes=[pltpu.SMEM((tq * TOP_K,), I32), pltpu.SemaphoreType.DMA(()),
                        pltpu.SemaphoreType.DMA(()), pltpu.SemaphoreType.DMA(())])
    return pl.pallas_call(
        functools.partial(_dispatch_kernel, tq=tq, ne=ne, rp=rp),
        out_shape=jax.ShapeDtypeStruct((rows * rp, LANES), U32),
        grid_spec=grid_spec,
        compiler_params=_params(("arbitrary",), 2 * _nbytes((tq * rp, LANES), U32)),
    )(pad_info, dest2, xp, zero)


def _expert_kernel(be_ref, nu_ref, xs_ref, wg_ref, wu_ref, wd_ref, ys_ref, wgb_ref, wub_ref, wdb_ref):
    b = pl.program_id(0)

    @pl.when(b < nu_ref[0])
    def _():
        prev = be_ref[jnp.maximum(b - 1, 0)]

        @pl.when((b == 0) | (be_ref[b] != prev))
        def _():
            wgb_ref[...] = wg_ref[...].astype(BF16)
            wub_ref[...] = wu_ref[...].astype(BF16)
            wdb_ref[...] = wd_ref[...].astype(BF16)

        t = EXPERT_ROWS
        d = wgb_ref.shape[0]
        xb = _load_packed_rows(xs_ref, t, d)
        dot = functools.partial(jnp.dot, preferred_element_type=F32)
        hmid = (_silu(dot(xb, wgb_ref[...])) * dot(xb, wub_ref[...])).astype(BF16)
        _store_packed_rows(ys_ref, dot(hmid, wdb_ref[...]))


def _experts(blk_exp, n_used, xs, wg, wu, wd, layer):
    _, ne, d, hid = wg.shape
    t = EXPERT_ROWS
    rp = d // PAIR
    rows = xs.shape[0] // rp
    nb = rows // t
    clamp = lambda b, nu: jnp.minimum(b, jnp.maximum(nu[0] - 1, 0))
    wspec = lambda shape: pl.BlockSpec((None, None) + shape,
                                       lambda b, be, nu: (layer, be[clamp(b, nu)], 0, 0))
    grid_spec = pltpu.PrefetchScalarGridSpec(
        num_scalar_prefetch=2, grid=(nb,),
        in_specs=[pl.BlockSpec((t * rp, LANES), lambda b, be, nu: (clamp(b, nu), 0)),
                  wspec((d, hid)), wspec((d, hid)), wspec((hid, d))],
        out_specs=pl.BlockSpec((t * rp, LANES), lambda b, be, nu: (clamp(b, nu), 0)),
        scratch_shapes=[pltpu.VMEM((d, hid), BF16), pltpu.VMEM((d, hid), BF16), pltpu.VMEM((hid, d), BF16)])
    vmem = 2 * (3 * _nbytes((d, hid), F32) + _nbytes((t, d // 2), U32) + _nbytes((t, d), F32)) \
        + 3 * _nbytes((d, hid), BF16) + 4 * _nbytes((t, d), F32)
    return pl.pallas_call(
        _expert_kernel,
        out_shape=jax.ShapeDtypeStruct((rows * rp, LANES), U32),
        grid_spec=grid_spec,
        compiler_params=_params(("arbitrary",), vmem),
    )(blk_exp, n_used, xs, wg, wu, wd)


def _combine_kernel(dest_ref, ys_hbm, x_ref, gate_ref, sg_ref, su_ref, sd_ref, g_ref, b_ref, o_ref,
                    buf_ref, sem_ref, *, tc, alpha):
    i = pl.program_id(0)
    rp = x_ref.shape[1] // PAIR
    half = i % 2

    def row_copy(src_row, dst_row, h):
        return pltpu.make_async_copy(ys_hbm.at[pl.ds(pl.multiple_of(src_row * rp, rp), rp)],
                                     buf_ref.at[pl.ds(pl.multiple_of(dst_row * rp, rp), rp)], sem_ref.at[h])

    def fetch(step, h):
        def per_token(t, carry):
            for k in range(TOP_K):
                src = dest_ref[(step * tc + t) * TOP_K + k]
                row_copy(src, (h * TOP_K + k) * tc + t, h).start(priority=k % 2)
            return carry
        lax.fori_loop(0, tc, per_token, 0)

    @pl.when(i == 0)
    def _():
        fetch(0, 0)

    @pl.when(i + 1 < pl.num_programs(0))
    def _():
        fetch(i + 1, 1 - half)

    x = x_ref[...]
    xb = x.astype(BF16)
    dot = functools.partial(jnp.dot, preferred_element_type=F32)
    hmid = (_silu(dot(xb, sg_ref[...])) * dot(xb, su_ref[...])).astype(BF16)
    shared = dot(hmid, sd_ref[...])

    def drain(t, carry):
        for k in range(TOP_K):
            row_copy(0, 0, half).wait()
        return carry

    lax.fori_loop(0, tc, drain, 0)

    gate = gate_ref[...]
    base = half * (TOP_K * tc * rp)
    cols = []
    for s in range(rp):
        lo = hi = None
        for k in range(TOP_K):
            w = buf_ref[pl.ds(base + k * tc * rp + s, tc, stride=rp), :]
            gk = gate[:, k:k + 1]
            lo_k = lax.bitcast_convert_type(w << 16, F32) * gk
            hi_k = lax.bitcast_convert_type(w & jnp.uint32(0xFFFF0000), F32) * gk
            lo = lo_k if lo is None else lo + lo_k
            hi = hi_k if hi is None else hi + hi_k
        cols += [lo, hi]
    routed = jnp.concatenate(cols, axis=1)
    o_ref[...] = _layer_norm_rows(alpha * x + (shared + routed), g_ref[...], b_ref[...])


def _combine(dest_tok, ys, x, gate_tok, sg, su, sd, g, b, alpha):
    n, d = x.shape
    hid = sg.shape[1]
    tc = _tile(n, 128)
    steps = n // tc
    rp = d // PAIR
    vec = pl.BlockSpec((1, d), lambda i, dest: (0, 0))
    full = lambda a: pl.BlockSpec(a.shape, lambda i, dest: (0, 0))
    buf_rows = 2 * TOP_K * tc * rp
    vmem = _nbytes((buf_rows, LANES), U32) + 2 * (2 * _nbytes((tc, d), F32) + 3 * _nbytes((d, hid), BF16)) \
        + 6 * _nbytes((tc, d), F32)
    grid_spec = pltpu.PrefetchScalarGridSpec(
        num_scalar_prefetch=1, grid=(steps,),
        in_specs=[pl.BlockSpec(memory_space=pl.ANY),
                  pl.BlockSpec((tc, d), lambda i, dest: (i, 0)),
                  pl.BlockSpec((tc, TOP_K), lambda i, dest: (i, 0)),
                  full(sg), full(su), full(sd), vec, vec],
        out_specs=pl.BlockSpec((tc, d), lambda i, dest: (i, 0)),
        scratch_shapes=[pltpu.VMEM((buf_rows, LANES), U32), pltpu.SemaphoreType.DMA((2,))])
    return pl.pallas_call(
        functools.partial(_combine_kernel, tc=tc, alpha=alpha),
        out_shape=jax.ShapeDtypeStruct((n, d), F32),
        grid_spec=grid_spec,
        compiler_params=_params(("arbitrary",), vmem),
    )(dest_tok, ys, x, gate_tok, sg, su, sd, g, b)


def _moe(x1, x1p, eid, rnk, gate, cnt, wg, wu, wd, layer, sg, su, sd, g, b, alpha):
    n, d = x1.shape
    ne = cnt.shape[0]
    counts = cnt[:, 0]
    padded = (counts + EXPERT_ROWS - 1) // EXPERT_ROWS * EXPERT_ROWS
    eidx = jnp.arange(ne, dtype=I32)
    pad_end = jnp.sum(jnp.where(eidx[None, :] <= eidx[:, None], padded[None, :], 0), axis=1)
    pad_start = pad_end - padded
    n_blocks = (n * TOP_K + ne * (EXPERT_ROWS - 1)) // EXPERT_ROWS
    rows = n_blocks * EXPERT_ROWS
    start_of = jnp.sum(jnp.where(eid[:, :, None] == eidx, pad_start, 0), axis=-1)
    dest_tok = (start_of + rnk).T.reshape(-1).astype(I32)
    pad_info = jnp.stack([pad_start + counts, padded - counts]).astype(I32)
    total = pad_end[ne - 1]
    n_used = (total // EXPERT_ROWS).astype(I32).reshape(1)
    blk_start = jnp.minimum(jnp.arange(n_blocks, dtype=I32) * EXPERT_ROWS, total - 1)
    blk_exp = jnp.sum((pad_end[None, :] <= blk_start[:, None]).astype(I32), axis=1)
    blk_exp = jnp.minimum(blk_exp, ne - 1).astype(I32)
    xs = _dispatch(pad_info, dest_tok, x1p, n, rows)
    ys = _experts(blk_exp, n_used, xs, wg, wu, wd, layer)
    return _combine(dest_tok, ys, x1, gate.T, sg, su, sd, g, b, alpha)


def kernel(x, mem, w_in, conv_w, w_conv_out, hg_lb_logits, hg_norm_g, w_hg_out, w_mem_k, w_mem_v,
           w_xa_out, w_o, ln1_g, ln1_b, router_w, router_b, exp_wg, exp_wu, exp_wd,
           sh_wg, sh_wu, sh_wd, ln2_g, ln2_b):
    bsz, seq, d = x.shape
    depth = w_in.shape[0]
    n_mem = mem.shape[1]
    cwid = conv_w.shape[2]
    hw = w_hg_out.shape[1]
    xw = w_mem_k.shape[2]
    alpha = float((2 * depth) ** 0.25)
    n = bsz * seq

    lb_all = jnp.cumsum(jax.nn.softmax(hg_lb_logits.astype(F32), axis=0), axis=0)
    lb_all = lb_all - lb_all[0:1]
    o_conv, o_hg, o_xa, o_gate = 0, 3 * cwid, 3 * cwid + 4 * hw, 3 * cwid + 4 * hw + xw

    h = x.reshape(n, d)
    memf = mem.reshape(bsz * n_mem, d)
    for l in range(depth):
        wl = w_in[l]
        w_conv_in = wl[:, o_conv:o_hg].astype(BF16)
        w_hg_in = wl[:, o_hg:o_xa].astype(BF16)
        w_xa_in = wl[:, o_xa:o_gate].astype(BF16)
        w_gates = wl[:, o_gate:].astype(BF16)
        vec = lambda a: a.astype(F32).reshape(1, -1)

        ya = _conv_branch(h, w_conv_in, conv_w[l].astype(F32), seq)
        lb = lb_all[l]
        yb = _hgrn_branch(h, w_hg_in, vec(jnp.log(lb)), vec(jnp.log1p(-lb)), vec(hg_norm_g[l]), bsz, seq)
        w_kv = jnp.concatenate([w_mem_k[l], w_mem_v[l]], axis=1).astype(BF16)
        kv = _matmul(memf, w_kv, BF16, 1024, xw)
        yx = _attn_branch(h, w_xa_in, kv, seq, n_mem)
        merged = _merge(h, ya, yb, yx, w_gates, w_conv_out[l].astype(BF16), w_hg_out[l].astype(BF16),
                        w_xa_out[l].astype(BF16))
        x1, x1p, eid, rnk, gate, cnt = _oproj_ln_route(
            merged, w_o[l].astype(BF16), h, vec(ln1_g[l]), vec(ln1_b[l]),
            router_w[l].astype(F32).T, router_b[l].astype(F32).reshape(-1, 1), alpha)
        h = _moe(x1, x1p, eid, rnk, gate, cnt, exp_wg, exp_wu, exp_wd, l,
                    sh_wg[l].astype(BF16), sh_wu[l].astype(BF16), sh_wd[l].astype(BF16),
                    vec(ln2_g[l]), vec(ln2_b[l]), alpha)
    return h.reshape(bsz, seq, d)
```

```python
import functools

import jax
import jax.numpy as jnp
import numpy as np
from jax import lax
from jax.experimental import pallas as pl
from jax.experimental.pallas import tpu as pltpu

F32 = jnp.float32
BF16 = jnp.bfloat16
I32 = jnp.int32
U32 = jnp.uint32

HG_DK = 128
XA_DH = 128
TOP_K = 8
N_GROUPS = 8
TOPK_GROUPS = 4
ROUTED_SCALE = 2.5
LN_EPS = 1e-5
RMS_EPS = 1e-6
HG_CHUNK = 64
HG_LEVELS = (1, 2, 4, 8, 16, 32)
EXPERT_ROWS = 512

V7X_VMEM_BYTES = 64 * 1024 * 1024
VMEM_CAP = V7X_VMEM_BYTES - 8 * 1024 * 1024


def _params(semantics, vmem_bytes):
    limit = int(min(VMEM_CAP, max(32 * 1024 * 1024, vmem_bytes * 5 // 4 + (4 << 20))))
    return pltpu.CompilerParams(dimension_semantics=semantics, vmem_limit_bytes=limit)


def _nbytes(shape, dtype):
    n = 1
    for s in shape:
        n *= s
    return n * jnp.dtype(dtype).itemsize


def _tile(n, want):
    t = min(n, want)
    while n % t:
        t //= 2
    return t


def _silu(x):
    return x * (1.0 / (1.0 + jnp.exp(-x)))


def _sigmoid(x):
    return 1.0 / (1.0 + jnp.exp(-x))


def _layer_norm_rows(y, g, b):
    mu = jnp.mean(y, axis=-1, keepdims=True)
    yc = y - mu
    var = jnp.mean(yc * yc, axis=-1, keepdims=True)
    return yc * lax.rsqrt(var + LN_EPS) * g + b


LANES = 128
PAIR = 2 * LANES


def _store_packed_rows(p_ref, y):
    rows, d = y.shape
    pieces = d // PAIR
    for s in range(pieces):
        lo = lax.bitcast_convert_type(y[:, s * PAIR:s * PAIR + LANES].astype(BF16).astype(F32), U32)
        hi = lax.bitcast_convert_type(y[:, s * PAIR + LANES:(s + 1) * PAIR].astype(BF16).astype(F32), U32)
        p_ref[pl.ds(s, rows, stride=pieces), :] = (hi & jnp.uint32(0xFFFF0000)) | (lo >> 16)


def _load_packed_rows(p_ref, rows, d):
    pieces = d // PAIR
    cols = []
    for s in range(pieces):
        w = p_ref[pl.ds(s, rows, stride=pieces), :]
        cols.append(lax.bitcast_convert_type(w << 16, F32).astype(BF16))
        cols.append(lax.bitcast_convert_type(w & jnp.uint32(0xFFFF0000), F32).astype(BF16))
    return jnp.concatenate(cols, axis=1)


def _matmul_kernel(x_ref, w_ref, o_ref):
    o_ref[...] = jnp.dot(x_ref[...].astype(BF16), w_ref[...],
                         preferred_element_type=F32).astype(o_ref.dtype)


def _matmul(x, w, out_dtype, tm, tn):
    m, k = x.shape
    n = w.shape[1]
    tm, tn = _tile(m, tm), _tile(n, tn)
    vmem = 2 * (_nbytes((tm, k), x.dtype) + _nbytes((k, tn), w.dtype) + _nbytes((tm, tn), out_dtype)) \
        + _nbytes((tm, k), BF16) + _nbytes((tm, tn), F32)
    return pl.pallas_call(
        _matmul_kernel,
        out_shape=jax.ShapeDtypeStruct((m, n), out_dtype),
        grid=(m // tm, n // tn),
        in_specs=[pl.BlockSpec((tm, k), lambda i, j: (i, 0)),
                  pl.BlockSpec((k, tn), lambda i, j: (0, j))],
        out_specs=pl.BlockSpec((tm, tn), lambda i, j: (i, j)),
        compiler_params=_params(("arbitrary", "arbitrary"), vmem),
    )(x, w)


def _conv_kernel(x_ref, w_ref, cw_ref, o_ref, carry_ref, *, tiles_per_seq):
    i = pl.program_id(0)
    tm, cwid = o_ref.shape

    @pl.when(i % tiles_per_seq == 0)
    def _():
        carry_ref[...] = jnp.zeros_like(carry_ref)

    p = jnp.dot(x_ref[...].astype(BF16), w_ref[...], preferred_element_type=F32)
    u = p[:, :cwid]
    b = p[:, cwid:2 * cwid]
    c = p[:, 2 * cwid:]
    cu = c * u
    prev = carry_ref[...]
    row = lax.broadcasted_iota(I32, (tm, cwid), 0)
    cu1 = jnp.where(row == 0, prev[7:8, :], pltpu.roll(cu, 1, 0))
    cu2 = pltpu.roll(cu, 2, 0)
    cu2 = jnp.where(row == 0, prev[6:7, :], jnp.where(row == 1, prev[7:8, :], cu2))
    cw = cw_ref[...]
    y = b * (cw[0:1, :] * cu2 + cw[1:2, :] * cu1 + cw[2:3, :] * cu)
    o_ref[...] = y.astype(o_ref.dtype)
    carry_ref[...] = cu[tm - 8:, :]


def _conv_branch(x, w_conv_in, conv_w, seq):
    n, d = x.shape
    cwid = conv_w.shape[1]
    assert conv_w.shape[0] == 3
    tm = _tile(seq, 512)
    vmem = 2 * (_nbytes((tm, d), x.dtype) + _nbytes(w_conv_in.shape, BF16) + _nbytes((tm, cwid), BF16)) \
        + 6 * _nbytes((tm, 3 * cwid), F32)
    return pl.pallas_call(
        functools.partial(_conv_kernel, tiles_per_seq=seq // tm),
        out_shape=jax.ShapeDtypeStruct((n, cwid), BF16),
        grid=(n // tm,),
        in_specs=[pl.BlockSpec((tm, d), lambda i: (i, 0)),
                  pl.BlockSpec(w_conv_in.shape, lambda i: (0, 0)),
                  pl.BlockSpec(conv_w.shape, lambda i: (0, 0))],
        out_specs=pl.BlockSpec((tm, cwid), lambda i: (i, 0)),
        scratch_shapes=[pltpu.VMEM((8, cwid), F32)],
        compiler_params=_params(("arbitrary",), vmem),
    )(x, w_conv_in, conv_w)


def _head_sum(x):
    return jnp.sum(x, axis=-1, keepdims=True)


def _hgrn_tables(tt):
    t = np.arange(tt)[:, None]
    j = np.arange(tt)[None, :]
    tri = ((t // HG_CHUNK) == (j // HG_CHUNK)) & (j <= t)
    level = np.zeros((tt, tt), np.int32)
    for i, h in enumerate(HG_LEVELS):
        level[((t // (2 * h)) == (j // (2 * h))) & ((t % (2 * h)) >= h) & ((j % (2 * h)) < h)] = i + 1
    return tri.astype(np.float32), level


def _hgrn_kernel(x_ref, w_ref, tri_ref, lvl_ref, c1_ref, c2_ref, gain_ref, o_ref,
                 st_ref, p_ref, u_ref, qd_ref, kd_ref, vb_ref, dec_ref, *, heads):
    tt, hw = o_ref.shape
    dk = HG_DK
    ch = HG_CHUNK
    nlev = len(HG_LEVELS)

    @pl.when(pl.program_id(1) == 0)
    def _():
        st_ref[...] = jnp.zeros_like(st_ref)

    p_ref[...] = jnp.dot(x_ref[...].astype(BF16), w_ref[...], preferred_element_type=F32)
    q_ref, f_ref, v_ref, g_ref = (p_ref.at[:, pl.ds(j * hw, hw)] for j in range(4))

    f = f_ref[...]
    log_sig = jnp.minimum(f, 0.0) - jnp.log1p(jnp.exp(-jnp.abs(f)))
    a1 = c1_ref[...]
    a2 = c2_ref[...] + log_sig
    log_f = jnp.maximum(a1, a2) + jnp.log1p(jnp.exp(-jnp.abs(a1 - a2)))
    kin = 1.0 - jnp.exp(log_f)

    t_hi = log_f.astype(BF16)
    rem = log_f - t_hi.astype(F32)
    t_mid = rem.astype(BF16)
    t_lo = (rem - t_mid.astype(F32)).astype(BF16)

    tri = tri_ref[...]
    dot = functools.partial(jnp.dot, preferred_element_type=F32)
    bcum = dot(tri, t_hi) + dot(tri, t_mid) + dot(tri, t_lo)

    def group_row(a, size, idx):
        a3 = a.reshape(tt // size, size, hw)
        return jnp.broadcast_to(a3[:, idx:idx + 1, :], a3.shape).reshape(tt, hw)

    blast = group_row(bcum, ch, ch - 1)
    row = lax.broadcasted_iota(I32, (tt, hw), 0)

    def level_decay(h):
        upper = (row % (2 * h)) >= h
        if h == 1:
            return upper, jnp.where(upper, 1.0 - kin, 1.0)
        if h == 2:
            pos = row % 4
            nxt = pltpu.roll(log_f, tt - 1, 0)
            prv = pltpu.roll(log_f, 1, 0)
            z = jnp.where(pos == 0, nxt, jnp.where(pos == 1, 0.0, jnp.where(pos == 2, log_f, log_f + prv)))
            return upper, jnp.exp(z)
        refb = group_row(bcum, 2 * h, h - 1)
        return upper, jnp.exp(jnp.where(upper, bcum - refb, refb - bcum))

    q = q_ref[...]
    qd_ref[...] = (q * jnp.exp(bcum)).astype(BF16)
    kd_ref[...] = (kin * jnp.exp(blast - bcum)).astype(BF16)
    vb_ref[...] = v_ref[...].astype(BF16)
    dec_ref[...] = jnp.exp(blast)

    for i, h in enumerate(HG_LEVELS):
        upper, decay = level_decay(h)
        u_ref[i] = (jnp.where(upper, q, kin) * decay).astype(BF16)

    lvl = lvl_ref[...]
    gain = gain_ref[...]
    nt_dims = (((1,), (1,)), ((), ()))
    for h in range(heads):
        ls = slice(h * dk, (h + 1) * dk)
        scores = jnp.zeros((tt, tt), F32)
        for i in range(nlev):
            u = u_ref[i, :, ls]
            gram = lax.dot_general(u, u, nt_dims, preferred_element_type=F32)
            scores = scores + jnp.where(lvl == i + 1, gram, 0.0)
        acc = jnp.dot(scores.astype(BF16), vb_ref[:, ls], preferred_element_type=F32)
        acc = acc + _head_sum(q[:, ls] * kin[:, ls]) * v_ref[:, ls]
        inter = []
        for c in range(tt // ch):
            rs = slice(c * ch, (c + 1) * ch)
            st = st_ref[h]
            inter.append(lax.dot_general(qd_ref[rs, ls], st.astype(BF16), nt_dims, preferred_element_type=F32))
            upd = lax.dot_general(vb_ref[rs, ls], kd_ref[rs, ls], (((0,), (0,)), ((), ())),
                                  preferred_element_type=F32)
            st_ref[h] = st * dec_ref[c * ch:c * ch + 1, ls] + upd
        acc = acc + jnp.concatenate(inter, axis=0)
        ms = _head_sum(acc * acc) * (1.0 / dk)
        y = acc * lax.rsqrt(ms + RMS_EPS) * gain[:, ls] * _silu(g_ref[:, ls])
        o_ref[:, ls] = y.astype(o_ref.dtype)


def _hgrn_branch(x, w_hg_in, c1, c2, gain, bsz, seq):
    n, d = x.shape
    hw = w_hg_in.shape[1] // 4
    heads = hw // HG_DK
    tt = _tile(seq, 256)
    assert tt % HG_CHUNK == 0
    nt = seq // tt
    nlev = len(HG_LEVELS)
    tri, level = _hgrn_tables(tt)
    tri = jnp.asarray(tri, BF16)
    level = jnp.asarray(level, I32)
    vec = pl.BlockSpec((1, hw), lambda b, t: (0, 0))
    vmem = 2 * (_nbytes((tt, d), x.dtype) + _nbytes(w_hg_in.shape, BF16) + _nbytes((tt, hw), BF16)
                + _nbytes(tri.shape, BF16) + _nbytes(level.shape, I32)) \
        + _nbytes((tt, d), BF16) + 16 * _nbytes((tt, hw), F32) + nlev * _nbytes((tt, hw), BF16) \
        + 4 * _nbytes((tt, tt), F32) + heads * HG_DK * HG_DK * 4
    return pl.pallas_call(
        functools.partial(_hgrn_kernel, heads=heads),
        out_shape=jax.ShapeDtypeStruct((n, hw), BF16),
        grid=(bsz, nt),
        in_specs=[pl.BlockSpec((tt, d), lambda b, t: (b * nt + t, 0)),
                  pl.BlockSpec(w_hg_in.shape, lambda b, t: (0, 0)),
                  pl.BlockSpec(tri.shape, lambda b, t: (0, 0)),
                  pl.BlockSpec(level.shape, lambda b, t: (0, 0)), vec, vec, vec],
        out_specs=pl.BlockSpec((tt, hw), lambda b, t: (b * nt + t, 0)),
        scratch_shapes=[pltpu.VMEM((heads, HG_DK, HG_DK), F32), pltpu.VMEM((tt, 4 * hw), F32),
                        pltpu.VMEM((nlev, tt, hw), BF16),
                        pltpu.VMEM((tt, hw), BF16), pltpu.VMEM((tt, hw), BF16),
                        pltpu.VMEM((tt, hw), BF16), pltpu.VMEM((tt, hw), F32)],
        compiler_params=_params(("arbitrary", "arbitrary"), vmem),
    )(x, w_hg_in, tri, level, c1, c2, gain)


def _attn_kernel(x_ref, wq_ref, k_ref, v_ref, o_ref, *, heads):
    dh = XA_DH
    q = jnp.dot(x_ref[...].astype(BF16), wq_ref[...], preferred_element_type=F32)
    scale = dh ** -0.5
    for h in range(heads):
        ls = slice(h * dh, (h + 1) * dh)
        s = lax.dot_general(q[:, ls].astype(BF16), k_ref[:, ls], (((1,), (1,)), ((), ())),
                            preferred_element_type=F32) * scale
        e = jnp.exp(s - jnp.max(s, axis=-1, keepdims=True))
        p = e / jnp.sum(e, axis=-1, keepdims=True)
        o = jnp.dot(p.astype(BF16), v_ref[:, ls], preferred_element_type=F32)
        o_ref[:, ls] = o.astype(o_ref.dtype)


def _attn_branch(x, wq, kv, seq, n_mem):
    n, d = x.shape
    xw = wq.shape[1]
    heads = xw // XA_DH
    tm = _tile(seq, 512)
    per_seq = seq // tm
    vmem = 2 * (_nbytes((tm, d), x.dtype) + _nbytes(wq.shape, BF16) + 2 * _nbytes((n_mem, xw), BF16)
                + _nbytes((tm, xw), BF16)) + 4 * _nbytes((tm, xw), F32) + 4 * _nbytes((tm, n_mem), F32)
    return pl.pallas_call(
        functools.partial(_attn_kernel, heads=heads),
        out_shape=jax.ShapeDtypeStruct((n, xw), BF16),
        grid=(n // tm,),
        in_specs=[pl.BlockSpec((tm, d), lambda i: (i, 0)),
                  pl.BlockSpec(wq.shape, lambda i: (0, 0)),
                  pl.BlockSpec((n_mem, xw), lambda i: (i // per_seq, 0)),
                  pl.BlockSpec((n_mem, xw), lambda i: (i // per_seq, 1))],
        out_specs=pl.BlockSpec((tm, xw), lambda i: (i, 0)),
        compiler_params=_params(("arbitrary",), vmem),
    )(x, wq, kv, kv)


def _merge_kernel(x_ref, ya_ref, yb_ref, yx_ref, wga_ref, wgb_ref, wgx_ref, wa_ref, wb_ref, wx_ref,
                  o_ref, xb_ref):
    @pl.when(pl.program_id(1) == 0)
    def _():
        xb_ref[...] = x_ref[...].astype(BF16)

    xb = xb_ref[...]
    dot = functools.partial(jnp.dot, preferred_element_type=F32)
    m = _sigmoid(dot(xb, wga_ref[...])) * dot(ya_ref[...], wa_ref[...])
    m = m + _sigmoid(dot(xb, wgb_ref[...])) * dot(yb_ref[...], wb_ref[...])
    m = m + _sigmoid(dot(xb, wgx_ref[...])) * dot(yx_ref[...], wx_ref[...])
    o_ref[...] = m.astype(o_ref.dtype)


def _merge(x, ya, yb, yx, w_gates, wa, wb, wx):
    n, d = x.shape
    tm, tn = _tile(n, 1024), _tile(d, 512)
    nj = d // tn
    row = lambda a: pl.BlockSpec((tm, a.shape[1]), lambda i, j: (i, 0))
    gate = lambda g: pl.BlockSpec((d, tn), lambda i, j, g=g: (0, g * nj + j))
    outw = lambda a: pl.BlockSpec((a.shape[0], tn), lambda i, j: (0, j))
    vmem = 2 * (_nbytes((tm, d), x.dtype) + _nbytes((tm, ya.shape[1] + yb.shape[1] + yx.shape[1]), BF16)
                + 3 * _nbytes((d, tn), BF16) + _nbytes((wa.shape[0] + wb.shape[0] + wx.shape[0], tn), BF16)
                + _nbytes((tm, tn), BF16)) + _nbytes((tm, d), BF16) + 8 * _nbytes((tm, tn), F32)
    return pl.pallas_call(
        _merge_kernel,
        out_shape=jax.ShapeDtypeStruct((n, d), BF16),
        grid=(n // tm, nj),
        in_specs=[row(x), row(ya), row(yb), row(yx), gate(0), gate(1), gate(2),
                  outw(wa), outw(wb), outw(wx)],
        out_specs=pl.BlockSpec((tm, tn), lambda i, j: (i, j)),
        scratch_shapes=[pltpu.VMEM((tm, d), BF16)],
        compiler_params=_params(("arbitrary", "arbitrary"), vmem),
    )(x, ya, yb, yx, w_gates, w_gates, w_gates, wa, wb, wx)


def _oproj_ln_kernel(m_ref, w_ref, x_ref, g_ref, b_ref, rw_ref, rb_ref,
                     o_ref, p_ref, eid_ref, rnk_ref, gate_ref, cnt_ref, carry_ref, *, alpha):
    y = alpha * x_ref[...] + jnp.dot(m_ref[...], w_ref[...], preferred_element_type=F32)
    y = _layer_norm_rows(y, g_ref[...], b_ref[...])
    o_ref[...] = y
    _store_packed_rows(p_ref, y)
    _route(y, rw_ref, rb_ref, eid_ref, rnk_ref, gate_ref, cnt_ref, carry_ref)


def _oproj_ln_route(merged, w_o, x, g, b, rw_t, rb, alpha):
    n, d = x.shape
    assert d % PAIR == 0
    pieces = d // PAIR
    ne = rw_t.shape[0]
    tm = _tile(n, 512)
    cur = lambda i: (i, 0)
    vec = pl.BlockSpec((1, d), lambda i: (0, 0))
    topk = lambda dt: jax.ShapeDtypeStruct((TOP_K, n), dt)
    kspec = pl.BlockSpec((TOP_K, tm), lambda i: (0, i))
    vmem = 2 * (_nbytes((tm, d), BF16) + _nbytes((d, d), BF16) + 2 * _nbytes((tm, d), F32)
                + _nbytes((tm, d // 2), U32) + _nbytes((ne, d), F32)) + 4 * _nbytes((tm, d), F32) \
        + 3 * _nbytes((tm, tm), F32) + 24 * _nbytes((ne, tm), F32)
    return pl.pallas_call(
        functools.partial(_oproj_ln_kernel, alpha=alpha),
        out_shape=(jax.ShapeDtypeStruct((n, d), F32), jax.ShapeDtypeStruct((n * pieces, LANES), U32),
                   topk(I32), topk(I32), topk(F32), jax.ShapeDtypeStruct((ne, 128), I32)),
        grid=(n // tm,),
        in_specs=[pl.BlockSpec((tm, d), cur),
                  pl.BlockSpec((d, d), lambda i: (0, 0)),
                  pl.BlockSpec((tm, d), cur), vec, vec,
                  pl.BlockSpec((ne, d), lambda i: (0, 0)),
                  pl.BlockSpec((ne, 1), lambda i: (0, 0))],
        out_specs=(pl.BlockSpec((tm, d), cur),
                   pl.BlockSpec((tm * pieces, LANES), cur),
                   kspec, kspec, kspec, pl.BlockSpec((ne, 128), lambda i: (0, 0))),
        scratch_shapes=[pltpu.VMEM((ne, 1), F32)],
        compiler_params=_params(("arbitrary",), vmem),
    )(merged, w_o, x, g, b, rw_t, rb)


def _rank_desc(vals):
    rows = vals.shape[0]
    iota = lax.broadcasted_iota(I32, vals.shape, 0)
    rank = jnp.zeros(vals.shape, I32)
    for j in range(rows):
        rowv = vals[j:j + 1, :]
        tie = jnp.where(iota > j, 1, 0)
        rank = rank + jnp.where(rowv > vals, 1, jnp.where(rowv == vals, tie, 0))
    return rank


def _route(x, rw_ref, rb_ref, eid_ref, rnk_ref, gate_ref, cnt_ref, carry_ref):
    ne = rw_ref.shape[0]
    tm = x.shape[0]
    gsz = ne // N_GROUPS

    @pl.when(pl.program_id(0) == 0)
    def _():
        carry_ref[...] = jnp.zeros_like(carry_ref)

    logits = lax.dot_general(rw_ref[...], x, (((1,), (1,)), ((), ())),
                             preferred_element_type=F32, precision=lax.Precision.HIGHEST)
    s = _sigmoid(logits)
    sel = s + rb_ref[...]

    grp = sel.reshape(N_GROUPS, gsz, tm)
    sub_iota = lax.broadcasted_iota(I32, (N_GROUPS, gsz, tm), 1)
    m1 = jnp.max(grp, axis=1, keepdims=True)
    first = jnp.min(jnp.where(grp == m1, sub_iota, gsz), axis=1, keepdims=True)
    m2 = jnp.max(jnp.where(sub_iota == first, -jnp.inf, grp), axis=1, keepdims=True)
    gscore = (m1 + m2).reshape(N_GROUPS, tm)

    gsel = _rank_desc(gscore) < TOPK_GROUPS
    emask = jnp.broadcast_to(gsel.reshape(N_GROUPS, 1, tm), (N_GROUPS, gsz, tm)).reshape(ne, tm)
    masked = jnp.where(emask, sel, -jnp.inf)

    eiota = lax.broadcasted_iota(I32, (ne, tm), 0)
    chosen = (_rank_desc(masked) < TOP_K) & emask

    w = jnp.where(chosen, s, 0.0)
    gate = w / jnp.sum(w, axis=0, keepdims=True) * ROUTED_SCALE

    ch = jnp.where(chosen, 1.0, 0.0).astype(BF16)
    tr = lax.broadcasted_iota(I32, (tm, tm), 0)
    tc = lax.broadcasted_iota(I32, (tm, tm), 1)
    before = jnp.where(tr < tc, 1.0, 0.0).astype(BF16)
    rank_tok = (carry_ref[...] + jnp.dot(ch, before, preferred_element_type=F32)).astype(I32)
    carry_ref[...] = carry_ref[...] + jnp.sum(jnp.where(chosen, 1.0, 0.0), axis=1, keepdims=True)
    cnt_ref[...] = jnp.broadcast_to(carry_ref[...], cnt_ref.shape).astype(I32)

    er = lax.broadcasted_iota(I32, (ne, ne), 0)
    ec = lax.broadcasted_iota(I32, (ne, ne), 1)
    lower = jnp.where(ec < er, 1.0, 0.0).astype(BF16)
    slot = jnp.dot(lower, ch, preferred_element_type=F32).astype(I32)
    for k in range(TOP_K):
        pick = chosen & (slot == k)
        eid_ref[k:k + 1, :] = jnp.sum(jnp.where(pick, eiota, 0), axis=0, keepdims=True)
        rnk_ref[k:k + 1, :] = jnp.sum(jnp.where(pick, rank_tok, 0), axis=0, keepdims=True)
        gate_ref[k:k + 1, :] = jnp.sum(jnp.where(pick, gate, 0.0), axis=0, keepdims=True)


def _pad_pieces(max_pad):
    pieces = []
    p = 1
    while p <= max_pad:
        pieces.append(p)
        p *= 2
    return pieces[::-1]


def _dispatch_kernel(pad_ref, dest_hbm, xp_ref, zero_hbm, sg_ref, su_ref, sd_ref, xs_hbm, sh_ref,
                     idx_ref, sem_idx, sem_row, sem_pad, *, tq, ne, rp):
    i = pl.program_id(0)

    def row_copy(src_row, dst_row):
        return pltpu.make_async_copy(xp_ref.at[pl.ds(pl.multiple_of(src_row * rp, rp), rp)],
                                     xs_hbm.at[pl.ds(pl.multiple_of(dst_row * rp, rp), rp)], sem_row)

    def pad_copies(fn):
        def per_expert(e, carry):
            row = pad_ref[0, e]
            npad = pad_ref[1, e]
            for p in _pad_pieces(EXPERT_ROWS - 1):
                @pl.when((npad & p) != 0)
                def _(row=row, p=p):
                    fn(pltpu.make_async_copy(zero_hbm.at[pl.ds(0, p * rp)],
                                             xs_hbm.at[pl.ds(pl.multiple_of(row * rp, rp), p * rp)], sem_pad))
                row = row + (npad & p)
            return carry
        lax.fori_loop(0, ne, per_expert, 0)

    @pl.when(i == 0)
    def _():
        pad_copies(lambda cp: cp.start())

    idx_cp = pltpu.make_async_copy(dest_hbm.at[i], idx_ref, sem_idx)
    idx_cp.start()
    idx_cp.wait()

    def per_token(t, carry):
        for k in range(TOP_K):
            row_copy(t, idx_ref[t * TOP_K + k]).start(priority=k % 2)
        return carry

    lax.fori_loop(0, tq, per_token, 0)

    xb = _load_packed_rows(xp_ref, tq, sh_ref.shape[1])
    dot = functools.partial(jnp.dot, preferred_element_type=F32)
    hmid = (_silu(dot(xb, sg_ref[...])) * dot(xb, su_ref[...])).astype(BF16)
    sh_ref[...] = dot(hmid, sd_ref[...])

    def drain(t, carry):
        for k in range(TOP_K):
            row_copy(0, 0).wait()
        return carry

    lax.fori_loop(0, tq, drain, 0)

    @pl.when(i == 0)
    def _():
        pad_copies(lambda cp: cp.wait())


def _dispatch(pad_info, dest_tok, xp, n, rows, sg, su, sd):
    rp = xp.shape[0] // n
    d, hid = sg.shape
    ne = pad_info.shape[1]
    tq = _tile(n, 512)
    steps = n // tq
    dest2 = dest_tok.reshape(steps, tq * TOP_K)
    zero = jnp.zeros((EXPERT_ROWS * rp, LANES), U32)
    any_spec = pl.BlockSpec(memory_space=pl.ANY)
    full = lambda a: pl.BlockSpec(a.shape, lambda i, pad: (0, 0))
    grid_spec = pltpu.PrefetchScalarGridSpec(
        num_scalar_prefetch=1, grid=(steps,),
        in_specs=[any_spec, pl.BlockSpec((tq * rp, LANES), lambda i, pad: (i, 0)), any_spec,
                  full(sg), full(su), full(sd)],
        out_specs=(any_spec, pl.BlockSpec((tq, d), lambda i, pad: (i, 0))),
        scratch_shapes=[pltpu.SMEM((tq * TOP_K,), I32), pltpu.SemaphoreType.DMA(()),
                        pltpu.SemaphoreType.DMA(()), pltpu.SemaphoreType.DMA(())])
    vmem = 2 * (_nbytes((tq * rp, LANES), U32) + 3 * _nbytes((d, hid), BF16) + _nbytes((tq, d), F32)) \
        + _nbytes((tq, d), BF16) + 3 * _nbytes((tq, hid), F32) + _nbytes((tq, d), F32)
    return pl.pallas_call(
        functools.partial(_dispatch_kernel, tq=tq, ne=ne, rp=rp),
        out_shape=(jax.ShapeDtypeStruct((rows * rp, LANES), U32), jax.ShapeDtypeStruct((n, d), F32)),
        grid_spec=grid_spec,
        compiler_params=_params(("arbitrary",), vmem),
    )(pad_info, dest2, xp, zero, sg, su, sd)


def _expert_kernel(be_ref, nu_ref, xs_ref, wg_ref, wu_ref, wd_ref, ys_ref, wgb_ref, wub_ref, wdb_ref):
    b = pl.program_id(0)

    @pl.when(b < nu_ref[0])
    def _():
        prev = be_ref[jnp.maximum(b - 1, 0)]

        @pl.when((b == 0) | (be_ref[b] != prev))
        def _():
            wgb_ref[...] = wg_ref[...].astype(BF16)
            wub_ref[...] = wu_ref[...].astype(BF16)
            wdb_ref[...] = wd_ref[...].astype(BF16)

        t = EXPERT_ROWS
        d = wgb_ref.shape[0]
        xb = _load_packed_rows(xs_ref, t, d)
        dot = functools.partial(jnp.dot, preferred_element_type=F32)
        hmid = (_silu(dot(xb, wgb_ref[...])) * dot(xb, wub_ref[...])).astype(BF16)
        _store_packed_rows(ys_ref, dot(hmid, wdb_ref[...]))


def _experts(blk_exp, n_used, xs, wg, wu, wd, layer):
    _, ne, d, hid = wg.shape
    t = EXPERT_ROWS
    rp = d // PAIR
    rows = xs.shape[0] // rp
    nb = rows // t
    clamp = lambda b, nu: jnp.minimum(b, jnp.maximum(nu[0] - 1, 0))
    wspec = lambda shape: pl.BlockSpec((None, None) + shape,
                                       lambda b, be, nu: (layer, be[clamp(b, nu)], 0, 0))
    grid_spec = pltpu.PrefetchScalarGridSpec(
        num_scalar_prefetch=2, grid=(nb,),
        in_specs=[pl.BlockSpec((t * rp, LANES), lambda b, be, nu: (clamp(b, nu), 0)),
                  wspec((d, hid)), wspec((d, hid)), wspec((hid, d))],
        out_specs=pl.BlockSpec((t * rp, LANES), lambda b, be, nu: (clamp(b, nu), 0)),
        scratch_shapes=[pltpu.VMEM((d, hid), BF16), pltpu.VMEM((d, hid), BF16), pltpu.VMEM((hid, d), BF16)])
    vmem = 2 * (3 * _nbytes((d, hid), F32) + _nbytes((t, d // 2), U32) + _nbytes((t, d), F32)) \
        + 3 * _nbytes((d, hid), BF16) + 4 * _nbytes((t, d), F32)
    return pl.pallas_call(
        _expert_kernel,
        out_shape=jax.ShapeDtypeStruct((rows * rp, LANES), U32),
        grid_spec=grid_spec,
        compiler_params=_params(("arbitrary",), vmem),
    )(blk_exp, n_used, xs, wg, wu, wd)


def _combine_kernel(dest_ref, ys_hbm, x_ref, gate_ref, sh_ref, g_ref, b_ref, o_ref,
                    buf_ref, sem_ref, *, tc, alpha):
    i = pl.program_id(0)
    rp = x_ref.shape[1] // PAIR
    half = i % 2

    def row_copy(src_row, dst_row, h):
        return pltpu.make_async_copy(ys_hbm.at[pl.ds(pl.multiple_of(src_row * rp, rp), rp)],
                                     buf_ref.at[pl.ds(pl.multiple_of(dst_row * rp, rp), rp)], sem_ref.at[h])

    def fetch(step, h):
        def per_token(t, carry):
            for k in range(TOP_K):
                src = dest_ref[(step * tc + t) * TOP_K + k]
                row_copy(src, (h * TOP_K + k) * tc + t, h).start(priority=k % 2)
            return carry
        lax.fori_loop(0, tc, per_token, 0)

    @pl.when(i == 0)
    def _():
        fetch(0, 0)

    @pl.when(i + 1 < pl.num_programs(0))
    def _():
        fetch(i + 1, 1 - half)

    x = x_ref[...]
    shared = sh_ref[...]

    def drain(t, carry):
        for k in range(TOP_K):
            row_copy(0, 0, half).wait()
        return carry

    lax.fori_loop(0, tc, drain, 0)

    gate = gate_ref[...]
    base = half * (TOP_K * tc * rp)
    cols = []
    for s in range(rp):
        lo = hi = None
        for k in range(TOP_K):
            w = buf_ref[pl.ds(base + k * tc * rp + s, tc, stride=rp), :]
            gk = gate[:, k:k + 1]
            lo_k = lax.bitcast_convert_type(w << 16, F32) * gk
            hi_k = lax.bitcast_convert_type(w & jnp.uint32(0xFFFF0000), F32) * gk
            lo = lo_k if lo is None else lo + lo_k
            hi = hi_k if hi is None else hi + hi_k
        cols += [lo, hi]
    routed = jnp.concatenate(cols, axis=1)
    o_ref[...] = _layer_norm_rows(alpha * x + (shared + routed), g_ref[...], b_ref[...])


def _combine(dest_tok, ys, x, gate_tok, shared, g, b, alpha):
    n, d = x.shape
    tc = _tile(n, 128)
    steps = n // tc
    rp = d // PAIR
    vec = pl.BlockSpec((1, d), lambda i, dest: (0, 0))
    rows = pl.BlockSpec((tc, d), lambda i, dest: (i, 0))
    buf_rows = 2 * TOP_K * tc * rp
    vmem = _nbytes((buf_rows, LANES), U32) + 2 * 3 * _nbytes((tc, d), F32) + 6 * _nbytes((tc, d), F32)
    grid_spec = pltpu.PrefetchScalarGridSpec(
        num_scalar_prefetch=1, grid=(steps,),
        in_specs=[pl.BlockSpec(memory_space=pl.ANY), rows,
                  pl.BlockSpec((tc, TOP_K), lambda i, dest: (i, 0)), rows, vec, vec],
        out_specs=pl.BlockSpec((tc, d), lambda i, dest: (i, 0)),
        scratch_shapes=[pltpu.VMEM((buf_rows, LANES), U32), pltpu.SemaphoreType.DMA((2,))])
    return pl.pallas_call(
        functools.partial(_combine_kernel, tc=tc, alpha=alpha),
        out_shape=jax.ShapeDtypeStruct((n, d), F32),
        grid_spec=grid_spec,
        compiler_params=_params(("arbitrary",), vmem),
    )(dest_tok, ys, x, gate_tok, shared, g, b)


def _moe(x1, x1p, eid, rnk, gate, cnt, wg, wu, wd, layer, sg, su, sd, g, b, alpha):
    n, d = x1.shape
    ne = cnt.shape[0]
    counts = cnt[:, 0]
    padded = (counts + EXPERT_ROWS - 1) // EXPERT_ROWS * EXPERT_ROWS
    eidx = jnp.arange(ne, dtype=I32)
    pad_end = jnp.sum(jnp.where(eidx[None, :] <= eidx[:, None], padded[None, :], 0), axis=1)
    pad_start = pad_end - padded
    n_blocks = (n * TOP_K + ne * (EXPERT_ROWS - 1)) // EXPERT_ROWS
    rows = n_blocks * EXPERT_ROWS
    start_of = jnp.sum(jnp.where(eid[:, :, None] == eidx, pad_start, 0), axis=-1)
    dest_tok = (start_of + rnk).T.reshape(-1).astype(I32)
    pad_info = jnp.stack([pad_start + counts, padded - counts]).astype(I32)
    total = pad_end[ne - 1]
    n_used = (total // EXPERT_ROWS).astype(I32).reshape(1)
    blk_start = jnp.minimum(jnp.arange(n_blocks, dtype=I32) * EXPERT_ROWS, total - 1)
    blk_exp = jnp.sum((pad_end[None, :] <= blk_start[:, None]).astype(I32), axis=1)
    blk_exp = jnp.minimum(blk_exp, ne - 1).astype(I32)
    xs, shared = _dispatch(pad_info, dest_tok, x1p, n, rows, sg, su, sd)
    ys = _experts(blk_exp, n_used, xs, wg, wu, wd, layer)
    return _combine(dest_tok, ys, x1, gate.T, shared, g, b, alpha)


def kernel(x, mem, w_in, conv_w, w_conv_out, hg_lb_logits, hg_norm_g, w_hg_out, w_mem_k, w_mem_v,
           w_xa_out, w_o, ln1_g, ln1_b, router_w, router_b, exp_wg, exp_wu, exp_wd,
           sh_wg, sh_wu, sh_wd, ln2_g, ln2_b):
    bsz, seq, d = x.shape
    depth = w_in.shape[0]
    n_mem = mem.shape[1]
    cwid = conv_w.shape[2]
    hw = w_hg_out.shape[1]
    xw = w_mem_k.shape[2]
    alpha = float((2 * depth) ** 0.25)
    n = bsz * seq

    lb_all = jnp.cumsum(jax.nn.softmax(hg_lb_logits.astype(F32), axis=0), axis=0)
    lb_all = lb_all - lb_all[0:1]
    o_conv, o_hg, o_xa, o_gate = 0, 3 * cwid, 3 * cwid + 4 * hw, 3 * cwid + 4 * hw + xw

    h = x.reshape(n, d)
    memf = mem.reshape(bsz * n_mem, d)
    for l in range(depth):
        wl = w_in[l]
        w_conv_in = wl[:, o_conv:o_hg].astype(BF16)
        w_hg_in = wl[:, o_hg:o_xa].astype(BF16)
        w_xa_in = wl[:, o_xa:o_gate].astype(BF16)
        w_gates = wl[:, o_gate:].astype(BF16)
        vec = lambda a: a.astype(F32).reshape(1, -1)

        ya = _conv_branch(h, w_conv_in, conv_w[l].astype(F32), seq)
        lb = lb_all[l]
        yb = _hgrn_branch(h, w_hg_in, vec(jnp.log(lb)), vec(jnp.log1p(-lb)), vec(hg_norm_g[l]), bsz, seq)
        w_kv = jnp.concatenate([w_mem_k[l], w_mem_v[l]], axis=1).astype(BF16)
        kv = _matmul(memf, w_kv, BF16, 1024, xw)
        yx = _attn_branch(h, w_xa_in, kv, seq, n_mem)
        merged = _merge(h, ya, yb, yx, w_gates, w_conv_out[l].astype(BF16), w_hg_out[l].astype(BF16),
                        w_xa_out[l].astype(BF16))
        x1, x1p, eid, rnk, gate, cnt = _oproj_ln_route(
            merged, w_o[l].astype(BF16), h, vec(ln1_g[l]), vec(ln1_b[l]),
            router_w[l].astype(F32).T, router_b[l].astype(F32).reshape(-1, 1), alpha)
        h = _moe(x1, x1p, eid, rnk, gate, cnt, exp_wg, exp_wu, exp_wd, l,
                    sh_wg[l].astype(BF16), sh_wu[l].astype(BF16), sh_wd[l].astype(BF16),
                    vec(ln2_g[l]), vec(ln2_b[l]), alpha)
    return h.reshape(bsz, seq, d)
```

```python
import functools

import jax
import jax.numpy as jnp
import numpy as np
from jax import lax
from jax.experimental import pallas as pl
from jax.experimental.pallas import tpu as pltpu

F32 = jnp.float32
BF16 = jnp.bfloat16
I32 = jnp.int32
U32 = jnp.uint32

HG_DK = 128
XA_DH = 128
TOP_K = 8
N_GROUPS = 8
TOPK_GROUPS = 4
ROUTED_SCALE = 2.5
LN_EPS = 1e-5
RMS_EPS = 1e-6
HG_CHUNK = 64
HG_LEVELS = (1, 2, 4, 8, 16, 32)
EXPERT_ROWS = 512

V7X_VMEM_BYTES = 64 * 1024 * 1024
VMEM_CAP = V7X_VMEM_BYTES - 8 * 1024 * 1024


def _params(semantics, vmem_bytes):
    limit = int(min(VMEM_CAP, max(32 * 1024 * 1024, vmem_bytes * 5 // 4 + (4 << 20))))
    return pltpu.CompilerParams(dimension_semantics=semantics, vmem_limit_bytes=limit)


def _nbytes(shape, dtype):
    n = 1
    for s in shape:
        n *= s
    return n * jnp.dtype(dtype).itemsize


def _tile(n, want):
    t = min(n, want)
    while n % t:
        t //= 2
    return t


def _silu(x):
    return x * (1.0 / (1.0 + jnp.exp(-x)))


def _sigmoid(x):
    return 1.0 / (1.0 + jnp.exp(-x))


def _layer_norm_rows(y, g, b):
    mu = jnp.mean(y, axis=-1, keepdims=True)
    yc = y - mu
    var = jnp.mean(yc * yc, axis=-1, keepdims=True)
    return yc * lax.rsqrt(var + LN_EPS) * g + b


LANES = 128
PAIR = 2 * LANES


def _store_packed_rows(p_ref, y):
    rows, d = y.shape
    pieces = d // PAIR
    for s in range(pieces):
        lo = lax.bitcast_convert_type(y[:, s * PAIR:s * PAIR + LANES].astype(BF16).astype(F32), U32)
        hi = lax.bitcast_convert_type(y[:, s * PAIR + LANES:(s + 1) * PAIR].astype(BF16).astype(F32), U32)
        p_ref[pl.ds(s, rows, stride=pieces), :] = (hi & jnp.uint32(0xFFFF0000)) | (lo >> 16)


def _load_packed_rows(p_ref, rows, d):
    pieces = d // PAIR
    cols = []
    for s in range(pieces):
        w = p_ref[pl.ds(s, rows, stride=pieces), :]
        cols.append(lax.bitcast_convert_type(w << 16, F32).astype(BF16))
        cols.append(lax.bitcast_convert_type(w & jnp.uint32(0xFFFF0000), F32).astype(BF16))
    return jnp.concatenate(cols, axis=1)


def _matmul_kernel(x_ref, w_ref, o_ref):
    o_ref[...] = jnp.dot(x_ref[...].astype(BF16), w_ref[...],
                         preferred_element_type=F32).astype(o_ref.dtype)


def _matmul(x, w, out_dtype, tm, tn):
    m, k = x.shape
    n = w.shape[1]
    tm, tn = _tile(m, tm), _tile(n, tn)
    vmem = 2 * (_nbytes((tm, k), x.dtype) + _nbytes((k, tn), w.dtype) + _nbytes((tm, tn), out_dtype)) \
        + _nbytes((tm, k), BF16) + _nbytes((tm, tn), F32)
    return pl.pallas_call(
        _matmul_kernel,
        out_shape=jax.ShapeDtypeStruct((m, n), out_dtype),
        grid=(m // tm, n // tn),
        in_specs=[pl.BlockSpec((tm, k), lambda i, j: (i, 0)),
                  pl.BlockSpec((k, tn), lambda i, j: (0, j))],
        out_specs=pl.BlockSpec((tm, tn), lambda i, j: (i, j)),
        compiler_params=_params(("arbitrary", "arbitrary"), vmem),
    )(x, w)


def _conv_kernel(x_ref, w_ref, cw_ref, o_ref, carry_ref, *, tiles_per_seq):
    i = pl.program_id(0)
    tm, cwid = o_ref.shape

    @pl.when(i % tiles_per_seq == 0)
    def _():
        carry_ref[...] = jnp.zeros_like(carry_ref)

    p = jnp.dot(x_ref[...].astype(BF16), w_ref[...], preferred_element_type=F32)
    u = p[:, :cwid]
    b = p[:, cwid:2 * cwid]
    c = p[:, 2 * cwid:]
    cu = c * u
    prev = carry_ref[...]
    row = lax.broadcasted_iota(I32, (tm, cwid), 0)
    cu1 = jnp.where(row == 0, prev[7:8, :], pltpu.roll(cu, 1, 0))
    cu2 = pltpu.roll(cu, 2, 0)
    cu2 = jnp.where(row == 0, prev[6:7, :], jnp.where(row == 1, prev[7:8, :], cu2))
    cw = cw_ref[...]
    y = b * (cw[0:1, :] * cu2 + cw[1:2, :] * cu1 + cw[2:3, :] * cu)
    o_ref[...] = y.astype(o_ref.dtype)
    carry_ref[...] = cu[tm - 8:, :]


def _conv_branch(x, w_conv_in, conv_w, seq):
    n, d = x.shape
    cwid = conv_w.shape[1]
    assert conv_w.shape[0] == 3
    tm = _tile(seq, 512)
    vmem = 2 * (_nbytes((tm, d), x.dtype) + _nbytes(w_conv_in.shape, BF16) + _nbytes((tm, cwid), BF16)) \
        + 6 * _nbytes((tm, 3 * cwid), F32)
    return pl.pallas_call(
        functools.partial(_conv_kernel, tiles_per_seq=seq // tm),
        out_shape=jax.ShapeDtypeStruct((n, cwid), BF16),
        grid=(n // tm,),
        in_specs=[pl.BlockSpec((tm, d), lambda i: (i, 0)),
                  pl.BlockSpec(w_conv_in.shape, lambda i: (0, 0)),
                  pl.BlockSpec(conv_w.shape, lambda i: (0, 0))],
        out_specs=pl.BlockSpec((tm, cwid), lambda i: (i, 0)),
        scratch_shapes=[pltpu.VMEM((8, cwid), F32)],
        compiler_params=_params(("arbitrary",), vmem),
    )(x, w_conv_in, conv_w)


def _head_sum(x):
    return jnp.sum(x, axis=-1, keepdims=True)


def _hgrn_tables(tt):
    t = np.arange(tt)[:, None]
    j = np.arange(tt)[None, :]
    tri = ((t // HG_CHUNK) == (j // HG_CHUNK)) & (j <= t)
    level = np.zeros((tt, tt), np.int32)
    for i, h in enumerate(HG_LEVELS):
        level[((t // (2 * h)) == (j // (2 * h))) & ((t % (2 * h)) >= h) & ((j % (2 * h)) < h)] = i + 1
    return tri.astype(np.float32), level


def _hgrn_kernel(x_ref, w_ref, tri_ref, lvl_ref, c1_ref, c2_ref, gain_ref, o_ref,
                 st_ref, p_ref, u_ref, qd_ref, kd_ref, vb_ref, dec_ref, *, heads):
    tt, hw = o_ref.shape
    dk = HG_DK
    ch = HG_CHUNK
    nlev = len(HG_LEVELS)

    @pl.when(pl.program_id(1) == 0)
    def _():
        st_ref[...] = jnp.zeros_like(st_ref)

    p_ref[...] = jnp.dot(x_ref[...].astype(BF16), w_ref[...], preferred_element_type=F32)
    q_ref, f_ref, v_ref, g_ref = (p_ref.at[:, pl.ds(j * hw, hw)] for j in range(4))

    f = f_ref[...]
    log_sig = jnp.minimum(f, 0.0) - jnp.log1p(jnp.exp(-jnp.abs(f)))
    a1 = c1_ref[...]
    a2 = c2_ref[...] + log_sig
    log_f = jnp.maximum(a1, a2) + jnp.log1p(jnp.exp(-jnp.abs(a1 - a2)))
    kin = 1.0 - jnp.exp(log_f)

    t_hi = log_f.astype(BF16)
    rem = log_f - t_hi.astype(F32)
    t_mid = rem.astype(BF16)
    t_lo = (rem - t_mid.astype(F32)).astype(BF16)

    tri = tri_ref[...]
    dot = functools.partial(jnp.dot, preferred_element_type=F32)
    bcum = dot(tri, t_hi) + dot(tri, t_mid) + dot(tri, t_lo)

    def group_row(a, size, idx):
        a3 = a.reshape(tt // size, size, hw)
        return jnp.broadcast_to(a3[:, idx:idx + 1, :], a3.shape).reshape(tt, hw)

    blast = group_row(bcum, ch, ch - 1)
    row = lax.broadcasted_iota(I32, (tt, hw), 0)

    def level_decay(h):
        upper = (row % (2 * h)) >= h
        if h == 1:
            return upper, jnp.where(upper, 1.0 - kin, 1.0)
        if h == 2:
            pos = row % 4
            nxt = pltpu.roll(log_f, tt - 1, 0)
            prv = pltpu.roll(log_f, 1, 0)
            z = jnp.where(pos == 0, nxt, jnp.where(pos == 1, 0.0, jnp.where(pos == 2, log_f, log_f + prv)))
            return upper, jnp.exp(z)
        refb = group_row(bcum, 2 * h, h - 1)
        return upper, jnp.exp(jnp.where(upper, bcum - refb, refb - bcum))

    q = q_ref[...]
    qd_ref[...] = (q * jnp.exp(bcum)).astype(BF16)
    kd_ref[...] = (kin * jnp.exp(blast - bcum)).astype(BF16)
    vb_ref[...] = v_ref[...].astype(BF16)
    dec_ref[...] = jnp.exp(blast)

    for i, h in enumerate(HG_LEVELS):
        upper, decay = level_decay(h)
        u_ref[i] = (jnp.where(upper, q, kin) * decay).astype(BF16)

    lvl = lvl_ref[...]
    gain = gain_ref[...]
    nt_dims = (((1,), (1,)), ((), ()))
    for h in range(heads):
        ls = slice(h * dk, (h + 1) * dk)
        scores = jnp.zeros((tt, tt), F32)
        for i in range(nlev):
            u = u_ref[i, :, ls]
            gram = lax.dot_general(u, u, nt_dims, preferred_element_type=F32)
            scores = scores + jnp.where(lvl == i + 1, gram, 0.0)
        acc = jnp.dot(scores.astype(BF16), vb_ref[:, ls], preferred_element_type=F32)
        acc = acc + _head_sum(q[:, ls] * kin[:, ls]) * v_ref[:, ls]
        inter = []
        for c in range(tt // ch):
            rs = slice(c * ch, (c + 1) * ch)
            st = st_ref[h]
            inter.append(lax.dot_general(qd_ref[rs, ls], st.astype(BF16), nt_dims, preferred_element_type=F32))
            upd = lax.dot_general(vb_ref[rs, ls], kd_ref[rs, ls], (((0,), (0,)), ((), ())),
                                  preferred_element_type=F32)
            st_ref[h] = st * dec_ref[c * ch:c * ch + 1, ls] + upd
        acc = acc + jnp.concatenate(inter, axis=0)
        ms = _head_sum(acc * acc) * (1.0 / dk)
        y = acc * lax.rsqrt(ms + RMS_EPS) * gain[:, ls] * _silu(g_ref[:, ls])
        o_ref[:, ls] = y.astype(o_ref.dtype)


def _hgrn_branch(x, w_hg_in, c1, c2, gain, bsz, seq):
    n, d = x.shape
    hw = w_hg_in.shape[1] // 4
    heads = hw // HG_DK
    tt = _tile(seq, 256)
    assert tt % HG_CHUNK == 0
    nt = seq // tt
    nlev = len(HG_LEVELS)
    tri, level = _hgrn_tables(tt)
    tri = jnp.asarray(tri, BF16)
    level = jnp.asarray(level, I32)
    vec = pl.BlockSpec((1, hw), lambda b, t: (0, 0))
    vmem = 2 * (_nbytes((tt, d), x.dtype) + _nbytes(w_hg_in.shape, BF16) + _nbytes((tt, hw), BF16)
                + _nbytes(tri.shape, BF16) + _nbytes(level.shape, I32)) \
        + _nbytes((tt, d), BF16) + 16 * _nbytes((tt, hw), F32) + nlev * _nbytes((tt, hw), BF16) \
        + 4 * _nbytes((tt, tt), F32) + heads * HG_DK * HG_DK * 4
    return pl.pallas_call(
        functools.partial(_hgrn_kernel, heads=heads),
        out_shape=jax.ShapeDtypeStruct((n, hw), BF16),
        grid=(bsz, nt),
        in_specs=[pl.BlockSpec((tt, d), lambda b, t: (b * nt + t, 0)),
                  pl.BlockSpec(w_hg_in.shape, lambda b, t: (0, 0)),
                  pl.BlockSpec(tri.shape, lambda b, t: (0, 0)),
                  pl.BlockSpec(level.shape, lambda b, t: (0, 0)), vec, vec, vec],
        out_specs=pl.BlockSpec((tt, hw), lambda b, t: (b * nt + t, 0)),
        scratch_shapes=[pltpu.VMEM((heads, HG_DK, HG_DK), F32), pltpu.VMEM((tt, 4 * hw), F32),
                        pltpu.VMEM((nlev, tt, hw), BF16),
                        pltpu.VMEM((tt, hw), BF16), pltpu.VMEM((tt, hw), BF16),
                        pltpu.VMEM((tt, hw), BF16), pltpu.VMEM((tt, hw), F32)],
        compiler_params=_params(("arbitrary", "arbitrary"), vmem),
    )(x, w_hg_in, tri, level, c1, c2, gain)


def _attn_kernel(x_ref, wq_ref, k_ref, v_ref, o_ref, *, heads):
    dh = XA_DH
    q = jnp.dot(x_ref[...].astype(BF16), wq_ref[...], preferred_element_type=F32)
    scale = dh ** -0.5
    for h in range(heads):
        ls = slice(h * dh, (h + 1) * dh)
        s = lax.dot_general(q[:, ls].astype(BF16), k_ref[:, ls], (((1,), (1,)), ((), ())),
                            preferred_element_type=F32) * scale
        e = jnp.exp(s - jnp.max(s, axis=-1, keepdims=True))
        p = e / jnp.sum(e, axis=-1, keepdims=True)
        o = jnp.dot(p.astype(BF16), v_ref[:, ls], preferred_element_type=F32)
        o_ref[:, ls] = o.astype(o_ref.dtype)


def _attn_branch(x, wq, kv, seq, n_mem):
    n, d = x.shape
    xw = wq.shape[1]
    heads = xw // XA_DH
    tm = _tile(seq, 512)
    per_seq = seq // tm
    vmem = 2 * (_nbytes((tm, d), x.dtype) + _nbytes(wq.shape, BF16) + 2 * _nbytes((n_mem, xw), BF16)
                + _nbytes((tm, xw), BF16)) + 4 * _nbytes((tm, xw), F32) + 4 * _nbytes((tm, n_mem), F32)
    return pl.pallas_call(
        functools.partial(_attn_kernel, heads=heads),
        out_shape=jax.ShapeDtypeStruct((n, xw), BF16),
        grid=(n // tm,),
        in_specs=[pl.BlockSpec((tm, d), lambda i: (i, 0)),
                  pl.BlockSpec(wq.shape, lambda i: (0, 0)),
                  pl.BlockSpec((n_mem, xw), lambda i: (i // per_seq, 0)),
                  pl.BlockSpec((n_mem, xw), lambda i: (i // per_seq, 1))],
        out_specs=pl.BlockSpec((tm, xw), lambda i: (i, 0)),
        compiler_params=_params(("arbitrary",), vmem),
    )(x, wq, kv, kv)


def _merge_kernel(x_ref, ya_ref, yb_ref, yx_ref, wga_ref, wgb_ref, wgx_ref, wa_ref, wb_ref, wx_ref,
                  o_ref, xb_ref):
    @pl.when(pl.program_id(1) == 0)
    def _():
        xb_ref[...] = x_ref[...].astype(BF16)

    xb = xb_ref[...]
    dot = functools.partial(jnp.dot, preferred_element_type=F32)
    m = _sigmoid(dot(xb, wga_ref[...])) * dot(ya_ref[...], wa_ref[...])
    m = m + _sigmoid(dot(xb, wgb_ref[...])) * dot(yb_ref[...], wb_ref[...])
    m = m + _sigmoid(dot(xb, wgx_ref[...])) * dot(yx_ref[...], wx_ref[...])
    o_ref[...] = m.astype(o_ref.dtype)


def _merge(x, ya, yb, yx, w_gates, wa, wb, wx):
    n, d = x.shape
    tm, tn = _tile(n, 1024), _tile(d, 512)
    nj = d // tn
    row = lambda a: pl.BlockSpec((tm, a.shape[1]), lambda i, j: (i, 0))
    gate = lambda g: pl.BlockSpec((d, tn), lambda i, j, g=g: (0, g * nj + j))
    outw = lambda a: pl.BlockSpec((a.shape[0], tn), lambda i, j: (0, j))
    vmem = 2 * (_nbytes((tm, d), x.dtype) + _nbytes((tm, ya.shape[1] + yb.shape[1] + yx.shape[1]), BF16)
                + 3 * _nbytes((d, tn), BF16) + _nbytes((wa.shape[0] + wb.shape[0] + wx.shape[0], tn), BF16)
                + _nbytes((tm, tn), BF16)) + _nbytes((tm, d), BF16) + 8 * _nbytes((tm, tn), F32)
    return pl.pallas_call(
        _merge_kernel,
        out_shape=jax.ShapeDtypeStruct((n, d), BF16),
        grid=(n // tm, nj),
        in_specs=[row(x), row(ya), row(yb), row(yx), gate(0), gate(1), gate(2),
                  outw(wa), outw(wb), outw(wx)],
        out_specs=pl.BlockSpec((tm, tn), lambda i, j: (i, j)),
        scratch_shapes=[pltpu.VMEM((tm, d), BF16)],
        compiler_params=_params(("arbitrary", "arbitrary"), vmem),
    )(x, ya, yb, yx, w_gates, w_gates, w_gates, wa, wb, wx)


def _oproj_ln_kernel(m_ref, w_ref, x_ref, g_ref, b_ref, rw_ref, rb_ref,
                     o_ref, p_ref, eid_ref, rnk_ref, gate_ref, cnt_ref, carry_ref, *, alpha):
    y = alpha * x_ref[...] + jnp.dot(m_ref[...], w_ref[...], preferred_element_type=F32)
    y = _layer_norm_rows(y, g_ref[...], b_ref[...])
    o_ref[...] = y
    _store_packed_rows(p_ref, y)
    _route(y, rw_ref, rb_ref, eid_ref, rnk_ref, gate_ref, cnt_ref, carry_ref)


def _oproj_ln_route(merged, w_o, x, g, b, rw_t, rb, alpha):
    n, d = x.shape
    assert d % PAIR == 0
    pieces = d // PAIR
    ne = rw_t.shape[0]
    tm = _tile(n, 512)
    cur = lambda i: (i, 0)
    vec = pl.BlockSpec((1, d), lambda i: (0, 0))
    topk = lambda dt: jax.ShapeDtypeStruct((TOP_K, n), dt)
    kspec = pl.BlockSpec((TOP_K, tm), lambda i: (0, i))
    vmem = 2 * (_nbytes((tm, d), BF16) + _nbytes((d, d), BF16) + 2 * _nbytes((tm, d), F32)
                + _nbytes((tm, d // 2), U32) + _nbytes((ne, d), F32)) + 4 * _nbytes((tm, d), F32) \
        + 3 * _nbytes((tm, tm), F32) + 24 * _nbytes((ne, tm), F32)
    return pl.pallas_call(
        functools.partial(_oproj_ln_kernel, alpha=alpha),
        out_shape=(jax.ShapeDtypeStruct((n, d), F32), jax.ShapeDtypeStruct((n * pieces, LANES), U32),
                   topk(I32), topk(I32), topk(F32), jax.ShapeDtypeStruct((ne, 128), I32)),
        grid=(n // tm,),
        in_specs=[pl.BlockSpec((tm, d), cur),
                  pl.BlockSpec((d, d), lambda i: (0, 0)),
                  pl.BlockSpec((tm, d), cur), vec, vec,
                  pl.BlockSpec((ne, d), lambda i: (0, 0)),
                  pl.BlockSpec((ne, 1), lambda i: (0, 0))],
        out_specs=(pl.BlockSpec((tm, d), cur),
                   pl.BlockSpec((tm * pieces, LANES), cur),
                   kspec, kspec, kspec, pl.BlockSpec((ne, 128), lambda i: (0, 0))),
        scratch_shapes=[pltpu.VMEM((ne, 1), F32)],
        compiler_params=_params(("arbitrary",), vmem),
    )(merged, w_o, x, g, b, rw_t, rb)


def _rank_desc(vals):
    rows = vals.shape[0]
    iota = lax.broadcasted_iota(I32, vals.shape, 0)
    rank = jnp.zeros(vals.shape, I32)
    for j in range(rows):
        rowv = vals[j:j + 1, :]
        tie = jnp.where(iota > j, 1, 0)
        rank = rank + jnp.where(rowv > vals, 1, jnp.where(rowv == vals, tie, 0))
    return rank


def _route(x, rw_ref, rb_ref, eid_ref, rnk_ref, gate_ref, cnt_ref, carry_ref):
    ne = rw_ref.shape[0]
    tm = x.shape[0]
    gsz = ne // N_GROUPS

    @pl.when(pl.program_id(0) == 0)
    def _():
        carry_ref[...] = jnp.zeros_like(carry_ref)

    logits = lax.dot_general(rw_ref[...], x, (((1,), (1,)), ((), ())),
                             preferred_element_type=F32, precision=lax.Precision.HIGHEST)
    s = _sigmoid(logits)
    sel = s + rb_ref[...]

    grp = sel.reshape(N_GROUPS, gsz, tm)
    sub_iota = lax.broadcasted_iota(I32, (N_GROUPS, gsz, tm), 1)
    m1 = jnp.max(grp, axis=1, keepdims=True)
    first = jnp.min(jnp.where(grp == m1, sub_iota, gsz), axis=1, keepdims=True)
    m2 = jnp.max(jnp.where(sub_iota == first, -jnp.inf, grp), axis=1, keepdims=True)
    gscore = (m1 + m2).reshape(N_GROUPS, tm)

    gsel = _rank_desc(gscore) < TOPK_GROUPS
    emask = jnp.broadcast_to(gsel.reshape(N_GROUPS, 1, tm), (N_GROUPS, gsz, tm)).reshape(ne, tm)
    masked = jnp.where(emask, sel, -jnp.inf)

    eiota = lax.broadcasted_iota(I32, (ne, tm), 0)
    chosen = (_rank_desc(masked) < TOP_K) & emask

    w = jnp.where(chosen, s, 0.0)
    gate = w / jnp.sum(w, axis=0, keepdims=True) * ROUTED_SCALE

    ch = jnp.where(chosen, 1.0, 0.0).astype(BF16)
    tr = lax.broadcasted_iota(I32, (tm, tm), 0)
    tc = lax.broadcasted_iota(I32, (tm, tm), 1)
    before = jnp.where(tr < tc, 1.0, 0.0).astype(BF16)
    rank_tok = (carry_ref[...] + jnp.dot(ch, before, preferred_element_type=F32)).astype(I32)
    carry_ref[...] = carry_ref[...] + jnp.sum(jnp.where(chosen, 1.0, 0.0), axis=1, keepdims=True)
    cnt_ref[...] = jnp.broadcast_to(carry_ref[...], cnt_ref.shape).astype(I32)

    er = lax.broadcasted_iota(I32, (ne, ne), 0)
    ec = lax.broadcasted_iota(I32, (ne, ne), 1)
    lower = jnp.where(ec < er, 1.0, 0.0).astype(BF16)
    slot = jnp.dot(lower, ch, preferred_element_type=F32).astype(I32)
    for k in range(TOP_K):
        pick = chosen & (slot == k)
        eid_ref[k:k + 1, :] = jnp.sum(jnp.where(pick, eiota, 0), axis=0, keepdims=True)
        rnk_ref[k:k + 1, :] = jnp.sum(jnp.where(pick, rank_tok, 0), axis=0, keepdims=True)
        gate_ref[k:k + 1, :] = jnp.sum(jnp.where(pick, gate, 0.0), axis=0, keepdims=True)


def _pad_pieces(max_pad):
    pieces = []
    p = 1
    while p <= max_pad:
        pieces.append(p)
        p *= 2
    return pieces[::-1]


def _dispatch_kernel(pad_ref, dest_hbm, xp_ref, zero_hbm, sg_ref, su_ref, sd_ref, xs_hbm, sh_ref,
                     idx_ref, sem_idx, sem_row, sem_pad, *, tq, ne, rp):
    i = pl.program_id(0)

    def row_copy(src_row, dst_row):
        return pltpu.make_async_copy(xp_ref.at[pl.ds(pl.multiple_of(src_row * rp, rp), rp)],
                                     xs_hbm.at[pl.ds(pl.multiple_of(dst_row * rp, rp), rp)], sem_row)

    def pad_copies(fn):
        def per_expert(e, carry):
            row = pad_ref[0, e]
            npad = pad_ref[1, e]
            for p in _pad_pieces(EXPERT_ROWS - 1):
                @pl.when((npad & p) != 0)
                def _(row=row, p=p):
                    fn(pltpu.make_async_copy(zero_hbm.at[pl.ds(0, p * rp)],
                                             xs_hbm.at[pl.ds(pl.multiple_of(row * rp, rp), p * rp)], sem_pad))
                row = row + (npad & p)
            return carry
        lax.fori_loop(0, ne, per_expert, 0)

    @pl.when(i == 0)
    def _():
        pad_copies(lambda cp: cp.start())

    idx_cp = pltpu.make_async_copy(dest_hbm.at[i], idx_ref, sem_idx)
    idx_cp.start()
    idx_cp.wait()

    def per_token(t, carry):
        for k in range(TOP_K):
            row_copy(t, idx_ref[t * TOP_K + k]).start(priority=k % 2)
        return carry

    lax.fori_loop(0, tq, per_token, 0)

    xb = _load_packed_rows(xp_ref, tq, sh_ref.shape[1])
    dot = functools.partial(jnp.dot, preferred_element_type=F32)
    hmid = (_silu(dot(xb, sg_ref[...])) * dot(xb, su_ref[...])).astype(BF16)
    sh_ref[...] = dot(hmid, sd_ref[...])

    def drain(t, carry):
        for k in range(TOP_K):
            row_copy(0, 0).wait()
        return carry

    lax.fori_loop(0, tq, drain, 0)

    @pl.when(i == 0)
    def _():
        pad_copies(lambda cp: cp.wait())


def _dispatch(pad_info, dest_tok, xp, n, rows, sg, su, sd):
    rp = xp.shape[0] // n
    d, hid = sg.shape
    ne = pad_info.shape[1]
    tq = _tile(n, 512)
    steps = n // tq
    dest2 = dest_tok.reshape(steps, tq * TOP_K)
    zero = jnp.zeros((EXPERT_ROWS * rp, LANES), U32)
    any_spec = pl.BlockSpec(memory_space=pl.ANY)
    full = lambda a: pl.BlockSpec(a.shape, lambda i, pad: (0, 0))
    grid_spec = pltpu.PrefetchScalarGridSpec(
        num_scalar_prefetch=1, grid=(steps,),
        in_specs=[any_spec, pl.BlockSpec((tq * rp, LANES), lambda i, pad: (i, 0)), any_spec,
                  full(sg), full(su), full(sd)],
        out_specs=(any_spec, pl.BlockSpec((tq, d), lambda i, pad: (i, 0))),
        scratch_shapes=[pltpu.SMEM((tq * TOP_K,), I32), pltpu.SemaphoreType.DMA(()),
                        pltpu.SemaphoreType.DMA(()), pltpu.SemaphoreType.DMA(())])
    vmem = 2 * (_nbytes((tq * rp, LANES), U32) + 3 * _nbytes((d, hid), BF16) + _nbytes((tq, d), F32)) \
        + _nbytes((tq, d), BF16) + 3 * _nbytes((tq, hid), F32) + _nbytes((tq, d), F32)
    return pl.pallas_call(
        functools.partial(_dispatch_kernel, tq=tq, ne=ne, rp=rp),
        out_shape=(jax.ShapeDtypeStruct((rows * rp, LANES), U32), jax.ShapeDtypeStruct((n, d), F32)),
        grid_spec=grid_spec,
        compiler_params=_params(("arbitrary",), vmem),
    )(pad_info, dest2, xp, zero, sg, su, sd)


def _expert_kernel(plan_ref, nu_ref, xs_ref, wg_hbm, wu_hbm, wd_hbm, ys_ref,
                   sg_ref, su_ref, sd_ref, wgb_ref, wub_ref, wdb_ref, sem_ref, *, layer):
    b = pl.program_id(0)

    def fetch(e, slot):
        return (pltpu.make_async_copy(wg_hbm.at[layer, e], sg_ref.at[slot], sem_ref.at[slot]),
                pltpu.make_async_copy(wu_hbm.at[layer, e], su_ref.at[slot], sem_ref.at[slot]),
                pltpu.make_async_copy(wd_hbm.at[layer, e], sd_ref.at[slot], sem_ref.at[slot]))

    @pl.when(b < nu_ref[0])
    def _():
        e = plan_ref[0, b]
        nxt = plan_ref[2, b]
        slot = plan_ref[3, b]

        @pl.when(b == 0)
        def _():
            for cp in fetch(e, slot):
                cp.start()

        @pl.when(plan_ref[1, b] == 1)
        def _():
            @pl.when(nxt >= 0)
            def _():
                for cp in fetch(nxt, 1 - slot):
                    cp.start()

            for cp in fetch(e, slot):
                cp.wait()
            wgb_ref[...] = sg_ref[slot].astype(BF16)
            wub_ref[...] = su_ref[slot].astype(BF16)
            wdb_ref[...] = sd_ref[slot].astype(BF16)

        t = EXPERT_ROWS
        d = wgb_ref.shape[0]
        xb = _load_packed_rows(xs_ref, t, d)
        dot = functools.partial(jnp.dot, preferred_element_type=F32)
        hmid = (_silu(dot(xb, wgb_ref[...])) * dot(xb, wub_ref[...])).astype(BF16)
        _store_packed_rows(ys_ref, dot(hmid, wdb_ref[...]))


def _experts(plan, n_used, xs, wg, wu, wd, layer):
    _, ne, d, hid = wg.shape
    t = EXPERT_ROWS
    rp = d // PAIR
    rows = xs.shape[0] // rp
    nb = rows // t
    clamp = lambda b, plan, nu: (jnp.minimum(b, jnp.maximum(nu[0] - 1, 0)), 0)
    any_spec = pl.BlockSpec(memory_space=pl.ANY)
    grid_spec = pltpu.PrefetchScalarGridSpec(
        num_scalar_prefetch=2, grid=(nb,),
        in_specs=[pl.BlockSpec((t * rp, LANES), clamp), any_spec, any_spec, any_spec],
        out_specs=pl.BlockSpec((t * rp, LANES), clamp),
        scratch_shapes=[pltpu.VMEM((2, d, hid), F32), pltpu.VMEM((2, d, hid), F32), pltpu.VMEM((2, hid, d), F32),
                        pltpu.VMEM((d, hid), BF16), pltpu.VMEM((d, hid), BF16), pltpu.VMEM((hid, d), BF16),
                        pltpu.SemaphoreType.DMA((2,))])
    vmem = 2 * (3 * _nbytes((d, hid), F32) + 2 * _nbytes((t, d // 2), U32)) \
        + 3 * _nbytes((d, hid), BF16) + 3 * _nbytes((t, d), F32)
    return pl.pallas_call(
        functools.partial(_expert_kernel, layer=layer),
        out_shape=jax.ShapeDtypeStruct((rows * rp, LANES), U32),
        grid_spec=grid_spec,
        compiler_params=_params(("arbitrary",), vmem),
    )(plan, n_used, xs, wg, wu, wd)


def _combine_kernel(dest_ref, ys_hbm, x_ref, gate_ref, sh_ref, g_ref, b_ref, o_ref,
                    buf_ref, sem_ref, *, tc, alpha):
    i = pl.program_id(0)
    rp = x_ref.shape[1] // PAIR
    half = i % 2

    def row_copy(src_row, dst_row, h):
        return pltpu.make_async_copy(ys_hbm.at[pl.ds(pl.multiple_of(src_row * rp, rp), rp)],
                                     buf_ref.at[pl.ds(pl.multiple_of(dst_row * rp, rp), rp)], sem_ref.at[h])

    def fetch(step, h):
        def per_token(t, carry):
            for k in range(TOP_K):
                src = dest_ref[(step * tc + t) * TOP_K + k]
                row_copy(src, (h * TOP_K + k) * tc + t, h).start(priority=k % 2)
            return carry
        lax.fori_loop(0, tc, per_token, 0)

    @pl.when(i == 0)
    def _():
        fetch(0, 0)

    @pl.when(i + 1 < pl.num_programs(0))
    def _():
        fetch(i + 1, 1 - half)

    x = x_ref[...]
    shared = sh_ref[...]

    def drain(t, carry):
        for k in range(TOP_K):
            row_copy(0, 0, half).wait()
        return carry

    lax.fori_loop(0, tc, drain, 0)

    gate = gate_ref[...]
    base = half * (TOP_K * tc * rp)
    cols = []
    for s in range(rp):
        lo = hi = None
        for k in range(TOP_K):
            w = buf_ref[pl.ds(base + k * tc * rp + s, tc, stride=rp), :]
            gk = gate[:, k:k + 1]
            lo_k = lax.bitcast_convert_type(w << 16, F32) * gk
            hi_k = lax.bitcast_convert_type(w & jnp.uint32(0xFFFF0000), F32) * gk
            lo = lo_k if lo is None else lo + lo_k
            hi = hi_k if hi is None else hi + hi_k
        cols += [lo, hi]
    routed = jnp.concatenate(cols, axis=1)
    o_ref[...] = _layer_norm_rows(alpha * x + (shared + routed), g_ref[...], b_ref[...])


def _combine(dest_tok, ys, x, gate_tok, shared, g, b, alpha):
    n, d = x.shape
    tc = _tile(n, 128)
    steps = n // tc
    rp = d // PAIR
    vec = pl.BlockSpec((1, d), lambda i, dest: (0, 0))
    rows = pl.BlockSpec((tc, d), lambda i, dest: (i, 0))
    buf_rows = 2 * TOP_K * tc * rp
    vmem = _nbytes((buf_rows, LANES), U32) + 2 * 3 * _nbytes((tc, d), F32) + 6 * _nbytes((tc, d), F32)
    grid_spec = pltpu.PrefetchScalarGridSpec(
        num_scalar_prefetch=1, grid=(steps,),
        in_specs=[pl.BlockSpec(memory_space=pl.ANY), rows,
                  pl.BlockSpec((tc, TOP_K), lambda i, dest: (i, 0)), rows, vec, vec],
        out_specs=pl.BlockSpec((tc, d), lambda i, dest: (i, 0)),
        scratch_shapes=[pltpu.VMEM((buf_rows, LANES), U32), pltpu.SemaphoreType.DMA((2,))])
    return pl.pallas_call(
        functools.partial(_combine_kernel, tc=tc, alpha=alpha),
        out_shape=jax.ShapeDtypeStruct((n, d), F32),
        grid_spec=grid_spec,
        compiler_params=_params(("arbitrary",), vmem),
    )(dest_tok, ys, x, gate_tok, shared, g, b)


def _moe(x1, x1p, eid, rnk, gate, cnt, wg, wu, wd, layer, sg, su, sd, g, b, alpha):
    n, d = x1.shape
    ne = cnt.shape[0]
    counts = cnt[:, 0]
    padded = (counts + EXPERT_ROWS - 1) // EXPERT_ROWS * EXPERT_ROWS
    eidx = jnp.arange(ne, dtype=I32)
    pad_end = jnp.sum(jnp.where(eidx[None, :] <= eidx[:, None], padded[None, :], 0), axis=1)
    pad_start = pad_end - padded
    n_blocks = (n * TOP_K + ne * (EXPERT_ROWS - 1)) // EXPERT_ROWS
    rows = n_blocks * EXPERT_ROWS
    start_of = jnp.sum(jnp.where(eid[:, :, None] == eidx, pad_start, 0), axis=-1)
    dest_tok = (start_of + rnk).T.reshape(-1).astype(I32)
    pad_info = jnp.stack([pad_start + counts, padded - counts]).astype(I32)
    total = pad_end[ne - 1]
    n_used = (total // EXPERT_ROWS).astype(I32).reshape(1)
    blk_start = jnp.minimum(jnp.arange(n_blocks, dtype=I32) * EXPERT_ROWS, total - 1)
    blk_exp = jnp.sum((pad_end[None, :] <= blk_start[:, None]).astype(I32), axis=1)
    blk_exp = jnp.minimum(blk_exp, ne - 1).astype(I32)
    has = counts > 0
    later = jnp.where((eidx[None, :] > eidx[:, None]) & has[None, :], eidx[None, :], ne)
    next_e = jnp.min(later, axis=1)
    next_e = jnp.where(next_e == ne, -1, next_e)
    ordinal = jnp.sum(jnp.where((eidx[None, :] <= eidx[:, None]) & has[None, :], 1, 0), axis=1) - 1
    first = jnp.concatenate([jnp.ones((1,), I32), (blk_exp[1:] != blk_exp[:-1]).astype(I32)])
    plan = jnp.stack([blk_exp, first, next_e[blk_exp], ordinal[blk_exp] % 2]).astype(I32)
    xs, shared = _dispatch(pad_info, dest_tok, x1p, n, rows, sg, su, sd)
    ys = _experts(plan, n_used, xs, wg, wu, wd, layer)
    return _combine(dest_tok, ys, x1, gate.T, shared, g, b, alpha)


def kernel(x, mem, w_in, conv_w, w_conv_out, hg_lb_logits, hg_norm_g, w_hg_out, w_mem_k, w_mem_v,
           w_xa_out, w_o, ln1_g, ln1_b, router_w, router_b, exp_wg, exp_wu, exp_wd,
           sh_wg, sh_wu, sh_wd, ln2_g, ln2_b):
    bsz, seq, d = x.shape
    depth = w_in.shape[0]
    n_mem = mem.shape[1]
    cwid = conv_w.shape[2]
    hw = w_hg_out.shape[1]
    xw = w_mem_k.shape[2]
    alpha = float((2 * depth) ** 0.25)
    n = bsz * seq

    lb_all = jnp.cumsum(jax.nn.softmax(hg_lb_logits.astype(F32), axis=0), axis=0)
    lb_all = lb_all - lb_all[0:1]
    o_conv, o_hg, o_xa, o_gate = 0, 3 * cwid, 3 * cwid + 4 * hw, 3 * cwid + 4 * hw + xw

    h = x.reshape(n, d)
    memf = mem.reshape(bsz * n_mem, d)
    for l in range(depth):
        wl = w_in[l]
        w_conv_in = wl[:, o_conv:o_hg].astype(BF16)
        w_hg_in = wl[:, o_hg:o_xa].astype(BF16)
        w_xa_in = wl[:, o_xa:o_gate].astype(BF16)
        w_gates = wl[:, o_gate:].astype(BF16)
        vec = lambda a: a.astype(F32).reshape(1, -1)

        ya = _conv_branch(h, w_conv_in, conv_w[l].astype(F32), seq)
        lb = lb_all[l]
        yb = _hgrn_branch(h, w_hg_in, vec(jnp.log(lb)), vec(jnp.log1p(-lb)), vec(hg_norm_g[l]), bsz, seq)
        w_kv = jnp.concatenate([w_mem_k[l], w_mem_v[l]], axis=1).astype(BF16)
        kv = _matmul(memf, w_kv, BF16, 1024, xw)
        yx = _attn_branch(h, w_xa_in, kv, seq, n_mem)
        merged = _merge(h, ya, yb, yx, w_gates, w_conv_out[l].astype(BF16), w_hg_out[l].astype(BF16),
                        w_xa_out[l].astype(BF16))
        x1, x1p, eid, rnk, gate, cnt = _oproj_ln_route(
            merged, w_o[l].astype(BF16), h, vec(ln1_g[l]), vec(ln1_b[l]),
            router_w[l].astype(F32).T, router_b[l].astype(F32).reshape(-1, 1), alpha)
        h = _moe(x1, x1p, eid, rnk, gate, cnt, exp_wg, exp_wu, exp_wd, l,
                    sh_wg[l].astype(BF16), sh_wu[l].astype(BF16), sh_wd[l].astype(BF16),
                    vec(ln2_g[l]), vec(ln2_b[l]), alpha)
    return h.reshape(bsz, seq, d)
```

```python
import functools

import jax
import jax.numpy as jnp
import numpy as np
from jax import lax
from jax.experimental import pallas as pl
from jax.experimental.pallas import tpu as pltpu

F32 = jnp.float32
BF16 = jnp.bfloat16
I32 = jnp.int32
U32 = jnp.uint32

HG_DK = 128
XA_DH = 128
TOP_K = 8
N_GROUPS = 8
TOPK_GROUPS = 4
ROUTED_SCALE = 2.5
LN_EPS = 1e-5
RMS_EPS = 1e-6
HG_CHUNK = 64
HG_LEVELS = (1, 2, 4, 8, 16, 32)
EXPERT_ROWS = 512
COMBINE_UNROLL = 8

V7X_VMEM_BYTES = 64 * 1024 * 1024
VMEM_CAP = V7X_VMEM_BYTES - 8 * 1024 * 1024


def _params(semantics, vmem_bytes):
    limit = int(min(VMEM_CAP, max(32 * 1024 * 1024, vmem_bytes * 5 // 4 + (4 << 20))))
    return pltpu.CompilerParams(dimension_semantics=semantics, vmem_limit_bytes=limit)


def _nbytes(shape, dtype):
    n = 1
    for s in shape:
        n *= s
    return n * jnp.dtype(dtype).itemsize


def _tile(n, want):
    t = min(n, want)
    while n % t:
        t //= 2
    return t


def _silu(x):
    return x * (1.0 / (1.0 + jnp.exp(-x)))


def _sigmoid(x):
    return 1.0 / (1.0 + jnp.exp(-x))


def _layer_norm_rows(y, g, b):
    mu = jnp.mean(y, axis=-1, keepdims=True)
    yc = y - mu
    var = jnp.mean(yc * yc, axis=-1, keepdims=True)
    return yc * lax.rsqrt(var + LN_EPS) * g + b


LANES = 128
PAIR = 2 * LANES


def _store_packed_rows(p_ref, y):
    rows, d = y.shape
    pieces = d // PAIR
    for s in range(pieces):
        lo = lax.bitcast_convert_type(y[:, s * PAIR:s * PAIR + LANES].astype(BF16).astype(F32), U32)
        hi = lax.bitcast_convert_type(y[:, s * PAIR + LANES:(s + 1) * PAIR].astype(BF16).astype(F32), U32)
        p_ref[pl.ds(s, rows, stride=pieces), :] = (hi & jnp.uint32(0xFFFF0000)) | (lo >> 16)


def _load_packed_rows(p_ref, rows, d):
    pieces = d // PAIR
    cols = []
    for s in range(pieces):
        w = p_ref[pl.ds(s, rows, stride=pieces), :]
        cols.append(lax.bitcast_convert_type(w << 16, F32).astype(BF16))
        cols.append(lax.bitcast_convert_type(w & jnp.uint32(0xFFFF0000), F32).astype(BF16))
    return jnp.concatenate(cols, axis=1)


def _matmul_kernel(x_ref, w_ref, o_ref):
    o_ref[...] = jnp.dot(x_ref[...].astype(BF16), w_ref[...],
                         preferred_element_type=F32).astype(o_ref.dtype)


def _matmul(x, w, out_dtype, tm, tn):
    m, k = x.shape
    n = w.shape[1]
    tm, tn = _tile(m, tm), _tile(n, tn)
    vmem = 2 * (_nbytes((tm, k), x.dtype) + _nbytes((k, tn), w.dtype) + _nbytes((tm, tn), out_dtype)) \
        + _nbytes((tm, k), BF16) + _nbytes((tm, tn), F32)
    return pl.pallas_call(
        _matmul_kernel,
        out_shape=jax.ShapeDtypeStruct((m, n), out_dtype),
        grid=(m // tm, n // tn),
        in_specs=[pl.BlockSpec((tm, k), lambda i, j: (i, 0)),
                  pl.BlockSpec((k, tn), lambda i, j: (0, j))],
        out_specs=pl.BlockSpec((tm, tn), lambda i, j: (i, j)),
        compiler_params=_params(("arbitrary", "arbitrary"), vmem),
    )(x, w)


def _conv_kernel(x_ref, w_ref, cw_ref, o_ref, carry_ref, *, tiles_per_seq):
    i = pl.program_id(0)
    tm, cwid = o_ref.shape

    @pl.when(i % tiles_per_seq == 0)
    def _():
        carry_ref[...] = jnp.zeros_like(carry_ref)

    p = jnp.dot(x_ref[...].astype(BF16), w_ref[...], preferred_element_type=F32)
    u = p[:, :cwid]
    b = p[:, cwid:2 * cwid]
    c = p[:, 2 * cwid:]
    cu = c * u
    prev = carry_ref[...]
    row = lax.broadcasted_iota(I32, (tm, cwid), 0)
    cu1 = jnp.where(row == 0, prev[7:8, :], pltpu.roll(cu, 1, 0))
    cu2 = pltpu.roll(cu, 2, 0)
    cu2 = jnp.where(row == 0, prev[6:7, :], jnp.where(row == 1, prev[7:8, :], cu2))
    cw = cw_ref[...]
    y = b * (cw[0:1, :] * cu2 + cw[1:2, :] * cu1 + cw[2:3, :] * cu)
    o_ref[...] = y.astype(o_ref.dtype)
    carry_ref[...] = cu[tm - 8:, :]


def _conv_branch(x, w_conv_in, conv_w, seq):
    n, d = x.shape
    cwid = conv_w.shape[1]
    assert conv_w.shape[0] == 3
    tm = _tile(seq, 512)
    vmem = 2 * (_nbytes((tm, d), x.dtype) + _nbytes(w_conv_in.shape, BF16) + _nbytes((tm, cwid), BF16)) \
        + 6 * _nbytes((tm, 3 * cwid), F32)
    return pl.pallas_call(
        functools.partial(_conv_kernel, tiles_per_seq=seq // tm),
        out_shape=jax.ShapeDtypeStruct((n, cwid), BF16),
        grid=(n // tm,),
        in_specs=[pl.BlockSpec((tm, d), lambda i: (i, 0)),
                  pl.BlockSpec(w_conv_in.shape, lambda i: (0, 0)),
                  pl.BlockSpec(conv_w.shape, lambda i: (0, 0))],
        out_specs=pl.BlockSpec((tm, cwid), lambda i: (i, 0)),
        scratch_shapes=[pltpu.VMEM((8, cwid), F32)],
        compiler_params=_params(("arbitrary",), vmem),
    )(x, w_conv_in, conv_w)


def _head_sum(x):
    return jnp.sum(x, axis=-1, keepdims=True)


def _hgrn_tables(tt):
    t = np.arange(tt)[:, None]
    j = np.arange(tt)[None, :]
    tri = ((t // HG_CHUNK) == (j // HG_CHUNK)) & (j <= t)
    level = np.zeros((tt, tt), np.int32)
    for i, h in enumerate(HG_LEVELS):
        level[((t // (2 * h)) == (j // (2 * h))) & ((t % (2 * h)) >= h) & ((j % (2 * h)) < h)] = i + 1
    return tri.astype(np.float32), level


def _hgrn_kernel(x_ref, w_ref, tri_ref, lvl_ref, c1_ref, c2_ref, gain_ref, o_ref,
                 st_ref, p_ref, u_ref, qd_ref, kd_ref, vb_ref, dec_ref, *, heads):
    tt, hw = o_ref.shape
    dk = HG_DK
    ch = HG_CHUNK
    nlev = len(HG_LEVELS)

    @pl.when(pl.program_id(1) == 0)
    def _():
        st_ref[...] = jnp.zeros_like(st_ref)

    p_ref[...] = jnp.dot(x_ref[...].astype(BF16), w_ref[...], preferred_element_type=F32)
    q_ref, f_ref, v_ref, g_ref = (p_ref.at[:, pl.ds(j * hw, hw)] for j in range(4))

    f = f_ref[...]
    log_sig = jnp.minimum(f, 0.0) - jnp.log1p(jnp.exp(-jnp.abs(f)))
    a1 = c1_ref[...]
    a2 = c2_ref[...] + log_sig
    log_f = jnp.maximum(a1, a2) + jnp.log1p(jnp.exp(-jnp.abs(a1 - a2)))
    kin = 1.0 - jnp.exp(log_f)

    t_hi = log_f.astype(BF16)
    rem = log_f - t_hi.astype(F32)
    t_mid = rem.astype(BF16)
    t_lo = (rem - t_mid.astype(F32)).astype(BF16)

    tri = tri_ref[...]
    dot = functools.partial(jnp.dot, preferred_element_type=F32)
    bcum = dot(tri, t_hi) + dot(tri, t_mid) + dot(tri, t_lo)

    def group_row(a, size, idx):
        a3 = a.reshape(tt // size, size, hw)
        return jnp.broadcast_to(a3[:, idx:idx + 1, :], a3.shape).reshape(tt, hw)

    blast = group_row(bcum, ch, ch - 1)
    row = lax.broadcasted_iota(I32, (tt, hw), 0)

    def level_decay(h):
        upper = (row % (2 * h)) >= h
        if h == 1:
            return upper, jnp.where(upper, 1.0 - kin, 1.0)
        if h == 2:
            pos = row % 4
            nxt = pltpu.roll(log_f, tt - 1, 0)
            prv = pltpu.roll(log_f, 1, 0)
            z = jnp.where(pos == 0, nxt, jnp.where(pos == 1, 0.0, jnp.where(pos == 2, log_f, log_f + prv)))
            return upper, jnp.exp(z)
        refb = group_row(bcum, 2 * h, h - 1)
        return upper, jnp.exp(jnp.where(upper, bcum - refb, refb - bcum))

    q = q_ref[...]
    qd_ref[...] = (q * jnp.exp(bcum)).astype(BF16)
    kd_ref[...] = (kin * jnp.exp(blast - bcum)).astype(BF16)
    vb_ref[...] = v_ref[...].astype(BF16)
    dec_ref[...] = jnp.exp(blast)

    for i, h in enumerate(HG_LEVELS):
        upper, decay = level_decay(h)
        u_ref[i] = (jnp.where(upper, q, kin) * decay).astype(BF16)

    lvl = lvl_ref[...]
    gain = gain_ref[...]
    nt_dims = (((1,), (1,)), ((), ()))
    for h in range(heads):
        ls = slice(h * dk, (h + 1) * dk)
        scores = jnp.zeros((tt, tt), F32)
        for i in range(nlev):
            u = u_ref[i, :, ls]
            gram = lax.dot_general(u, u, nt_dims, preferred_element_type=F32)
            scores = scores + jnp.where(lvl == i + 1, gram, 0.0)
        acc = jnp.dot(scores.astype(BF16), vb_ref[:, ls], preferred_element_type=F32)
        acc = acc + _head_sum(q[:, ls] * kin[:, ls]) * v_ref[:, ls]
        inter = []
        for c in range(tt // ch):
            rs = slice(c * ch, (c + 1) * ch)
            st = st_ref[h]
            inter.append(lax.dot_general(qd_ref[rs, ls], st.astype(BF16), nt_dims, preferred_element_type=F32))
            upd = lax.dot_general(vb_ref[rs, ls], kd_ref[rs, ls], (((0,), (0,)), ((), ())),
                                  preferred_element_type=F32)
            st_ref[h] = st * dec_ref[c * ch:c * ch + 1, ls] + upd
        acc = acc + jnp.concatenate(inter, axis=0)
        ms = _head_sum(acc * acc) * (1.0 / dk)
        y = acc * lax.rsqrt(ms + RMS_EPS) * gain[:, ls] * _silu(g_ref[:, ls])
        o_ref[:, ls] = y.astype(o_ref.dtype)


def _hgrn_branch(x, w_hg_in, c1, c2, gain, bsz, seq):
    n, d = x.shape
    hw = w_hg_in.shape[1] // 4
    heads = hw // HG_DK
    tt = _tile(seq, 256)
    assert tt % HG_CHUNK == 0
    nt = seq // tt
    nlev = len(HG_LEVELS)
    tri, level = _hgrn_tables(tt)
    tri = jnp.asarray(tri, BF16)
    level = jnp.asarray(level, I32)
    vec = pl.BlockSpec((1, hw), lambda b, t: (0, 0))
    vmem = 2 * (_nbytes((tt, d), x.dtype) + _nbytes(w_hg_in.shape, BF16) + _nbytes((tt, hw), BF16)
                + _nbytes(tri.shape, BF16) + _nbytes(level.shape, I32)) \
        + _nbytes((tt, d), BF16) + 16 * _nbytes((tt, hw), F32) + nlev * _nbytes((tt, hw), BF16) \
        + 4 * _nbytes((tt, tt), F32) + heads * HG_DK * HG_DK * 4
    return pl.pallas_call(
        functools.partial(_hgrn_kernel, heads=heads),
        out_shape=jax.ShapeDtypeStruct((n, hw), BF16),
        grid=(bsz, nt),
        in_specs=[pl.BlockSpec((tt, d), lambda b, t: (b * nt + t, 0)),
                  pl.BlockSpec(w_hg_in.shape, lambda b, t: (0, 0)),
                  pl.BlockSpec(tri.shape, lambda b, t: (0, 0)),
                  pl.BlockSpec(level.shape, lambda b, t: (0, 0)), vec, vec, vec],
        out_specs=pl.BlockSpec((tt, hw), lambda b, t: (b * nt + t, 0)),
        scratch_shapes=[pltpu.VMEM((heads, HG_DK, HG_DK), F32), pltpu.VMEM((tt, 4 * hw), F32),
                        pltpu.VMEM((nlev, tt, hw), BF16),
                        pltpu.VMEM((tt, hw), BF16), pltpu.VMEM((tt, hw), BF16),
                        pltpu.VMEM((tt, hw), BF16), pltpu.VMEM((tt, hw), F32)],
        compiler_params=_params(("arbitrary", "arbitrary"), vmem),
    )(x, w_hg_in, tri, level, c1, c2, gain)


def _attn_kernel(x_ref, wq_ref, k_ref, v_ref, o_ref, *, heads):
    dh = XA_DH
    q = jnp.dot(x_ref[...].astype(BF16), wq_ref[...], preferred_element_type=F32)
    scale = dh ** -0.5
    for h in range(heads):
        ls = slice(h * dh, (h + 1) * dh)
        s = lax.dot_general(q[:, ls].astype(BF16), k_ref[:, ls], (((1,), (1,)), ((), ())),
                            preferred_element_type=F32) * scale
        e = jnp.exp(s - jnp.max(s, axis=-1, keepdims=True))
        p = e / jnp.sum(e, axis=-1, keepdims=True)
        o = jnp.dot(p.astype(BF16), v_ref[:, ls], preferred_element_type=F32)
        o_ref[:, ls] = o.astype(o_ref.dtype)


def _attn_branch(x, wq, kv, seq, n_mem):
    n, d = x.shape
    xw = wq.shape[1]
    heads = xw // XA_DH
    tm = _tile(seq, 512)
    per_seq = seq // tm
    vmem = 2 * (_nbytes((tm, d), x.dtype) + _nbytes(wq.shape, BF16) + 2 * _nbytes((n_mem, xw), BF16)
                + _nbytes((tm, xw), BF16)) + 4 * _nbytes((tm, xw), F32) + 4 * _nbytes((tm, n_mem), F32)
    return pl.pallas_call(
        functools.partial(_attn_kernel, heads=heads),
        out_shape=jax.ShapeDtypeStruct((n, xw), BF16),
        grid=(n // tm,),
        in_specs=[pl.BlockSpec((tm, d), lambda i: (i, 0)),
                  pl.BlockSpec(wq.shape, lambda i: (0, 0)),
                  pl.BlockSpec((n_mem, xw), lambda i: (i // per_seq, 0)),
                  pl.BlockSpec((n_mem, xw), lambda i: (i // per_seq, 1))],
        out_specs=pl.BlockSpec((tm, xw), lambda i: (i, 0)),
        compiler_params=_params(("arbitrary",), vmem),
    )(x, wq, kv, kv)


def _merge_kernel(x_ref, ya_ref, yb_ref, yx_ref, wga_ref, wgb_ref, wgx_ref, wa_ref, wb_ref, wx_ref,
                  o_ref, xb_ref):
    @pl.when(pl.program_id(1) == 0)
    def _():
        xb_ref[...] = x_ref[...].astype(BF16)

    xb = xb_ref[...]
    dot = functools.partial(jnp.dot, preferred_element_type=F32)
    m = _sigmoid(dot(xb, wga_ref[...])) * dot(ya_ref[...], wa_ref[...])
    m = m + _sigmoid(dot(xb, wgb_ref[...])) * dot(yb_ref[...], wb_ref[...])
    m = m + _sigmoid(dot(xb, wgx_ref[...])) * dot(yx_ref[...], wx_ref[...])
    o_ref[...] = m.astype(o_ref.dtype)


def _merge(x, ya, yb, yx, w_gates, wa, wb, wx):
    n, d = x.shape
    tm, tn = _tile(n, 1024), _tile(d, 512)
    nj = d // tn
    row = lambda a: pl.BlockSpec((tm, a.shape[1]), lambda i, j: (i, 0))
    gate = lambda g: pl.BlockSpec((d, tn), lambda i, j, g=g: (0, g * nj + j))
    outw = lambda a: pl.BlockSpec((a.shape[0], tn), lambda i, j: (0, j))
    vmem = 2 * (_nbytes((tm, d), x.dtype) + _nbytes((tm, ya.shape[1] + yb.shape[1] + yx.shape[1]), BF16)
                + 3 * _nbytes((d, tn), BF16) + _nbytes((wa.shape[0] + wb.shape[0] + wx.shape[0], tn), BF16)
                + _nbytes((tm, tn), BF16)) + _nbytes((tm, d), BF16) + 8 * _nbytes((tm, tn), F32)
    return pl.pallas_call(
        _merge_kernel,
        out_shape=jax.ShapeDtypeStruct((n, d), BF16),
        grid=(n // tm, nj),
        in_specs=[row(x), row(ya), row(yb), row(yx), gate(0), gate(1), gate(2),
                  outw(wa), outw(wb), outw(wx)],
        out_specs=pl.BlockSpec((tm, tn), lambda i, j: (i, j)),
        scratch_shapes=[pltpu.VMEM((tm, d), BF16)],
        compiler_params=_params(("arbitrary", "arbitrary"), vmem),
    )(x, ya, yb, yx, w_gates, w_gates, w_gates, wa, wb, wx)


def _oproj_ln_kernel(m_ref, w_ref, x_ref, g_ref, b_ref, rw_ref, rb_ref,
                     o_ref, p_ref, eid_ref, rnk_ref, gate_ref, cnt_ref, carry_ref, *, alpha):
    y = alpha * x_ref[...] + jnp.dot(m_ref[...], w_ref[...], preferred_element_type=F32)
    y = _layer_norm_rows(y, g_ref[...], b_ref[...])
    o_ref[...] = y
    _store_packed_rows(p_ref, y)
    _route(y, rw_ref, rb_ref, eid_ref, rnk_ref, gate_ref, cnt_ref, carry_ref)


def _oproj_ln_route(merged, w_o, x, g, b, rw_t, rb, alpha):
    n, d = x.shape
    assert d % PAIR == 0
    pieces = d // PAIR
    ne = rw_t.shape[0]
    tm = _tile(n, 512)
    cur = lambda i: (i, 0)
    vec = pl.BlockSpec((1, d), lambda i: (0, 0))
    topk = lambda dt: jax.ShapeDtypeStruct((TOP_K, n), dt)
    kspec = pl.BlockSpec((TOP_K, tm), lambda i: (0, i))
    vmem = 2 * (_nbytes((tm, d), BF16) + _nbytes((d, d), BF16) + 2 * _nbytes((tm, d), F32)
                + _nbytes((tm, d // 2), U32) + _nbytes((ne, d), F32)) + 4 * _nbytes((tm, d), F32) \
        + 3 * _nbytes((tm, tm), F32) + 24 * _nbytes((ne, tm), F32)
    return pl.pallas_call(
        functools.partial(_oproj_ln_kernel, alpha=alpha),
        out_shape=(jax.ShapeDtypeStruct((n, d), F32), jax.ShapeDtypeStruct((n * pieces, LANES), U32),
                   topk(I32), topk(I32), topk(F32), jax.ShapeDtypeStruct((ne, 128), I32)),
        grid=(n // tm,),
        in_specs=[pl.BlockSpec((tm, d), cur),
                  pl.BlockSpec((d, d), lambda i: (0, 0)),
                  pl.BlockSpec((tm, d), cur), vec, vec,
                  pl.BlockSpec((ne, d), lambda i: (0, 0)),
                  pl.BlockSpec((ne, 1), lambda i: (0, 0))],
        out_specs=(pl.BlockSpec((tm, d), cur),
                   pl.BlockSpec((tm * pieces, LANES), cur),
                   kspec, kspec, kspec, pl.BlockSpec((ne, 128), lambda i: (0, 0))),
        scratch_shapes=[pltpu.VMEM((ne, 1), F32)],
        compiler_params=_params(("arbitrary",), vmem),
    )(merged, w_o, x, g, b, rw_t, rb)


def _rank_desc(vals):
    rows = vals.shape[0]
    iota = lax.broadcasted_iota(I32, vals.shape, 0)
    rank = jnp.zeros(vals.shape, I32)
    for j in range(rows):
        rowv = vals[j:j + 1, :]
        tie = jnp.where(iota > j, 1, 0)
        rank = rank + jnp.where(rowv > vals, 1, jnp.where(rowv == vals, tie, 0))
    return rank


def _route(x, rw_ref, rb_ref, eid_ref, rnk_ref, gate_ref, cnt_ref, carry_ref):
    ne = rw_ref.shape[0]
    tm = x.shape[0]
    gsz = ne // N_GROUPS

    @pl.when(pl.program_id(0) == 0)
    def _():
        carry_ref[...] = jnp.zeros_like(carry_ref)

    logits = lax.dot_general(rw_ref[...], x, (((1,), (1,)), ((), ())),
                             preferred_element_type=F32, precision=lax.Precision.HIGHEST)
    s = _sigmoid(logits)
    sel = s + rb_ref[...]

    grp = sel.reshape(N_GROUPS, gsz, tm)
    sub_iota = lax.broadcasted_iota(I32, (N_GROUPS, gsz, tm), 1)
    m1 = jnp.max(grp, axis=1, keepdims=True)
    first = jnp.min(jnp.where(grp == m1, sub_iota, gsz), axis=1, keepdims=True)
    m2 = jnp.max(jnp.where(sub_iota == first, -jnp.inf, grp), axis=1, keepdims=True)
    gscore = (m1 + m2).reshape(N_GROUPS, tm)

    gsel = _rank_desc(gscore) < TOPK_GROUPS
    emask = jnp.broadcast_to(gsel.reshape(N_GROUPS, 1, tm), (N_GROUPS, gsz, tm)).reshape(ne, tm)
    masked = jnp.where(emask, sel, -jnp.inf)

    eiota = lax.broadcasted_iota(I32, (ne, tm), 0)
    chosen = (_rank_desc(masked) < TOP_K) & emask

    w = jnp.where(chosen, s, 0.0)
    gate = w / jnp.sum(w, axis=0, keepdims=True) * ROUTED_SCALE

    ch = jnp.where(chosen, 1.0, 0.0).astype(BF16)
    tr = lax.broadcasted_iota(I32, (tm, tm), 0)
    tc = lax.broadcasted_iota(I32, (tm, tm), 1)
    before = jnp.where(tr < tc, 1.0, 0.0).astype(BF16)
    rank_tok = (carry_ref[...] + jnp.dot(ch, before, preferred_element_type=F32)).astype(I32)
    carry_ref[...] = carry_ref[...] + jnp.sum(jnp.where(chosen, 1.0, 0.0), axis=1, keepdims=True)
    cnt_ref[...] = jnp.broadcast_to(carry_ref[...], cnt_ref.shape).astype(I32)

    er = lax.broadcasted_iota(I32, (ne, ne), 0)
    ec = lax.broadcasted_iota(I32, (ne, ne), 1)
    lower = jnp.where(ec < er, 1.0, 0.0).astype(BF16)
    slot = jnp.dot(lower, ch, preferred_element_type=F32).astype(I32)
    for k in range(TOP_K):
        pick = chosen & (slot == k)
        eid_ref[k:k + 1, :] = jnp.sum(jnp.where(pick, eiota, 0), axis=0, keepdims=True)
        rnk_ref[k:k + 1, :] = jnp.sum(jnp.where(pick, rank_tok, 0), axis=0, keepdims=True)
        gate_ref[k:k + 1, :] = jnp.sum(jnp.where(pick, gate, 0.0), axis=0, keepdims=True)


def _pad_pieces(max_pad):
    pieces = []
    p = 1
    while p <= max_pad:
        pieces.append(p)
        p *= 2
    return pieces[::-1]


def _dispatch_kernel(pad_ref, dest_hbm, xp_ref, zero_hbm, sg_ref, su_ref, sd_ref, xs_hbm, sh_ref,
                     idx_ref, sem_idx, sem_row, sem_pad, *, tq, ne, rp):
    i = pl.program_id(0)

    def row_copy(src_row, dst_row):
        return pltpu.make_async_copy(xp_ref.at[pl.ds(pl.multiple_of(src_row * rp, rp), rp)],
                                     xs_hbm.at[pl.ds(pl.multiple_of(dst_row * rp, rp), rp)], sem_row)

    def pad_copies(fn):
        def per_expert(e, carry):
            row = pad_ref[0, e]
            npad = pad_ref[1, e]
            for p in _pad_pieces(EXPERT_ROWS - 1):
                @pl.when((npad & p) != 0)
                def _(row=row, p=p):
                    fn(pltpu.make_async_copy(zero_hbm.at[pl.ds(0, p * rp)],
                                             xs_hbm.at[pl.ds(pl.multiple_of(row * rp, rp), p * rp)], sem_pad))
                row = row + (npad & p)
            return carry
        lax.fori_loop(0, ne, per_expert, 0)

    @pl.when(i == 0)
    def _():
        pad_copies(lambda cp: cp.start())

    idx_cp = pltpu.make_async_copy(dest_hbm.at[i], idx_ref, sem_idx)
    idx_cp.start()
    idx_cp.wait()

    def per_token(t, carry):
        for k in range(TOP_K):
            row_copy(t, idx_ref[t * TOP_K + k]).start(priority=k % 2)
        return carry

    d = sh_ref.shape[1]
    hid = sg_ref.shape[1]
    slab = min(PAIR, hid, d)
    xb = _load_packed_rows(xp_ref, tq, d)
    dot = functools.partial(jnp.dot, preferred_element_type=F32)
    hmid = [None] * (hid // slab)

    def gate_up(j):
        cols = slice(j * slab, (j + 1) * slab)
        hmid[j] = (_silu(dot(xb, sg_ref[:, cols])) * dot(xb, su_ref[:, cols])).astype(BF16)

    def down(c):
        cols = slice(c * slab, (c + 1) * slab)
        acc = dot(hmid[0], sd_ref[0:slab, cols])
        for j in range(1, hid // slab):
            acc = acc + dot(hmid[j], sd_ref[j * slab:(j + 1) * slab, cols])
        sh_ref[:, cols] = acc

    pieces = [(gate_up, j, 2 * d) for j in range(hid // slab)] + [(down, c, hid) for c in range(d // slab)]
    total = sum(w for _, _, w in pieces)
    done = 0
    lo = 0
    for fn, arg, work in pieces:
        done += work
        hi = tq if done == total else (tq * done // total)
        lax.fori_loop(lo, hi, per_token, 0)
        fn(arg)
        lo = hi

    def drain(t, carry):
        for k in range(TOP_K):
            row_copy(0, 0).wait()
        return carry

    lax.fori_loop(0, tq, drain, 0)

    @pl.when(i == 0)
    def _():
        pad_copies(lambda cp: cp.wait())


def _dispatch(pad_info, dest_tok, xp, n, rows, sg, su, sd):
    rp = xp.shape[0] // n
    d, hid = sg.shape
    ne = pad_info.shape[1]
    tq = _tile(n, 512)
    steps = n // tq
    dest2 = dest_tok.reshape(steps, tq * TOP_K)
    zero = jnp.zeros((EXPERT_ROWS * rp, LANES), U32)
    any_spec = pl.BlockSpec(memory_space=pl.ANY)
    full = lambda a: pl.BlockSpec(a.shape, lambda i, pad: (0, 0))
    grid_spec = pltpu.PrefetchScalarGridSpec(
        num_scalar_prefetch=1, grid=(steps,),
        in_specs=[any_spec, pl.BlockSpec((tq * rp, LANES), lambda i, pad: (i, 0)), any_spec,
                  full(sg), full(su), full(sd)],
        out_specs=(any_spec, pl.BlockSpec((tq, d), lambda i, pad: (i, 0))),
        scratch_shapes=[pltpu.SMEM((tq * TOP_K,), I32), pltpu.SemaphoreType.DMA(()),
                        pltpu.SemaphoreType.DMA(()), pltpu.SemaphoreType.DMA(())])
    vmem = 2 * (_nbytes((tq * rp, LANES), U32) + 3 * _nbytes((d, hid), BF16) + _nbytes((tq, d), F32)) \
        + _nbytes((tq, d), BF16) + 3 * _nbytes((tq, hid), F32) + _nbytes((tq, d), F32)
    return pl.pallas_call(
        functools.partial(_dispatch_kernel, tq=tq, ne=ne, rp=rp),
        out_shape=(jax.ShapeDtypeStruct((rows * rp, LANES), U32), jax.ShapeDtypeStruct((n, d), F32)),
        grid_spec=grid_spec,
        compiler_params=_params(("arbitrary",), vmem),
    )(pad_info, dest2, xp, zero, sg, su, sd)


def _expert_kernel(plan_ref, nu_ref, xs_ref, wg_hbm, wu_hbm, wd_hbm, ys_ref,
                   sg_ref, su_ref, sd_ref, wgb_ref, wub_ref, wdb_ref, sem_ref, *, layer):
    b = pl.program_id(0)

    def fetch(e, slot):
        return (pltpu.make_async_copy(wg_hbm.at[layer, e], sg_ref.at[slot], sem_ref.at[slot]),
                pltpu.make_async_copy(wu_hbm.at[layer, e], su_ref.at[slot], sem_ref.at[slot]),
                pltpu.make_async_copy(wd_hbm.at[layer, e], sd_ref.at[slot], sem_ref.at[slot]))

    @pl.when(b < nu_ref[0])
    def _():
        e = plan_ref[0, b]
        nxt = plan_ref[2, b]
        slot = plan_ref[3, b]

        @pl.when(b == 0)
        def _():
            for cp in fetch(e, slot):
                cp.start()

        @pl.when(plan_ref[1, b] == 1)
        def _():
            @pl.when(nxt >= 0)
            def _():
                for cp in fetch(nxt, 1 - slot):
                    cp.start()

            for cp in fetch(e, slot):
                cp.wait()
            wgb_ref[...] = sg_ref[slot].astype(BF16)
            wub_ref[...] = su_ref[slot].astype(BF16)
            wdb_ref[...] = sd_ref[slot].astype(BF16)

        t = EXPERT_ROWS
        d = wgb_ref.shape[0]
        xb = _load_packed_rows(xs_ref, t, d)
        dot = functools.partial(jnp.dot, preferred_element_type=F32)
        hmid = (_silu(dot(xb, wgb_ref[...])) * dot(xb, wub_ref[...])).astype(BF16)
        _store_packed_rows(ys_ref, dot(hmid, wdb_ref[...]))


def _experts(plan, n_used, xs, wg, wu, wd, layer):
    _, ne, d, hid = wg.shape
    t = EXPERT_ROWS
    rp = d // PAIR
    rows = xs.shape[0] // rp
    nb = rows // t
    clamp = lambda b, plan, nu: (jnp.minimum(b, jnp.maximum(nu[0] - 1, 0)), 0)
    any_spec = pl.BlockSpec(memory_space=pl.ANY)
    grid_spec = pltpu.PrefetchScalarGridSpec(
        num_scalar_prefetch=2, grid=(nb,),
        in_specs=[pl.BlockSpec((t * rp, LANES), clamp), any_spec, any_spec, any_spec],
        out_specs=pl.BlockSpec((t * rp, LANES), clamp),
        scratch_shapes=[pltpu.VMEM((2, d, hid), F32), pltpu.VMEM((2, d, hid), F32), pltpu.VMEM((2, hid, d), F32),
                        pltpu.VMEM((d, hid), BF16), pltpu.VMEM((d, hid), BF16), pltpu.VMEM((hid, d), BF16),
                        pltpu.SemaphoreType.DMA((2,))])
    vmem = 2 * (3 * _nbytes((d, hid), F32) + 2 * _nbytes((t, d // 2), U32)) \
        + 3 * _nbytes((d, hid), BF16) + 3 * _nbytes((t, d), F32)
    return pl.pallas_call(
        functools.partial(_expert_kernel, layer=layer),
        out_shape=jax.ShapeDtypeStruct((rows * rp, LANES), U32),
        grid_spec=grid_spec,
        compiler_params=_params(("arbitrary",), vmem),
    )(plan, n_used, xs, wg, wu, wd)


def _combine_kernel(dest_ref, ys_hbm, x_ref, gate_ref, sh_ref, g_ref, b_ref, o_ref,
                    buf_ref, sem_ref, *, tc, alpha):
    i = pl.program_id(0)
    rp = x_ref.shape[1] // PAIR
    half = i % 2

    def row_copy(src_row, dst_row, h):
        return pltpu.make_async_copy(ys_hbm.at[pl.ds(pl.multiple_of(src_row * rp, rp), rp)],
                                     buf_ref.at[pl.ds(pl.multiple_of(dst_row * rp, rp), rp)], sem_ref.at[h])

    def fetch(step, h):
        def per_group(g, carry):
            src0 = (step * tc + g * COMBINE_UNROLL) * TOP_K
            dst0 = h * TOP_K * tc + g * COMBINE_UNROLL
            for u in range(COMBINE_UNROLL):
                for k in range(TOP_K):
                    row_copy(dest_ref[src0 + u * TOP_K + k], dst0 + k * tc + u, h).start(priority=k % 2)
            return carry
        lax.fori_loop(0, tc // COMBINE_UNROLL, per_group, 0)

    @pl.when(i == 0)
    def _():
        fetch(0, 0)

    @pl.when(i + 1 < pl.num_programs(0))
    def _():
        fetch(i + 1, 1 - half)

    x = x_ref[...]
    shared = sh_ref[...]

    def drain(g, carry):
        for _ in range(COMBINE_UNROLL * TOP_K):
            row_copy(0, 0, half).wait()
        return carry

    lax.fori_loop(0, tc // COMBINE_UNROLL, drain, 0)

    gate = gate_ref[...]
    base = half * (TOP_K * tc * rp)
    cols = []
    for s in range(rp):
        lo = hi = None
        for k in range(TOP_K):
            w = buf_ref[pl.ds(base + k * tc * rp + s, tc, stride=rp), :]
            gk = gate[:, k:k + 1]
            lo_k = lax.bitcast_convert_type(w << 16, F32) * gk
            hi_k = lax.bitcast_convert_type(w & jnp.uint32(0xFFFF0000), F32) * gk
            lo = lo_k if lo is None else lo + lo_k
            hi = hi_k if hi is None else hi + hi_k
        cols += [lo, hi]
    routed = jnp.concatenate(cols, axis=1)
    o_ref[...] = _layer_norm_rows(alpha * x + (shared + routed), g_ref[...], b_ref[...])


def _combine(dest_tok, ys, x, gate_tok, shared, g, b, alpha):
    n, d = x.shape
    tc = _tile(n, 128)
    steps = n // tc
    rp = d // PAIR
    vec = pl.BlockSpec((1, d), lambda i, dest: (0, 0))
    rows = pl.BlockSpec((tc, d), lambda i, dest: (i, 0))
    buf_rows = 2 * TOP_K * tc * rp
    vmem = _nbytes((buf_rows, LANES), U32) + 2 * 3 * _nbytes((tc, d), F32) + 6 * _nbytes((tc, d), F32)
    grid_spec = pltpu.PrefetchScalarGridSpec(
        num_scalar_prefetch=1, grid=(steps,),
        in_specs=[pl.BlockSpec(memory_space=pl.ANY), rows,
                  pl.BlockSpec((tc, TOP_K), lambda i, dest: (i, 0)), rows, vec, vec],
        out_specs=pl.BlockSpec((tc, d), lambda i, dest: (i, 0)),
        scratch_shapes=[pltpu.VMEM((buf_rows, LANES), U32), pltpu.SemaphoreType.DMA((2,))])
    return pl.pallas_call(
        functools.partial(_combine_kernel, tc=tc, alpha=alpha),
        out_shape=jax.ShapeDtypeStruct((n, d), F32),
        grid_spec=grid_spec,
        compiler_params=_params(("arbitrary",), vmem),
    )(dest_tok, ys, x, gate_tok, shared, g, b)


def _moe(x1, x1p, eid, rnk, gate, cnt, wg, wu, wd, layer, sg, su, sd, g, b, alpha):
    n, d = x1.shape
    ne = cnt.shape[0]
    counts = cnt[:, 0]
    padded = (counts + EXPERT_ROWS - 1) // EXPERT_ROWS * EXPERT_ROWS
    eidx = jnp.arange(ne, dtype=I32)
    pad_end = jnp.sum(jnp.where(eidx[None, :] <= eidx[:, None], padded[None, :], 0), axis=1)
    pad_start = pad_end - padded
    n_blocks = (n * TOP_K + ne * (EXPERT_ROWS - 1)) // EXPERT_ROWS
    rows = n_blocks * EXPERT_ROWS
    start_of = jnp.sum(jnp.where(eid[:, :, None] == eidx, pad_start, 0), axis=-1)
    dest_tok = (start_of + rnk).T.reshape(-1).astype(I32)
    pad_info = jnp.stack([pad_start + counts, padded - counts]).astype(I32)
    total = pad_end[ne - 1]
    n_used = (total // EXPERT_ROWS).astype(I32).reshape(1)
    blk_start = jnp.minimum(jnp.arange(n_blocks, dtype=I32) * EXPERT_ROWS, total - 1)
    blk_exp = jnp.sum((pad_end[None, :] <= blk_start[:, None]).astype(I32), axis=1)
    blk_exp = jnp.minimum(blk_exp, ne - 1).astype(I32)
    has = counts > 0
    later = jnp.where((eidx[None, :] > eidx[:, None]) & has[None, :], eidx[None, :], ne)
    next_e = jnp.min(later, axis=1)
    next_e = jnp.where(next_e == ne, -1, next_e)
    ordinal = jnp.sum(jnp.where((eidx[None, :] <= eidx[:, None]) & has[None, :], 1, 0), axis=1) - 1
    first = jnp.concatenate([jnp.ones((1,), I32), (blk_exp[1:] != blk_exp[:-1]).astype(I32)])
    plan = jnp.stack([blk_exp, first, next_e[blk_exp], ordinal[blk_exp] % 2]).astype(I32)
    xs, shared = _dispatch(pad_info, dest_tok, x1p, n, rows, sg, su, sd)
    ys = _experts(plan, n_used, xs, wg, wu, wd, layer)
    return _combine(dest_tok, ys, x1, gate.T, shared, g, b, alpha)


def kernel(x, mem, w_in, conv_w, w_conv_out, hg_lb_logits, hg_norm_g, w_hg_out, w_mem_k, w_mem_v,
           w_xa_out, w_o, ln1_g, ln1_b, router_w, router_b, exp_wg, exp_wu, exp_wd,
           sh_wg, sh_wu, sh_wd, ln2_g, ln2_b):
    bsz, seq, d = x.shape
    depth = w_in.shape[0]
    n_mem = mem.shape[1]
    cwid = conv_w.shape[2]
    hw = w_hg_out.shape[1]
    xw = w_mem_k.shape[2]
    alpha = float((2 * depth) ** 0.25)
    n = bsz * seq

    lb_all = jnp.cumsum(jax.nn.softmax(hg_lb_logits.astype(F32), axis=0), axis=0)
    lb_all = lb_all - lb_all[0:1]
    o_conv, o_hg, o_xa, o_gate = 0, 3 * cwid, 3 * cwid + 4 * hw, 3 * cwid + 4 * hw + xw

    h = x.reshape(n, d)
    memf = mem.reshape(bsz * n_mem, d)
    for l in range(depth):
        wl = w_in[l]
        w_conv_in = wl[:, o_conv:o_hg].astype(BF16)
        w_hg_in = wl[:, o_hg:o_xa].astype(BF16)
        w_xa_in = wl[:, o_xa:o_gate].astype(BF16)
        w_gates = wl[:, o_gate:].astype(BF16)
        vec = lambda a: a.astype(F32).reshape(1, -1)

        ya = _conv_branch(h, w_conv_in, conv_w[l].astype(F32), seq)
        lb = lb_all[l]
        yb = _hgrn_branch(h, w_hg_in, vec(jnp.log(lb)), vec(jnp.log1p(-lb)), vec(hg_norm_g[l]), bsz, seq)
        w_kv = jnp.concatenate([w_mem_k[l], w_mem_v[l]], axis=1).astype(BF16)
        kv = _matmul(memf, w_kv, BF16, 1024, xw)
        yx = _attn_branch(h, w_xa_in, kv, seq, n_mem)
        merged = _merge(h, ya, yb, yx, w_gates, w_conv_out[l].astype(BF16), w_hg_out[l].astype(BF16),
                        w_xa_out[l].astype(BF16))
        x1, x1p, eid, rnk, gate, cnt = _oproj_ln_route(
            merged, w_o[l].astype(BF16), h, vec(ln1_g[l]), vec(ln1_b[l]),
            router_w[l].astype(F32).T, router_b[l].astype(F32).reshape(-1, 1), alpha)
        h = _moe(x1, x1p, eid, rnk, gate, cnt, exp_wg, exp_wu, exp_wd, l,
                    sh_wg[l].astype(BF16), sh_wu[l].astype(BF16), sh_wd[l].astype(BF16),
                    vec(ln2_g[l]), vec(ln2_b[l]), alpha)
    return h.reshape(bsz, seq, d)
```

```python
import functools

import jax
import jax.numpy as jnp
import numpy as np
from jax import lax
from jax.experimental import pallas as pl
from jax.experimental.pallas import tpu as pltpu

F32 = jnp.float32
BF16 = jnp.bfloat16
I32 = jnp.int32
U32 = jnp.uint32

HG_DK = 128
XA_DH = 128
TOP_K = 8
N_GROUPS = 8
TOPK_GROUPS = 4
ROUTED_SCALE = 2.5
LN_EPS = 1e-5
RMS_EPS = 1e-6
HG_CHUNK = 64
HG_LEVELS = (1, 2, 4, 8, 16, 32)
EXPERT_ROWS = 512
COPY_UNROLL = 8

V7X_VMEM_BYTES = 64 * 1024 * 1024
VMEM_CAP = V7X_VMEM_BYTES - 8 * 1024 * 1024


def _params(semantics, vmem_bytes):
    limit = int(min(VMEM_CAP, max(32 * 1024 * 1024, vmem_bytes * 5 // 4 + (4 << 20))))
    return pltpu.CompilerParams(dimension_semantics=semantics, vmem_limit_bytes=limit)


def _nbytes(shape, dtype):
    n = 1
    for s in shape:
        n *= s
    return n * jnp.dtype(dtype).itemsize


def _tile(n, want):
    t = min(n, want)
    while n % t:
        t //= 2
    return t


def _silu(x):
    return x * (1.0 / (1.0 + jnp.exp(-x)))


def _sigmoid(x):
    return 1.0 / (1.0 + jnp.exp(-x))


def _layer_norm_rows(y, g, b):
    mu = jnp.mean(y, axis=-1, keepdims=True)
    yc = y - mu
    var = jnp.mean(yc * yc, axis=-1, keepdims=True)
    return yc * lax.rsqrt(var + LN_EPS) * g + b


LANES = 128
PAIR = 2 * LANES


def _store_packed_rows(p_ref, y):
    rows, d = y.shape
    pieces = d // PAIR
    for s in range(pieces):
        lo = lax.bitcast_convert_type(y[:, s * PAIR:s * PAIR + LANES].astype(BF16).astype(F32), U32)
        hi = lax.bitcast_convert_type(y[:, s * PAIR + LANES:(s + 1) * PAIR].astype(BF16).astype(F32), U32)
        p_ref[pl.ds(s, rows, stride=pieces), :] = (hi & jnp.uint32(0xFFFF0000)) | (lo >> 16)


def _load_packed_rows(p_ref, rows, d):
    pieces = d // PAIR
    cols = []
    for s in range(pieces):
        w = p_ref[pl.ds(s, rows, stride=pieces), :]
        cols.append(lax.bitcast_convert_type(w << 16, F32).astype(BF16))
        cols.append(lax.bitcast_convert_type(w & jnp.uint32(0xFFFF0000), F32).astype(BF16))
    return jnp.concatenate(cols, axis=1)


def _matmul_kernel(x_ref, w_ref, o_ref):
    o_ref[...] = jnp.dot(x_ref[...].astype(BF16), w_ref[...],
                         preferred_element_type=F32).astype(o_ref.dtype)


def _matmul(x, w, out_dtype, tm, tn):
    m, k = x.shape
    n = w.shape[1]
    tm, tn = _tile(m, tm), _tile(n, tn)
    vmem = 2 * (_nbytes((tm, k), x.dtype) + _nbytes((k, tn), w.dtype) + _nbytes((tm, tn), out_dtype)) \
        + _nbytes((tm, k), BF16) + _nbytes((tm, tn), F32)
    return pl.pallas_call(
        _matmul_kernel,
        out_shape=jax.ShapeDtypeStruct((m, n), out_dtype),
        grid=(m // tm, n // tn),
        in_specs=[pl.BlockSpec((tm, k), lambda i, j: (i, 0)),
                  pl.BlockSpec((k, tn), lambda i, j: (0, j))],
        out_specs=pl.BlockSpec((tm, tn), lambda i, j: (i, j)),
        compiler_params=_params(("arbitrary", "arbitrary"), vmem),
    )(x, w)


def _conv_kernel(x_ref, w_ref, cw_ref, o_ref, carry_ref, *, tiles_per_seq):
    i = pl.program_id(0)
    tm, cwid = o_ref.shape

    @pl.when(i % tiles_per_seq == 0)
    def _():
        carry_ref[...] = jnp.zeros_like(carry_ref)

    p = jnp.dot(x_ref[...].astype(BF16), w_ref[...], preferred_element_type=F32)
    u = p[:, :cwid]
    b = p[:, cwid:2 * cwid]
    c = p[:, 2 * cwid:]
    cu = c * u
    prev = carry_ref[...]
    row = lax.broadcasted_iota(I32, (tm, cwid), 0)
    cu1 = jnp.where(row == 0, prev[7:8, :], pltpu.roll(cu, 1, 0))
    cu2 = pltpu.roll(cu, 2, 0)
    cu2 = jnp.where(row == 0, prev[6:7, :], jnp.where(row == 1, prev[7:8, :], cu2))
    cw = cw_ref[...]
    y = b * (cw[0:1, :] * cu2 + cw[1:2, :] * cu1 + cw[2:3, :] * cu)
    o_ref[...] = y.astype(o_ref.dtype)
    carry_ref[...] = cu[tm - 8:, :]


def _conv_branch(x, w_conv_in, conv_w, seq):
    n, d = x.shape
    cwid = conv_w.shape[1]
    assert conv_w.shape[0] == 3
    tm = _tile(seq, 512)
    vmem = 2 * (_nbytes((tm, d), x.dtype) + _nbytes(w_conv_in.shape, BF16) + _nbytes((tm, cwid), BF16)) \
        + 6 * _nbytes((tm, 3 * cwid), F32)
    return pl.pallas_call(
        functools.partial(_conv_kernel, tiles_per_seq=seq // tm),
        out_shape=jax.ShapeDtypeStruct((n, cwid), BF16),
        grid=(n // tm,),
        in_specs=[pl.BlockSpec((tm, d), lambda i: (i, 0)),
                  pl.BlockSpec(w_conv_in.shape, lambda i: (0, 0)),
                  pl.BlockSpec(conv_w.shape, lambda i: (0, 0))],
        out_specs=pl.BlockSpec((tm, cwid), lambda i: (i, 0)),
        scratch_shapes=[pltpu.VMEM((8, cwid), F32)],
        compiler_params=_params(("arbitrary",), vmem),
    )(x, w_conv_in, conv_w)


def _head_sum(x):
    return jnp.sum(x, axis=-1, keepdims=True)


def _hgrn_tables(tt):
    t = np.arange(tt)[:, None]
    j = np.arange(tt)[None, :]
    tri = ((t // HG_CHUNK) == (j // HG_CHUNK)) & (j <= t)
    level = np.zeros((tt, tt), np.int32)
    for i, h in enumerate(HG_LEVELS):
        level[((t // (2 * h)) == (j // (2 * h))) & ((t % (2 * h)) >= h) & ((j % (2 * h)) < h)] = i + 1
    return tri.astype(np.float32), level


def _hgrn_kernel(x_ref, w_ref, tri_ref, lvl_ref, c1_ref, c2_ref, gain_ref, o_ref,
                 st_ref, p_ref, u_ref, qd_ref, kd_ref, vb_ref, dec_ref, *, heads):
    tt, hw = o_ref.shape
    dk = HG_DK
    ch = HG_CHUNK
    nlev = len(HG_LEVELS)

    @pl.when(pl.program_id(1) == 0)
    def _():
        st_ref[...] = jnp.zeros_like(st_ref)

    p_ref[...] = jnp.dot(x_ref[...].astype(BF16), w_ref[...], preferred_element_type=F32)
    q_ref, f_ref, v_ref, g_ref = (p_ref.at[:, pl.ds(j * hw, hw)] for j in range(4))

    f = f_ref[...]
    log_sig = jnp.minimum(f, 0.0) - jnp.log(1.0 + jnp.exp(-jnp.abs(f)))
    a1 = c1_ref[...]
    a2 = c2_ref[...] + log_sig
    log_f = jnp.maximum(a1, a2) + jnp.log(1.0 + jnp.exp(-jnp.abs(a1 - a2)))
    kin = 1.0 - jnp.exp(log_f)

    t_hi = log_f.astype(BF16)
    rem = log_f - t_hi.astype(F32)
    t_mid = rem.astype(BF16)
    t_lo = (rem - t_mid.astype(F32)).astype(BF16)

    tri = tri_ref[...]
    dot = functools.partial(jnp.dot, preferred_element_type=F32)
    bcum = dot(tri, t_hi) + dot(tri, t_mid) + dot(tri, t_lo)

    def group_row(a, size, idx):
        a3 = a.reshape(tt // size, size, hw)
        return jnp.broadcast_to(a3[:, idx:idx + 1, :], a3.shape).reshape(tt, hw)

    blast = group_row(bcum, ch, ch - 1)
    row = lax.broadcasted_iota(I32, (tt, hw), 0)

    def level_decay(h):
        upper = (row % (2 * h)) >= h
        if h == 1:
            return upper, jnp.where(upper, 1.0 - kin, 1.0)
        if h == 2:
            pos = row % 4
            nxt = pltpu.roll(log_f, tt - 1, 0)
            prv = pltpu.roll(log_f, 1, 0)
            z = jnp.where(pos == 0, nxt, jnp.where(pos == 1, 0.0, jnp.where(pos == 2, log_f, log_f + prv)))
            return upper, jnp.exp(z)
        refb = group_row(bcum, 2 * h, h - 1)
        return upper, jnp.exp(jnp.where(upper, bcum - refb, refb - bcum))

    q = q_ref[...]
    qd_ref[...] = (q * jnp.exp(bcum)).astype(BF16)
    kd_ref[...] = (kin * jnp.exp(blast - bcum)).astype(BF16)
    vb_ref[...] = v_ref[...].astype(BF16)
    dec_ref[...] = jnp.exp(blast)

    for i, h in enumerate(HG_LEVELS):
        upper, decay = level_decay(h)
        u_ref[i] = (jnp.where(upper, q, kin) * decay).astype(BF16)

    lvl = lvl_ref[...]
    gain = gain_ref[...]
    nt_dims = (((1,), (1,)), ((), ()))
    for h in range(heads):
        ls = slice(h * dk, (h + 1) * dk)
        scores = jnp.zeros((tt, tt), F32)
        for i in range(nlev):
            u = u_ref[i, :, ls]
            gram = lax.dot_general(u, u, nt_dims, preferred_element_type=F32)
            scores = scores + jnp.where(lvl == i + 1, gram, 0.0)
        acc = jnp.dot(scores.astype(BF16), vb_ref[:, ls], preferred_element_type=F32)
        acc = acc + _head_sum(q[:, ls] * kin[:, ls]) * v_ref[:, ls]
        inter = []
        for c in range(tt // ch):
            rs = slice(c * ch, (c + 1) * ch)
            st = st_ref[h]
            inter.append(lax.dot_general(qd_ref[rs, ls], st.astype(BF16), nt_dims, preferred_element_type=F32))
            upd = lax.dot_general(vb_ref[rs, ls], kd_ref[rs, ls], (((0,), (0,)), ((), ())),
                                  preferred_element_type=F32)
            st_ref[h] = st * dec_ref[c * ch:c * ch + 1, ls] + upd
        acc = acc + jnp.concatenate(inter, axis=0)
        ms = _head_sum(acc * acc) * (1.0 / dk)
        y = acc * lax.rsqrt(ms + RMS_EPS) * gain[:, ls] * _silu(g_ref[:, ls])
        o_ref[:, ls] = y.astype(o_ref.dtype)


def _hgrn_branch(x, w_hg_in, c1, c2, gain, bsz, seq):
    n, d = x.shape
    hw = w_hg_in.shape[1] // 4
    heads = hw // HG_DK
    tt = _tile(seq, 256)
    assert tt % HG_CHUNK == 0
    nt = seq // tt
    nlev = len(HG_LEVELS)
    tri, level = _hgrn_tables(tt)
    tri = jnp.asarray(tri, BF16)
    level = jnp.asarray(level, I32)
    vec = pl.BlockSpec((1, hw), lambda b, t: (0, 0))
    vmem = 2 * (_nbytes((tt, d), x.dtype) + _nbytes(w_hg_in.shape, BF16) + _nbytes((tt, hw), BF16)
                + _nbytes(tri.shape, BF16) + _nbytes(level.shape, I32)) \
        + _nbytes((tt, d), BF16) + 16 * _nbytes((tt, hw), F32) + nlev * _nbytes((tt, hw), BF16) \
        + 4 * _nbytes((tt, tt), F32) + heads * HG_DK * HG_DK * 4
    return pl.pallas_call(
        functools.partial(_hgrn_kernel, heads=heads),
        out_shape=jax.ShapeDtypeStruct((n, hw), BF16),
        grid=(bsz, nt),
        in_specs=[pl.BlockSpec((tt, d), lambda b, t: (b * nt + t, 0)),
                  pl.BlockSpec(w_hg_in.shape, lambda b, t: (0, 0)),
                  pl.BlockSpec(tri.shape, lambda b, t: (0, 0)),
                  pl.BlockSpec(level.shape, lambda b, t: (0, 0)), vec, vec, vec],
        out_specs=pl.BlockSpec((tt, hw), lambda b, t: (b * nt + t, 0)),
        scratch_shapes=[pltpu.VMEM((heads, HG_DK, HG_DK), F32), pltpu.VMEM((tt, 4 * hw), F32),
                        pltpu.VMEM((nlev, tt, hw), BF16),
                        pltpu.VMEM((tt, hw), BF16), pltpu.VMEM((tt, hw), BF16),
                        pltpu.VMEM((tt, hw), BF16), pltpu.VMEM((tt, hw), F32)],
        compiler_params=_params(("arbitrary", "arbitrary"), vmem),
    )(x, w_hg_in, tri, level, c1, c2, gain)


def _attn_kernel(x_ref, wq_ref, k_ref, v_ref, o_ref, *, heads):
    dh = XA_DH
    q = jnp.dot(x_ref[...].astype(BF16), wq_ref[...], preferred_element_type=F32)
    scale = dh ** -0.5
    for h in range(heads):
        ls = slice(h * dh, (h + 1) * dh)
        s = lax.dot_general(q[:, ls].astype(BF16), k_ref[:, ls], (((1,), (1,)), ((), ())),
                            preferred_element_type=F32) * scale
        e = jnp.exp(s - jnp.max(s, axis=-1, keepdims=True))
        p = e / jnp.sum(e, axis=-1, keepdims=True)
        o = jnp.dot(p.astype(BF16), v_ref[:, ls], preferred_element_type=F32)
        o_ref[:, ls] = o.astype(o_ref.dtype)


def _attn_branch(x, wq, kv, seq, n_mem):
    n, d = x.shape
    xw = wq.shape[1]
    heads = xw // XA_DH
    tm = _tile(seq, 512)
    per_seq = seq // tm
    vmem = 2 * (_nbytes((tm, d), x.dtype) + _nbytes(wq.shape, BF16) + 2 * _nbytes((n_mem, xw), BF16)
                + _nbytes((tm, xw), BF16)) + 4 * _nbytes((tm, xw), F32) + 4 * _nbytes((tm, n_mem), F32)
    return pl.pallas_call(
        functools.partial(_attn_kernel, heads=heads),
        out_shape=jax.ShapeDtypeStruct((n, xw), BF16),
        grid=(n // tm,),
        in_specs=[pl.BlockSpec((tm, d), lambda i: (i, 0)),
                  pl.BlockSpec(wq.shape, lambda i: (0, 0)),
                  pl.BlockSpec((n_mem, xw), lambda i: (i // per_seq, 0)),
                  pl.BlockSpec((n_mem, xw), lambda i: (i // per_seq, 1))],
        out_specs=pl.BlockSpec((tm, xw), lambda i: (i, 0)),
        compiler_params=_params(("arbitrary",), vmem),
    )(x, wq, kv, kv)


def _merge_kernel(x_ref, ya_ref, yb_ref, yx_ref, wga_ref, wgb_ref, wgx_ref, wa_ref, wb_ref, wx_ref,
                  o_ref, xb_ref):
    @pl.when(pl.program_id(1) == 0)
    def _():
        xb_ref[...] = x_ref[...].astype(BF16)

    xb = xb_ref[...]
    dot = functools.partial(jnp.dot, preferred_element_type=F32)
    m = _sigmoid(dot(xb, wga_ref[...])) * dot(ya_ref[...], wa_ref[...])
    m = m + _sigmoid(dot(xb, wgb_ref[...])) * dot(yb_ref[...], wb_ref[...])
    m = m + _sigmoid(dot(xb, wgx_ref[...])) * dot(yx_ref[...], wx_ref[...])
    o_ref[...] = m.astype(o_ref.dtype)


def _merge(x, ya, yb, yx, w_gates, wa, wb, wx):
    n, d = x.shape
    tm, tn = _tile(n, 1024), _tile(d, 512)
    nj = d // tn
    row = lambda a: pl.BlockSpec((tm, a.shape[1]), lambda i, j: (i, 0))
    gate = lambda g: pl.BlockSpec((d, tn), lambda i, j, g=g: (0, g * nj + j))
    outw = lambda a: pl.BlockSpec((a.shape[0], tn), lambda i, j: (0, j))
    vmem = 2 * (_nbytes((tm, d), x.dtype) + _nbytes((tm, ya.shape[1] + yb.shape[1] + yx.shape[1]), BF16)
                + 3 * _nbytes((d, tn), BF16) + _nbytes((wa.shape[0] + wb.shape[0] + wx.shape[0], tn), BF16)
                + _nbytes((tm, tn), BF16)) + _nbytes((tm, d), BF16) + 8 * _nbytes((tm, tn), F32)
    return pl.pallas_call(
        _merge_kernel,
        out_shape=jax.ShapeDtypeStruct((n, d), BF16),
        grid=(n // tm, nj),
        in_specs=[row(x), row(ya), row(yb), row(yx), gate(0), gate(1), gate(2),
                  outw(wa), outw(wb), outw(wx)],
        out_specs=pl.BlockSpec((tm, tn), lambda i, j: (i, j)),
        scratch_shapes=[pltpu.VMEM((tm, d), BF16)],
        compiler_params=_params(("arbitrary", "arbitrary"), vmem),
    )(x, ya, yb, yx, w_gates, w_gates, w_gates, wa, wb, wx)


def _oproj_ln_kernel(m_ref, w_ref, x_ref, g_ref, b_ref, rw_ref, rb_ref,
                     o_ref, p_ref, eid_ref, rnk_ref, gate_ref, cnt_ref, carry_ref, *, alpha):
    y = alpha * x_ref[...] + jnp.dot(m_ref[...], w_ref[...], preferred_element_type=F32)
    y = _layer_norm_rows(y, g_ref[...], b_ref[...])
    o_ref[...] = y
    _store_packed_rows(p_ref, y)
    _route(y, rw_ref, rb_ref, eid_ref, rnk_ref, gate_ref, cnt_ref, carry_ref)


def _oproj_ln_route(merged, w_o, x, g, b, rw_t, rb, alpha):
    n, d = x.shape
    assert d % PAIR == 0
    pieces = d // PAIR
    ne = rw_t.shape[0]
    tm = _tile(n, 512)
    cur = lambda i: (i, 0)
    vec = pl.BlockSpec((1, d), lambda i: (0, 0))
    topk = lambda dt: jax.ShapeDtypeStruct((TOP_K, n), dt)
    kspec = pl.BlockSpec((TOP_K, tm), lambda i: (0, i))
    vmem = 2 * (_nbytes((tm, d), BF16) + _nbytes((d, d), BF16) + 2 * _nbytes((tm, d), F32)
                + _nbytes((tm, d // 2), U32) + _nbytes((ne, d), F32)) + 4 * _nbytes((tm, d), F32) \
        + 3 * _nbytes((tm, tm), F32) + 24 * _nbytes((ne, tm), F32)
    return pl.pallas_call(
        functools.partial(_oproj_ln_kernel, alpha=alpha),
        out_shape=(jax.ShapeDtypeStruct((n, d), F32), jax.ShapeDtypeStruct((n * pieces, LANES), U32),
                   topk(I32), topk(I32), topk(F32), jax.ShapeDtypeStruct((ne, 128), I32)),
        grid=(n // tm,),
        in_specs=[pl.BlockSpec((tm, d), cur),
                  pl.BlockSpec((d, d), lambda i: (0, 0)),
                  pl.BlockSpec((tm, d), cur), vec, vec,
                  pl.BlockSpec((ne, d), lambda i: (0, 0)),
                  pl.BlockSpec((ne, 1), lambda i: (0, 0))],
        out_specs=(pl.BlockSpec((tm, d), cur),
                   pl.BlockSpec((tm * pieces, LANES), cur),
                   kspec, kspec, kspec, pl.BlockSpec((ne, 128), lambda i: (0, 0))),
        scratch_shapes=[pltpu.VMEM((ne, 1), F32)],
        compiler_params=_params(("arbitrary",), vmem),
    )(merged, w_o, x, g, b, rw_t, rb)


def _rank_desc(vals):
    rows = vals.shape[0]
    iota = lax.broadcasted_iota(I32, vals.shape, 0)
    rank = jnp.zeros(vals.shape, I32)
    for j in range(rows):
        rowv = vals[j:j + 1, :]
        tie = jnp.where(iota > j, 1, 0)
        rank = rank + jnp.where(rowv > vals, 1, jnp.where(rowv == vals, tie, 0))
    return rank


def _route(x, rw_ref, rb_ref, eid_ref, rnk_ref, gate_ref, cnt_ref, carry_ref):
    ne = rw_ref.shape[0]
    tm = x.shape[0]
    gsz = ne // N_GROUPS

    @pl.when(pl.program_id(0) == 0)
    def _():
        carry_ref[...] = jnp.zeros_like(carry_ref)

    logits = lax.dot_general(rw_ref[...], x, (((1,), (1,)), ((), ())),
                             preferred_element_type=F32, precision=lax.Precision.HIGHEST)
    s = _sigmoid(logits)
    sel = s + rb_ref[...]

    grp = sel.reshape(N_GROUPS, gsz, tm)
    sub_iota = lax.broadcasted_iota(I32, (N_GROUPS, gsz, tm), 1)
    m1 = jnp.max(grp, axis=1, keepdims=True)
    first = jnp.min(jnp.where(grp == m1, sub_iota, gsz), axis=1, keepdims=True)
    m2 = jnp.max(jnp.where(sub_iota == first, -jnp.inf, grp), axis=1, keepdims=True)
    gscore = (m1 + m2).reshape(N_GROUPS, tm)

    gsel = _rank_desc(gscore) < TOPK_GROUPS
    emask = jnp.broadcast_to(gsel.reshape(N_GROUPS, 1, tm), (N_GROUPS, gsz, tm)).reshape(ne, tm)
    masked = jnp.where(emask, sel, -jnp.inf)

    eiota = lax.broadcasted_iota(I32, (ne, tm), 0)
    chosen = (_rank_desc(masked) < TOP_K) & emask

    w = jnp.where(chosen, s, 0.0)
    gate = w / jnp.sum(w, axis=0, keepdims=True) * ROUTED_SCALE

    ch = jnp.where(chosen, 1.0, 0.0).astype(BF16)
    tr = lax.broadcasted_iota(I32, (tm, tm), 0)
    tc = lax.broadcasted_iota(I32, (tm, tm), 1)
    before = jnp.where(tr < tc, 1.0, 0.0).astype(BF16)
    rank_tok = (carry_ref[...] + jnp.dot(ch, before, preferred_element_type=F32)).astype(I32)
    carry_ref[...] = carry_ref[...] + jnp.sum(jnp.where(chosen, 1.0, 0.0), axis=1, keepdims=True)
    cnt_ref[...] = jnp.broadcast_to(carry_ref[...], cnt_ref.shape).astype(I32)

    er = lax.broadcasted_iota(I32, (ne, ne), 0)
    ec = lax.broadcasted_iota(I32, (ne, ne), 1)
    lower = jnp.where(ec < er, 1.0, 0.0).astype(BF16)
    slot = jnp.dot(lower, ch, preferred_element_type=F32).astype(I32)
    for k in range(TOP_K):
        pick = chosen & (slot == k)
        eid_ref[k:k + 1, :] = jnp.sum(jnp.where(pick, eiota, 0), axis=0, keepdims=True)
        rnk_ref[k:k + 1, :] = jnp.sum(jnp.where(pick, rank_tok, 0), axis=0, keepdims=True)
        gate_ref[k:k + 1, :] = jnp.sum(jnp.where(pick, gate, 0.0), axis=0, keepdims=True)


def _pad_pieces(max_pad):
    pieces = []
    p = 1
    while p <= max_pad:
        pieces.append(p)
        p *= 2
    return pieces[::-1]


def _dispatch_kernel(pad_ref, dest_hbm, xp_ref, zero_hbm, sg_ref, su_ref, sd_ref, xs_hbm, sh_ref,
                     idx_ref, sem_idx, sem_row, sem_pad, *, tq, ne, rp):
    i = pl.program_id(0)

    def row_copy(src_row, dst_row):
        return pltpu.make_async_copy(xp_ref.at[pl.ds(pl.multiple_of(src_row * rp, rp), rp)],
                                     xs_hbm.at[pl.ds(pl.multiple_of(dst_row * rp, rp), rp)], sem_row)

    def pad_copies(fn):
        def per_expert(e, carry):
            row = pad_ref[0, e]
            npad = pad_ref[1, e]
            for p in _pad_pieces(EXPERT_ROWS - 1):
                @pl.when((npad & p) != 0)
                def _(row=row, p=p):
                    fn(pltpu.make_async_copy(zero_hbm.at[pl.ds(0, p * rp)],
                                             xs_hbm.at[pl.ds(pl.multiple_of(row * rp, rp), p * rp)], sem_pad))
                row = row + (npad & p)
            return carry
        lax.fori_loop(0, ne, per_expert, 0)

    @pl.when(i == 0)
    def _():
        pad_copies(lambda cp: cp.start())

    idx_cp = pltpu.make_async_copy(dest_hbm.at[i], idx_ref, sem_idx)
    idx_cp.start()
    idx_cp.wait()

    def per_token(t, carry):
        for k in range(TOP_K):
            row_copy(t, idx_ref[t * TOP_K + k]).start(priority=k % 2)
        return carry

    lax.fori_loop(0, tq, per_token, 0)

    xb = _load_packed_rows(xp_ref, tq, sh_ref.shape[1])
    dot = functools.partial(jnp.dot, preferred_element_type=F32)
    hmid = (_silu(dot(xb, sg_ref[...])) * dot(xb, su_ref[...])).astype(BF16)
    sh_ref[...] = dot(hmid, sd_ref[...])

    def drain(t, carry):
        for _ in range(TOP_K * COPY_UNROLL):
            row_copy(0, 0).wait()
        return carry

    lax.fori_loop(0, tq // COPY_UNROLL, drain, 0)

    @pl.when(i == 0)
    def _():
        pad_copies(lambda cp: cp.wait())


def _dispatch(pad_info, dest_tok, xp, n, rows, sg, su, sd):
    rp = xp.shape[0] // n
    d, hid = sg.shape
    ne = pad_info.shape[1]
    tq = _tile(n, 512)
    steps = n // tq
    dest2 = dest_tok.reshape(steps, tq * TOP_K)
    zero = jnp.zeros((EXPERT_ROWS * rp, LANES), U32)
    any_spec = pl.BlockSpec(memory_space=pl.ANY)
    full = lambda a: pl.BlockSpec(a.shape, lambda i, pad: (0, 0))
    grid_spec = pltpu.PrefetchScalarGridSpec(
        num_scalar_prefetch=1, grid=(steps,),
        in_specs=[any_spec, pl.BlockSpec((tq * rp, LANES), lambda i, pad: (i, 0)), any_spec,
                  full(sg), full(su), full(sd)],
        out_specs=(any_spec, pl.BlockSpec((tq, d), lambda i, pad: (i, 0))),
        scratch_shapes=[pltpu.SMEM((tq * TOP_K,), I32), pltpu.SemaphoreType.DMA(()),
                        pltpu.SemaphoreType.DMA(()), pltpu.SemaphoreType.DMA(())])
    vmem = 2 * (_nbytes((tq * rp, LANES), U32) + 3 * _nbytes((d, hid), BF16) + _nbytes((tq, d), F32)) \
        + _nbytes((tq, d), BF16) + 3 * _nbytes((tq, hid), F32) + _nbytes((tq, d), F32)
    return pl.pallas_call(
        functools.partial(_dispatch_kernel, tq=tq, ne=ne, rp=rp),
        out_shape=(jax.ShapeDtypeStruct((rows * rp, LANES), U32), jax.ShapeDtypeStruct((n, d), F32)),
        grid_spec=grid_spec,
        compiler_params=_params(("arbitrary",), vmem),
    )(pad_info, dest2, xp, zero, sg, su, sd)


def _expert_kernel(plan_ref, nu_ref, xs_ref, wg_hbm, wu_hbm, wd_hbm, ys_ref,
                   sg_ref, su_ref, sd_ref, wgb_ref, wub_ref, wdb_ref, sem_ref, *, layer):
    b = pl.program_id(0)

    def fetch(e, slot):
        return (pltpu.make_async_copy(wg_hbm.at[layer, e], sg_ref.at[slot], sem_ref.at[slot]),
                pltpu.make_async_copy(wu_hbm.at[layer, e], su_ref.at[slot], sem_ref.at[slot]),
                pltpu.make_async_copy(wd_hbm.at[layer, e], sd_ref.at[slot], sem_ref.at[slot]))

    @pl.when(b < nu_ref[0])
    def _():
        e = plan_ref[0, b]
        nxt = plan_ref[2, b]
        slot = plan_ref[3, b]

        @pl.when(b == 0)
        def _():
            for cp in fetch(e, slot):
                cp.start()

        @pl.when(plan_ref[1, b] == 1)
        def _():
            @pl.when(nxt >= 0)
            def _():
                for cp in fetch(nxt, 1 - slot):
                    cp.start()

            for cp in fetch(e, slot):
                cp.wait()
            wgb_ref[...] = sg_ref[slot].astype(BF16)
            wub_ref[...] = su_ref[slot].astype(BF16)
            wdb_ref[...] = sd_ref[slot].astype(BF16)

        t = EXPERT_ROWS
        d = wgb_ref.shape[0]
        xb = _load_packed_rows(xs_ref, t, d)
        dot = functools.partial(jnp.dot, preferred_element_type=F32)
        hmid = (_silu(dot(xb, wgb_ref[...])) * dot(xb, wub_ref[...])).astype(BF16)
        _store_packed_rows(ys_ref, dot(hmid, wdb_ref[...]))


def _experts(plan, n_used, xs, wg, wu, wd, layer):
    _, ne, d, hid = wg.shape
    t = EXPERT_ROWS
    rp = d // PAIR
    rows = xs.shape[0] // rp
    nb = rows // t
    clamp = lambda b, plan, nu: (jnp.minimum(b, jnp.maximum(nu[0] - 1, 0)), 0)
    any_spec = pl.BlockSpec(memory_space=pl.ANY)
    grid_spec = pltpu.PrefetchScalarGridSpec(
        num_scalar_prefetch=2, grid=(nb,),
        in_specs=[pl.BlockSpec((t * rp, LANES), clamp), any_spec, any_spec, any_spec],
        out_specs=pl.BlockSpec((t * rp, LANES), clamp),
        scratch_shapes=[pltpu.VMEM((2, d, hid), F32), pltpu.VMEM((2, d, hid), F32), pltpu.VMEM((2, hid, d), F32),
                        pltpu.VMEM((d, hid), BF16), pltpu.VMEM((d, hid), BF16), pltpu.VMEM((hid, d), BF16),
                        pltpu.SemaphoreType.DMA((2,))])
    vmem = 2 * (3 * _nbytes((d, hid), F32) + 2 * _nbytes((t, d // 2), U32)) \
        + 3 * _nbytes((d, hid), BF16) + 3 * _nbytes((t, d), F32)
    return pl.pallas_call(
        functools.partial(_expert_kernel, layer=layer),
        out_shape=jax.ShapeDtypeStruct((rows * rp, LANES), U32),
        grid_spec=grid_spec,
        compiler_params=_params(("arbitrary",), vmem),
    )(plan, n_used, xs, wg, wu, wd)


def _combine_kernel(dest_ref, ys_hbm, x_ref, gate_ref, sh_ref, g_ref, b_ref, o_ref,
                    buf_ref, sem_ref, *, tc, alpha):
    i = pl.program_id(0)
    rp = x_ref.shape[1] // PAIR
    half = i % 2

    def row_copy(src_row, dst_row, h):
        return pltpu.make_async_copy(ys_hbm.at[pl.ds(pl.multiple_of(src_row * rp, rp), rp)],
                                     buf_ref.at[pl.ds(pl.multiple_of(dst_row * rp, rp), rp)], sem_ref.at[h])

    def fetch(step, h):
        def per_group(g, carry):
            src0 = (step * tc + g * COPY_UNROLL) * TOP_K
            dst0 = h * TOP_K * tc + g * COPY_UNROLL
            for u in range(COPY_UNROLL):
                for k in range(TOP_K):
                    row_copy(dest_ref[src0 + u * TOP_K + k], dst0 + k * tc + u, h).start(priority=k % 2)
            return carry
        lax.fori_loop(0, tc // COPY_UNROLL, per_group, 0)

    @pl.when(i == 0)
    def _():
        fetch(0, 0)

    @pl.when(i + 1 < pl.num_programs(0))
    def _():
        fetch(i + 1, 1 - half)

    x = x_ref[...]
    shared = sh_ref[...]

    def drain(g, carry):
        for _ in range(COPY_UNROLL * TOP_K):
            row_copy(0, 0, half).wait()
        return carry

    lax.fori_loop(0, tc // COPY_UNROLL, drain, 0)

    gate = gate_ref[...]
    base = half * (TOP_K * tc * rp)
    cols = []
    for s in range(rp):
        lo = hi = None
        for k in range(TOP_K):
            w = buf_ref[pl.ds(base + k * tc * rp + s, tc, stride=rp), :]
            gk = gate[:, k:k + 1]
            lo_k = lax.bitcast_convert_type(w << 16, F32) * gk
            hi_k = lax.bitcast_convert_type(w & jnp.uint32(0xFFFF0000), F32) * gk
            lo = lo_k if lo is None else lo + lo_k
            hi = hi_k if hi is None else hi + hi_k
        cols += [lo, hi]
    routed = jnp.concatenate(cols, axis=1)
    o_ref[...] = _layer_norm_rows(alpha * x + (shared + routed), g_ref[...], b_ref[...])


def _combine(dest_tok, ys, x, gate_tok, shared, g, b, alpha):
    n, d = x.shape
    tc = _tile(n, 128)
    steps = n // tc
    rp = d // PAIR
    vec = pl.BlockSpec((1, d), lambda i, dest: (0, 0))
    rows = pl.BlockSpec((tc, d), lambda i, dest: (i, 0))
    buf_rows = 2 * TOP_K * tc * rp
    vmem = _nbytes((buf_rows, LANES), U32) + 2 * 3 * _nbytes((tc, d), F32) + 6 * _nbytes((tc, d), F32)
    grid_spec = pltpu.PrefetchScalarGridSpec(
        num_scalar_prefetch=1, grid=(steps,),
        in_specs=[pl.BlockSpec(memory_space=pl.ANY), rows,
                  pl.BlockSpec((tc, TOP_K), lambda i, dest: (i, 0)), rows, vec, vec],
        out_specs=pl.BlockSpec((tc, d), lambda i, dest: (i, 0)),
        scratch_shapes=[pltpu.VMEM((buf_rows, LANES), U32), pltpu.SemaphoreType.DMA((2,))])
    return pl.pallas_call(
        functools.partial(_combine_kernel, tc=tc, alpha=alpha),
        out_shape=jax.ShapeDtypeStruct((n, d), F32),
        grid_spec=grid_spec,
        compiler_params=_params(("arbitrary",), vmem),
    )(dest_tok, ys, x, gate_tok, shared, g, b)


def _moe(x1, x1p, eid, rnk, gate, cnt, wg, wu, wd, layer, sg, su, sd, g, b, alpha):
    n, d = x1.shape
    ne = cnt.shape[0]
    counts = cnt[:, 0]
    padded = (counts + EXPERT_ROWS - 1) // EXPERT_ROWS * EXPERT_ROWS
    eidx = jnp.arange(ne, dtype=I32)
    pad_end = jnp.sum(jnp.where(eidx[None, :] <= eidx[:, None], padded[None, :], 0), axis=1)
    pad_start = pad_end - padded
    n_blocks = (n * TOP_K + ne * (EXPERT_ROWS - 1)) // EXPERT_ROWS
    rows = n_blocks * EXPERT_ROWS
    start_of = jnp.sum(jnp.where(eid[:, :, None] == eidx, pad_start, 0), axis=-1)
    dest_tok = (start_of + rnk).T.reshape(-1).astype(I32)
    pad_info = jnp.stack([pad_start + counts, padded - counts]).astype(I32)
    total = pad_end[ne - 1]
    n_used = (total // EXPERT_ROWS).astype(I32).reshape(1)
    blk_start = jnp.minimum(jnp.arange(n_blocks, dtype=I32) * EXPERT_ROWS, total - 1)
    blk_exp = jnp.sum((pad_end[None, :] <= blk_start[:, None]).astype(I32), axis=1)
    blk_exp = jnp.minimum(blk_exp, ne - 1).astype(I32)
    has = counts > 0
    later = jnp.where((eidx[None, :] > eidx[:, None]) & has[None, :], eidx[None, :], ne)
    next_e = jnp.min(later, axis=1)
    next_e = jnp.where(next_e == ne, -1, next_e)
    ordinal = jnp.sum(jnp.where((eidx[None, :] <= eidx[:, None]) & has[None, :], 1, 0), axis=1) - 1
    first = jnp.concatenate([jnp.ones((1,), I32), (blk_exp[1:] != blk_exp[:-1]).astype(I32)])
    plan = jnp.stack([blk_exp, first, next_e[blk_exp], ordinal[blk_exp] % 2]).astype(I32)
    xs, shared = _dispatch(pad_info, dest_tok, x1p, n, rows, sg, su, sd)
    ys = _experts(plan, n_used, xs, wg, wu, wd, layer)
    return _combine(dest_tok, ys, x1, gate.T, shared, g, b, alpha)


def kernel(x, mem, w_in, conv_w, w_conv_out, hg_lb_logits, hg_norm_g, w_hg_out, w_mem_k, w_mem_v,
           w_xa_out, w_o, ln1_g, ln1_b, router_w, router_b, exp_wg, exp_wu, exp_wd,
           sh_wg, sh_wu, sh_wd, ln2_g, ln2_b):
    bsz, seq, d = x.shape
    depth = w_in.shape[0]
    n_mem = mem.shape[1]
    cwid = conv_w.shape[2]
    hw = w_hg_out.shape[1]
    xw = w_mem_k.shape[2]
    alpha = float((2 * depth) ** 0.25)
    n = bsz * seq

    lb_all = jnp.cumsum(jax.nn.softmax(hg_lb_logits.astype(F32), axis=0), axis=0)
    lb_all = lb_all - lb_all[0:1]
    o_conv, o_hg, o_xa, o_gate = 0, 3 * cwid, 3 * cwid + 4 * hw, 3 * cwid + 4 * hw + xw

    h = x.reshape(n, d)
    memf = mem.reshape(bsz * n_mem, d)
    for l in range(depth):
        wl = w_in[l]
        w_conv_in = wl[:, o_conv:o_hg].astype(BF16)
        w_hg_in = wl[:, o_hg:o_xa].astype(BF16)
        w_xa_in = wl[:, o_xa:o_gate].astype(BF16)
        w_gates = wl[:, o_gate:].astype(BF16)
        vec = lambda a: a.astype(F32).reshape(1, -1)

        ya = _conv_branch(h, w_conv_in, conv_w[l].astype(F32), seq)
        lb = lb_all[l]
        yb = _hgrn_branch(h, w_hg_in, vec(jnp.log(lb)), vec(jnp.log1p(-lb)), vec(hg_norm_g[l]), bsz, seq)
        w_kv = jnp.concatenate([w_mem_k[l], w_mem_v[l]], axis=1).astype(BF16)
        kv = _matmul(memf, w_kv, BF16, 1024, xw)
        yx = _attn_branch(h, w_xa_in, kv, seq, n_mem)
        merged = _merge(h, ya, yb, yx, w_gates, w_conv_out[l].astype(BF16), w_hg_out[l].astype(BF16),
                        w_xa_out[l].astype(BF16))
        x1, x1p, eid, rnk, gate, cnt = _oproj_ln_route(
            merged, w_o[l].astype(BF16), h, vec(ln1_g[l]), vec(ln1_b[l]),
            router_w[l].astype(F32).T, router_b[l].astype(F32).reshape(-1, 1), alpha)
        h = _moe(x1, x1p, eid, rnk, gate, cnt, exp_wg, exp_wu, exp_wd, l,
                    sh_wg[l].astype(BF16), sh_wu[l].astype(BF16), sh_wd[l].astype(BF16),
                    vec(ln2_g[l]), vec(ln2_b[l]), alpha)
    return h.reshape(bsz, seq, d)
```

```python
import functools

import jax
import jax.numpy as jnp
import numpy as np
from jax import lax
from jax.experimental import pallas as pl
from jax.experimental.pallas import tpu as pltpu

F32 = jnp.float32
BF16 = jnp.bfloat16
I32 = jnp.int32
U32 = jnp.uint32

HG_DK = 128
XA_DH = 128
TOP_K = 8
N_GROUPS = 8
TOPK_GROUPS = 4
ROUTED_SCALE = 2.5
LN_EPS = 1e-5
RMS_EPS = 1e-6
HG_CHUNK = 64
HG_LEVELS = (1, 2, 4, 8, 16, 32)
EXPERT_ROWS = 512
COPY_UNROLL = 8

PROJ_ROWS = 512
HGRN_ROWS = 256
MERGE_ROWS, MERGE_COLS = 1024, 512
KV_ROWS = 1024
DISPATCH_ROWS = 512
COMBINE_ROWS = 256

V7X_VMEM_BYTES = 64 * 1024 * 1024
VMEM_CAP = V7X_VMEM_BYTES - 8 * 1024 * 1024


def _params(semantics, vmem_bytes):
    limit = int(min(VMEM_CAP, max(32 * 1024 * 1024, vmem_bytes * 5 // 4 + (4 << 20))))
    return pltpu.CompilerParams(dimension_semantics=semantics, vmem_limit_bytes=limit)


def _nbytes(shape, dtype):
    n = 1
    for s in shape:
        n *= s
    return n * jnp.dtype(dtype).itemsize


def _tile(n, want):
    t = min(n, want)
    while n % t:
        t //= 2
    return t


def _silu(x):
    return x * (1.0 / (1.0 + jnp.exp(-x)))


def _sigmoid(x):
    return 1.0 / (1.0 + jnp.exp(-x))


def _layer_norm_rows(y, g, b):
    mu = jnp.mean(y, axis=-1, keepdims=True)
    yc = y - mu
    var = jnp.mean(yc * yc, axis=-1, keepdims=True)
    return yc * lax.rsqrt(var + LN_EPS) * g + b


LANES = 128
PAIR = 2 * LANES


def _store_packed_rows(p_ref, y):
    rows, d = y.shape
    pieces = d // PAIR
    for s in range(pieces):
        lo = lax.bitcast_convert_type(y[:, s * PAIR:s * PAIR + LANES].astype(BF16).astype(F32), U32)
        hi = lax.bitcast_convert_type(y[:, s * PAIR + LANES:(s + 1) * PAIR].astype(BF16).astype(F32), U32)
        p_ref[pl.ds(s, rows, stride=pieces), :] = (hi & jnp.uint32(0xFFFF0000)) | (lo >> 16)


def _load_packed_rows(p_ref, rows, d):
    pieces = d // PAIR
    cols = []
    for s in range(pieces):
        w = p_ref[pl.ds(s, rows, stride=pieces), :]
        cols.append(lax.bitcast_convert_type(w << 16, F32).astype(BF16))
        cols.append(lax.bitcast_convert_type(w & jnp.uint32(0xFFFF0000), F32).astype(BF16))
    return jnp.concatenate(cols, axis=1)


def _matmul_kernel(x_ref, w_ref, o_ref):
    o_ref[...] = jnp.dot(x_ref[...].astype(BF16), w_ref[...],
                         preferred_element_type=F32).astype(o_ref.dtype)


def _matmul(x, w, out_dtype, tm, tn):
    m, k = x.shape
    n = w.shape[1]
    tm, tn = _tile(m, tm), _tile(n, tn)
    vmem = 2 * (_nbytes((tm, k), x.dtype) + _nbytes((k, tn), w.dtype) + _nbytes((tm, tn), out_dtype)) \
        + _nbytes((tm, k), BF16) + _nbytes((tm, tn), F32)
    return pl.pallas_call(
        _matmul_kernel,
        out_shape=jax.ShapeDtypeStruct((m, n), out_dtype),
        grid=(m // tm, n // tn),
        in_specs=[pl.BlockSpec((tm, k), lambda i, j: (i, 0)),
                  pl.BlockSpec((k, tn), lambda i, j: (0, j))],
        out_specs=pl.BlockSpec((tm, tn), lambda i, j: (i, j)),
        compiler_params=_params(("arbitrary", "arbitrary"), vmem),
    )(x, w)


def _conv_kernel(x_ref, w_ref, cw_ref, o_ref, carry_ref, *, tiles_per_seq):
    i = pl.program_id(0)
    tm, cwid = o_ref.shape

    @pl.when(i % tiles_per_seq == 0)
    def _():
        carry_ref[...] = jnp.zeros_like(carry_ref)

    p = jnp.dot(x_ref[...].astype(BF16), w_ref[...], preferred_element_type=F32)
    u = p[:, :cwid]
    b = p[:, cwid:2 * cwid]
    c = p[:, 2 * cwid:]
    cu = c * u
    prev = carry_ref[...]
    row = lax.broadcasted_iota(I32, (tm, cwid), 0)
    cu1 = jnp.where(row == 0, prev[7:8, :], pltpu.roll(cu, 1, 0))
    cu2 = pltpu.roll(cu, 2, 0)
    cu2 = jnp.where(row == 0, prev[6:7, :], jnp.where(row == 1, prev[7:8, :], cu2))
    cw = cw_ref[...]
    y = b * (cw[0:1, :] * cu2 + cw[1:2, :] * cu1 + cw[2:3, :] * cu)
    o_ref[...] = y.astype(o_ref.dtype)
    carry_ref[...] = cu[tm - 8:, :]


def _conv_branch(x, w_conv_in, conv_w, seq):
    n, d = x.shape
    cwid = conv_w.shape[1]
    assert conv_w.shape[0] == 3
    tm = _tile(seq, PROJ_ROWS)
    vmem = 2 * (_nbytes((tm, d), x.dtype) + _nbytes(w_conv_in.shape, BF16) + _nbytes((tm, cwid), BF16)) \
        + 6 * _nbytes((tm, 3 * cwid), F32)
    return pl.pallas_call(
        functools.partial(_conv_kernel, tiles_per_seq=seq // tm),
        out_shape=jax.ShapeDtypeStruct((n, cwid), BF16),
        grid=(n // tm,),
        in_specs=[pl.BlockSpec((tm, d), lambda i: (i, 0)),
                  pl.BlockSpec(w_conv_in.shape, lambda i: (0, 0)),
                  pl.BlockSpec(conv_w.shape, lambda i: (0, 0))],
        out_specs=pl.BlockSpec((tm, cwid), lambda i: (i, 0)),
        scratch_shapes=[pltpu.VMEM((8, cwid), F32)],
        compiler_params=_params(("arbitrary",), vmem),
    )(x, w_conv_in, conv_w)


def _head_sum(x):
    return jnp.sum(x, axis=-1, keepdims=True)


def _hgrn_tables(tt):
    t = np.arange(tt)[:, None]
    j = np.arange(tt)[None, :]
    tri = ((t // HG_CHUNK) == (j // HG_CHUNK)) & (j <= t)
    level = np.zeros((tt, tt), np.int32)
    for i, h in enumerate(HG_LEVELS):
        level[((t // (2 * h)) == (j // (2 * h))) & ((t % (2 * h)) >= h) & ((j % (2 * h)) < h)] = i + 1
    return tri.astype(np.float32), level


def _hgrn_kernel(x_ref, w_ref, tri_ref, lvl_ref, c1_ref, c2_ref, gain_ref, o_ref,
                 st_ref, p_ref, u_ref, qd_ref, kd_ref, vb_ref, dec_ref, *, heads):
    tt, hw = o_ref.shape
    dk = HG_DK
    ch = HG_CHUNK
    nlev = len(HG_LEVELS)

    @pl.when(pl.program_id(1) == 0)
    def _():
        st_ref[...] = jnp.zeros_like(st_ref)

    p_ref[...] = jnp.dot(x_ref[...].astype(BF16), w_ref[...], preferred_element_type=F32)
    q_ref, f_ref, v_ref, g_ref = (p_ref.at[:, pl.ds(j * hw, hw)] for j in range(4))

    f = f_ref[...]
    log_sig = jnp.minimum(f, 0.0) - jnp.log(1.0 + jnp.exp(-jnp.abs(f)))
    a1 = c1_ref[...]
    a2 = c2_ref[...] + log_sig
    log_f = jnp.maximum(a1, a2) + jnp.log(1.0 + jnp.exp(-jnp.abs(a1 - a2)))
    kin = 1.0 - jnp.exp(log_f)

    t_hi = log_f.astype(BF16)
    rem = log_f - t_hi.astype(F32)
    t_mid = rem.astype(BF16)
    t_lo = (rem - t_mid.astype(F32)).astype(BF16)

    tri = tri_ref[...]
    dot = functools.partial(jnp.dot, preferred_element_type=F32)
    bcum = dot(tri, t_hi) + dot(tri, t_mid) + dot(tri, t_lo)

    def group_row(a, size, idx):
        a3 = a.reshape(tt // size, size, hw)
        return jnp.broadcast_to(a3[:, idx:idx + 1, :], a3.shape).reshape(tt, hw)

    blast = group_row(bcum, ch, ch - 1)
    row = lax.broadcasted_iota(I32, (tt, hw), 0)

    def level_decay(h):
        upper = (row % (2 * h)) >= h
        if h == 1:
            return upper, jnp.where(upper, 1.0 - kin, 1.0)
        if h == 2:
            pos = row % 4
            nxt = pltpu.roll(log_f, tt - 1, 0)
            prv = pltpu.roll(log_f, 1, 0)
            z = jnp.where(pos == 0, nxt, jnp.where(pos == 1, 0.0, jnp.where(pos == 2, log_f, log_f + prv)))
            return upper, jnp.exp(z)
        refb = group_row(bcum, 2 * h, h - 1)
        return upper, jnp.exp(jnp.where(upper, bcum - refb, refb - bcum))

    q = q_ref[...]
    qd_ref[...] = (q * jnp.exp(bcum)).astype(BF16)
    kd_ref[...] = (kin * jnp.exp(blast - bcum)).astype(BF16)
    vb_ref[...] = v_ref[...].astype(BF16)
    dec_ref[...] = jnp.exp(blast)

    for i, h in enumerate(HG_LEVELS):
        upper, decay = level_decay(h)
        u_ref[i] = (jnp.where(upper, q, kin) * decay).astype(BF16)

    lvl = lvl_ref[...]
    gain = gain_ref[...]
    nt_dims = (((1,), (1,)), ((), ()))
    for h in range(heads):
        ls = slice(h * dk, (h + 1) * dk)
        scores = jnp.zeros((tt, tt), F32)
        for i in range(nlev):
            u = u_ref[i, :, ls]
            gram = lax.dot_general(u, u, nt_dims, preferred_element_type=F32)
            scores = scores + jnp.where(lvl == i + 1, gram, 0.0)
        acc = jnp.dot(scores.astype(BF16), vb_ref[:, ls], preferred_element_type=F32)
        acc = acc + _head_sum(q[:, ls] * kin[:, ls]) * v_ref[:, ls]
        inter = []
        for c in range(tt // ch):
            rs = slice(c * ch, (c + 1) * ch)
            st = st_ref[h]
            inter.append(lax.dot_general(qd_ref[rs, ls], st.astype(BF16), nt_dims, preferred_element_type=F32))
            upd = lax.dot_general(vb_ref[rs, ls], kd_ref[rs, ls], (((0,), (0,)), ((), ())),
                                  preferred_element_type=F32)
            st_ref[h] = st * dec_ref[c * ch:c * ch + 1, ls] + upd
        acc = acc + jnp.concatenate(inter, axis=0)
        ms = _head_sum(acc * acc) * (1.0 / dk)
        y = acc * lax.rsqrt(ms + RMS_EPS) * gain[:, ls] * _silu(g_ref[:, ls])
        o_ref[:, ls] = y.astype(o_ref.dtype)


def _hgrn_branch(x, w_hg_in, c1, c2, gain, bsz, seq):
    n, d = x.shape
    hw = w_hg_in.shape[1] // 4
    heads = hw // HG_DK
    tt = _tile(seq, HGRN_ROWS)
    assert tt % HG_CHUNK == 0
    nt = seq // tt
    nlev = len(HG_LEVELS)
    tri, level = _hgrn_tables(tt)
    tri = jnp.asarray(tri, BF16)
    level = jnp.asarray(level, I32)
    vec = pl.BlockSpec((1, hw), lambda b, t: (0, 0))
    vmem = 2 * (_nbytes((tt, d), x.dtype) + _nbytes(w_hg_in.shape, BF16) + _nbytes((tt, hw), BF16)
                + _nbytes(tri.shape, BF16) + _nbytes(level.shape, I32)) \
        + _nbytes((tt, d), BF16) + 16 * _nbytes((tt, hw), F32) + nlev * _nbytes((tt, hw), BF16) \
        + 4 * _nbytes((tt, tt), F32) + heads * HG_DK * HG_DK * 4
    return pl.pallas_call(
        functools.partial(_hgrn_kernel, heads=heads),
        out_shape=jax.ShapeDtypeStruct((n, hw), BF16),
        grid=(bsz, nt),
        in_specs=[pl.BlockSpec((tt, d), lambda b, t: (b * nt + t, 0)),
                  pl.BlockSpec(w_hg_in.shape, lambda b, t: (0, 0)),
                  pl.BlockSpec(tri.shape, lambda b, t: (0, 0)),
                  pl.BlockSpec(level.shape, lambda b, t: (0, 0)), vec, vec, vec],
        out_specs=pl.BlockSpec((tt, hw), lambda b, t: (b * nt + t, 0)),
        scratch_shapes=[pltpu.VMEM((heads, HG_DK, HG_DK), F32), pltpu.VMEM((tt, 4 * hw), F32),
                        pltpu.VMEM((nlev, tt, hw), BF16),
                        pltpu.VMEM((tt, hw), BF16), pltpu.VMEM((tt, hw), BF16),
                        pltpu.VMEM((tt, hw), BF16), pltpu.VMEM((tt, hw), F32)],
        compiler_params=_params(("arbitrary", "arbitrary"), vmem),
    )(x, w_hg_in, tri, level, c1, c2, gain)


def _attn_kernel(x_ref, wq_ref, k_ref, v_ref, o_ref, *, heads):
    dh = XA_DH
    q = jnp.dot(x_ref[...].astype(BF16), wq_ref[...], preferred_element_type=F32)
    scale = dh ** -0.5
    for h in range(heads):
        ls = slice(h * dh, (h + 1) * dh)
        s = lax.dot_general(q[:, ls].astype(BF16), k_ref[:, ls], (((1,), (1,)), ((), ())),
                            preferred_element_type=F32) * scale
        e = jnp.exp(s - jnp.max(s, axis=-1, keepdims=True))
        p = e / jnp.sum(e, axis=-1, keepdims=True)
        o = jnp.dot(p.astype(BF16), v_ref[:, ls], preferred_element_type=F32)
        o_ref[:, ls] = o.astype(o_ref.dtype)


def _attn_branch(x, wq, kv, seq, n_mem):
    n, d = x.shape
    xw = wq.shape[1]
    heads = xw // XA_DH
    tm = _tile(seq, PROJ_ROWS)
    per_seq = seq // tm
    vmem = 2 * (_nbytes((tm, d), x.dtype) + _nbytes(wq.shape, BF16) + 2 * _nbytes((n_mem, xw), BF16)
                + _nbytes((tm, xw), BF16)) + 4 * _nbytes((tm, xw), F32) + 4 * _nbytes((tm, n_mem), F32)
    return pl.pallas_call(
        functools.partial(_attn_kernel, heads=heads),
        out_shape=jax.ShapeDtypeStruct((n, xw), BF16),
        grid=(n // tm,),
        in_specs=[pl.BlockSpec((tm, d), lambda i: (i, 0)),
                  pl.BlockSpec(wq.shape, lambda i: (0, 0)),
                  pl.BlockSpec((n_mem, xw), lambda i: (i // per_seq, 0)),
                  pl.BlockSpec((n_mem, xw), lambda i: (i // per_seq, 1))],
        out_specs=pl.BlockSpec((tm, xw), lambda i: (i, 0)),
        compiler_params=_params(("arbitrary",), vmem),
    )(x, wq, kv, kv)


def _merge_kernel(x_ref, ya_ref, yb_ref, yx_ref, wga_ref, wgb_ref, wgx_ref, wa_ref, wb_ref, wx_ref,
                  o_ref, xb_ref):
    @pl.when(pl.program_id(1) == 0)
    def _():
        xb_ref[...] = x_ref[...].astype(BF16)

    xb = xb_ref[...]
    dot = functools.partial(jnp.dot, preferred_element_type=F32)
    m = _sigmoid(dot(xb, wga_ref[...])) * dot(ya_ref[...], wa_ref[...])
    m = m + _sigmoid(dot(xb, wgb_ref[...])) * dot(yb_ref[...], wb_ref[...])
    m = m + _sigmoid(dot(xb, wgx_ref[...])) * dot(yx_ref[...], wx_ref[...])
    o_ref[...] = m.astype(o_ref.dtype)


def _merge(x, ya, yb, yx, w_gates, wa, wb, wx):
    n, d = x.shape
    tm, tn = _tile(n, MERGE_ROWS), _tile(d, MERGE_COLS)
    nj = d // tn
    row = lambda a: pl.BlockSpec((tm, a.shape[1]), lambda i, j: (i, 0))
    gate = lambda g: pl.BlockSpec((d, tn), lambda i, j, g=g: (0, g * nj + j))
    outw = lambda a: pl.BlockSpec((a.shape[0], tn), lambda i, j: (0, j))
    vmem = 2 * (_nbytes((tm, d), x.dtype) + _nbytes((tm, ya.shape[1] + yb.shape[1] + yx.shape[1]), BF16)
                + 3 * _nbytes((d, tn), BF16) + _nbytes((wa.shape[0] + wb.shape[0] + wx.shape[0], tn), BF16)
                + _nbytes((tm, tn), BF16)) + _nbytes((tm, d), BF16) + 8 * _nbytes((tm, tn), F32)
    return pl.pallas_call(
        _merge_kernel,
        out_shape=jax.ShapeDtypeStruct((n, d), BF16),
        grid=(n // tm, nj),
        in_specs=[row(x), row(ya), row(yb), row(yx), gate(0), gate(1), gate(2),
                  outw(wa), outw(wb), outw(wx)],
        out_specs=pl.BlockSpec((tm, tn), lambda i, j: (i, j)),
        scratch_shapes=[pltpu.VMEM((tm, d), BF16)],
        compiler_params=_params(("arbitrary", "arbitrary"), vmem),
    )(x, ya, yb, yx, w_gates, w_gates, w_gates, wa, wb, wx)


def _oproj_ln_kernel(m_ref, w_ref, x_ref, g_ref, b_ref, rw_ref, rb_ref,
                     o_ref, p_ref, eid_ref, rnk_ref, gate_ref, cnt_ref, carry_ref, *, alpha):
    y = alpha * x_ref[...] + jnp.dot(m_ref[...], w_ref[...], preferred_element_type=F32)
    y = _layer_norm_rows(y, g_ref[...], b_ref[...])
    o_ref[...] = y
    _store_packed_rows(p_ref, y)
    _route(y, rw_ref, rb_ref, eid_ref, rnk_ref, gate_ref, cnt_ref, carry_ref)


def _oproj_ln_route(merged, w_o, x, g, b, rw_t, rb, alpha):
    n, d = x.shape
    assert d % PAIR == 0
    pieces = d // PAIR
    ne = rw_t.shape[0]
    tm = _tile(n, PROJ_ROWS)
    cur = lambda i: (i, 0)
    vec = pl.BlockSpec((1, d), lambda i: (0, 0))
    topk = lambda dt: jax.ShapeDtypeStruct((TOP_K, n), dt)
    kspec = pl.BlockSpec((TOP_K, tm), lambda i: (0, i))
    vmem = 2 * (_nbytes((tm, d), BF16) + _nbytes((d, d), BF16) + 2 * _nbytes((tm, d), F32)
                + _nbytes((tm, d // 2), U32) + _nbytes((ne, d), F32)) + 4 * _nbytes((tm, d), F32) \
        + 3 * _nbytes((tm, tm), F32) + 24 * _nbytes((ne, tm), F32)
    return pl.pallas_call(
        functools.partial(_oproj_ln_kernel, alpha=alpha),
        out_shape=(jax.ShapeDtypeStruct((n, d), F32), jax.ShapeDtypeStruct((n * pieces, LANES), U32),
                   topk(I32), topk(I32), topk(F32), jax.ShapeDtypeStruct((ne, 128), I32)),
        grid=(n // tm,),
        in_specs=[pl.BlockSpec((tm, d), cur),
                  pl.BlockSpec((d, d), lambda i: (0, 0)),
                  pl.BlockSpec((tm, d), cur), vec, vec,
                  pl.BlockSpec((ne, d), lambda i: (0, 0)),
                  pl.BlockSpec((ne, 1), lambda i: (0, 0))],
        out_specs=(pl.BlockSpec((tm, d), cur),
                   pl.BlockSpec((tm * pieces, LANES), cur),
                   kspec, kspec, kspec, pl.BlockSpec((ne, 128), lambda i: (0, 0))),
        scratch_shapes=[pltpu.VMEM((ne, 1), F32)],
        compiler_params=_params(("arbitrary",), vmem),
    )(merged, w_o, x, g, b, rw_t, rb)


def _rank_desc(vals):
    rows = vals.shape[0]
    iota = lax.broadcasted_iota(I32, vals.shape, 0)
    rank = jnp.zeros(vals.shape, I32)
    for j in range(rows):
        rowv = vals[j:j + 1, :]
        tie = jnp.where(iota > j, 1, 0)
        rank = rank + jnp.where(rowv > vals, 1, jnp.where(rowv == vals, tie, 0))
    return rank


def _route(x, rw_ref, rb_ref, eid_ref, rnk_ref, gate_ref, cnt_ref, carry_ref):
    ne = rw_ref.shape[0]
    tm = x.shape[0]
    gsz = ne // N_GROUPS

    @pl.when(pl.program_id(0) == 0)
    def _():
        carry_ref[...] = jnp.zeros_like(carry_ref)

    logits = lax.dot_general(rw_ref[...], x, (((1,), (1,)), ((), ())),
                             preferred_element_type=F32, precision=lax.Precision.HIGHEST)
    s = _sigmoid(logits)
    sel = s + rb_ref[...]

    grp = sel.reshape(N_GROUPS, gsz, tm)
    sub_iota = lax.broadcasted_iota(I32, (N_GROUPS, gsz, tm), 1)
    m1 = jnp.max(grp, axis=1, keepdims=True)
    first = jnp.min(jnp.where(grp == m1, sub_iota, gsz), axis=1, keepdims=True)
    m2 = jnp.max(jnp.where(sub_iota == first, -jnp.inf, grp), axis=1, keepdims=True)
    gscore = (m1 + m2).reshape(N_GROUPS, tm)

    gsel = _rank_desc(gscore) < TOPK_GROUPS
    emask = jnp.broadcast_to(gsel.reshape(N_GROUPS, 1, tm), (N_GROUPS, gsz, tm)).reshape(ne, tm)
    masked = jnp.where(emask, sel, -jnp.inf)

    eiota = lax.broadcasted_iota(I32, (ne, tm), 0)
    chosen = (_rank_desc(masked) < TOP_K) & emask

    w = jnp.where(chosen, s, 0.0)
    gate = w / jnp.sum(w, axis=0, keepdims=True) * ROUTED_SCALE

    ch = jnp.where(chosen, 1.0, 0.0).astype(BF16)
    tr = lax.broadcasted_iota(I32, (tm, tm), 0)
    tc = lax.broadcasted_iota(I32, (tm, tm), 1)
    before = jnp.where(tr < tc, 1.0, 0.0).astype(BF16)
    rank_tok = (carry_ref[...] + jnp.dot(ch, before, preferred_element_type=F32)).astype(I32)
    carry_ref[...] = carry_ref[...] + jnp.sum(jnp.where(chosen, 1.0, 0.0), axis=1, keepdims=True)
    cnt_ref[...] = jnp.broadcast_to(carry_ref[...], cnt_ref.shape).astype(I32)

    er = lax.broadcasted_iota(I32, (ne, ne), 0)
    ec = lax.broadcasted_iota(I32, (ne, ne), 1)
    lower = jnp.where(ec < er, 1.0, 0.0).astype(BF16)
    slot = jnp.dot(lower, ch, preferred_element_type=F32).astype(I32)
    for k in range(TOP_K):
        pick = chosen & (slot == k)
        eid_ref[k:k + 1, :] = jnp.sum(jnp.where(pick, eiota, 0), axis=0, keepdims=True)
        rnk_ref[k:k + 1, :] = jnp.sum(jnp.where(pick, rank_tok, 0), axis=0, keepdims=True)
        gate_ref[k:k + 1, :] = jnp.sum(jnp.where(pick, gate, 0.0), axis=0, keepdims=True)


def _pad_pieces(max_pad):
    pieces = []
    p = 1
    while p <= max_pad:
        pieces.append(p)
        p *= 2
    return pieces[::-1]


def _dispatch_kernel(pad_ref, dest_hbm, xp_ref, zero_hbm, sg_ref, su_ref, sd_ref, xs_hbm, sh_ref,
                     idx_ref, sem_idx, sem_row, sem_pad, *, tq, ne, rp):
    i = pl.program_id(0)

    def row_copy(src_row, dst_row):
        return pltpu.make_async_copy(xp_ref.at[pl.ds(pl.multiple_of(src_row * rp, rp), rp)],
                                     xs_hbm.at[pl.ds(pl.multiple_of(dst_row * rp, rp), rp)], sem_row)

    def pad_copies(fn):
        def per_expert(e, carry):
            row = pad_ref[0, e]
            npad = pad_ref[1, e]
            for p in _pad_pieces(EXPERT_ROWS - 1):
                @pl.when((npad & p) != 0)
                def _(row=row, p=p):
                    fn(pltpu.make_async_copy(zero_hbm.at[pl.ds(0, p * rp)],
                                             xs_hbm.at[pl.ds(pl.multiple_of(row * rp, rp), p * rp)], sem_pad))
                row = row + (npad & p)
            return carry
        lax.fori_loop(0, ne, per_expert, 0)

    @pl.when(i == 0)
    def _():
        pad_copies(lambda cp: cp.start())

    idx_cp = pltpu.make_async_copy(dest_hbm.at[i], idx_ref, sem_idx)
    idx_cp.start()
    idx_cp.wait()

    def per_token(t, carry):
        for k in range(TOP_K):
            row_copy(t, idx_ref[t * TOP_K + k]).start(priority=k % 2)
        return carry

    lax.fori_loop(0, tq, per_token, 0)

    xb = _load_packed_rows(xp_ref, tq, sh_ref.shape[1])
    dot = functools.partial(jnp.dot, preferred_element_type=F32)
    hmid = (_silu(dot(xb, sg_ref[...])) * dot(xb, su_ref[...])).astype(BF16)
    sh_ref[...] = dot(hmid, sd_ref[...])

    def drain(t, carry):
        for _ in range(TOP_K * COPY_UNROLL):
            row_copy(0, 0).wait()
        return carry

    lax.fori_loop(0, tq // COPY_UNROLL, drain, 0)

    @pl.when(i == 0)
    def _():
        pad_copies(lambda cp: cp.wait())


def _dispatch(pad_info, dest_tok, xp, n, rows, sg, su, sd):
    rp = xp.shape[0] // n
    d, hid = sg.shape
    ne = pad_info.shape[1]
    tq = _tile(n, DISPATCH_ROWS)
    steps = n // tq
    dest2 = dest_tok.reshape(steps, tq * TOP_K)
    zero = jnp.zeros((EXPERT_ROWS * rp, LANES), U32)
    any_spec = pl.BlockSpec(memory_space=pl.ANY)
    full = lambda a: pl.BlockSpec(a.shape, lambda i, pad: (0, 0))
    grid_spec = pltpu.PrefetchScalarGridSpec(
        num_scalar_prefetch=1, grid=(steps,),
        in_specs=[any_spec, pl.BlockSpec((tq * rp, LANES), lambda i, pad: (i, 0)), any_spec,
                  full(sg), full(su), full(sd)],
        out_specs=(any_spec, pl.BlockSpec((tq, d), lambda i, pad: (i, 0))),
        scratch_shapes=[pltpu.SMEM((tq * TOP_K,), I32), pltpu.SemaphoreType.DMA(()),
                        pltpu.SemaphoreType.DMA(()), pltpu.SemaphoreType.DMA(())])
    vmem = 2 * (_nbytes((tq * rp, LANES), U32) + 3 * _nbytes((d, hid), BF16) + _nbytes((tq, d), F32)) \
        + _nbytes((tq, d), BF16) + 3 * _nbytes((tq, hid), F32) + _nbytes((tq, d), F32)
    return pl.pallas_call(
        functools.partial(_dispatch_kernel, tq=tq, ne=ne, rp=rp),
        out_shape=(jax.ShapeDtypeStruct((rows * rp, LANES), U32), jax.ShapeDtypeStruct((n, d), F32)),
        grid_spec=grid_spec,
        compiler_params=_params(("arbitrary",), vmem),
    )(pad_info, dest2, xp, zero, sg, su, sd)


def _expert_kernel(plan_ref, nu_ref, xs_ref, wg_hbm, wu_hbm, wd_hbm, ys_ref,
                   sg_ref, su_ref, sd_ref, wgb_ref, wub_ref, wdb_ref, sem_ref, *, layer):
    b = pl.program_id(0)

    def fetch(e, slot):
        return (pltpu.make_async_copy(wg_hbm.at[layer, e], sg_ref.at[slot], sem_ref.at[slot]),
                pltpu.make_async_copy(wu_hbm.at[layer, e], su_ref.at[slot], sem_ref.at[slot]),
                pltpu.make_async_copy(wd_hbm.at[layer, e], sd_ref.at[slot], sem_ref.at[slot]))

    @pl.when(b < nu_ref[0])
    def _():
        e = plan_ref[0, b]
        nxt = plan_ref[2, b]
        slot = plan_ref[3, b]

        @pl.when(b == 0)
        def _():
            for cp in fetch(e, slot):
                cp.start()

        @pl.when(plan_ref[1, b] == 1)
        def _():
            @pl.when(nxt >= 0)
            def _():
                for cp in fetch(nxt, 1 - slot):
                    cp.start()

            for cp in fetch(e, slot):
                cp.wait()
            wgb_ref[...] = sg_ref[slot].astype(BF16)
            wub_ref[...] = su_ref[slot].astype(BF16)
            wdb_ref[...] = sd_ref[slot].astype(BF16)

        t = EXPERT_ROWS
        d = wgb_ref.shape[0]
        xb = _load_packed_rows(xs_ref, t, d)
        dot = functools.partial(jnp.dot, preferred_element_type=F32)
        hmid = (_silu(dot(xb, wgb_ref[...])) * dot(xb, wub_ref[...])).astype(BF16)
        _store_packed_rows(ys_ref, dot(hmid, wdb_ref[...]))


def _experts(plan, n_used, xs, wg, wu, wd, layer):
    _, ne, d, hid = wg.shape
    t = EXPERT_ROWS
    rp = d // PAIR
    rows = xs.shape[0] // rp
    nb = rows // t
    clamp = lambda b, plan, nu: (jnp.minimum(b, jnp.maximum(nu[0] - 1, 0)), 0)
    any_spec = pl.BlockSpec(memory_space=pl.ANY)
    grid_spec = pltpu.PrefetchScalarGridSpec(
        num_scalar_prefetch=2, grid=(nb,),
        in_specs=[pl.BlockSpec((t * rp, LANES), clamp), any_spec, any_spec, any_spec],
        out_specs=pl.BlockSpec((t * rp, LANES), clamp),
        scratch_shapes=[pltpu.VMEM((2, d, hid), F32), pltpu.VMEM((2, d, hid), F32), pltpu.VMEM((2, hid, d), F32),
                        pltpu.VMEM((d, hid), BF16), pltpu.VMEM((d, hid), BF16), pltpu.VMEM((hid, d), BF16),
                        pltpu.SemaphoreType.DMA((2,))])
    vmem = 2 * (3 * _nbytes((d, hid), F32) + 2 * _nbytes((t, d // 2), U32)) \
        + 3 * _nbytes((d, hid), BF16) + 3 * _nbytes((t, d), F32)
    return pl.pallas_call(
        functools.partial(_expert_kernel, layer=layer),
        out_shape=jax.ShapeDtypeStruct((rows * rp, LANES), U32),
        grid_spec=grid_spec,
        compiler_params=_params(("arbitrary",), vmem),
    )(plan, n_used, xs, wg, wu, wd)


def _combine_kernel(dest_ref, ys_hbm, x_ref, gate_ref, sh_ref, g_ref, b_ref, o_ref,
                    buf_ref, sem_ref, *, tc, alpha):
    i = pl.program_id(0)
    rp = x_ref.shape[1] // PAIR
    half = i % 2

    def row_copy(src_row, dst_row, h):
        return pltpu.make_async_copy(ys_hbm.at[pl.ds(pl.multiple_of(src_row * rp, rp), rp)],
                                     buf_ref.at[pl.ds(pl.multiple_of(dst_row * rp, rp), rp)], sem_ref.at[h])

    def fetch(step, h):
        def per_group(g, carry):
            src0 = (step * tc + g * COPY_UNROLL) * TOP_K
            dst0 = h * TOP_K * tc + g * COPY_UNROLL
            for u in range(COPY_UNROLL):
                for k in range(TOP_K):
                    row_copy(dest_ref[src0 + u * TOP_K + k], dst0 + k * tc + u, h).start(priority=k % 2)
            return carry
        lax.fori_loop(0, tc // COPY_UNROLL, per_group, 0)

    @pl.when(i == 0)
    def _():
        fetch(0, 0)

    @pl.when(i + 1 < pl.num_programs(0))
    def _():
        fetch(i + 1, 1 - half)

    x = x_ref[...]
    shared = sh_ref[...]

    def drain(g, carry):
        for _ in range(COPY_UNROLL * TOP_K):
            row_copy(0, 0, half).wait()
        return carry

    lax.fori_loop(0, tc // COPY_UNROLL, drain, 0)

    gate = gate_ref[...]
    base = half * (TOP_K * tc * rp)
    cols = []
    for s in range(rp):
        lo = hi = None
        for k in range(TOP_K):
            w = buf_ref[pl.ds(base + k * tc * rp + s, tc, stride=rp), :]
            gk = gate[:, k:k + 1]
            lo_k = lax.bitcast_convert_type(w << 16, F32) * gk
            hi_k = lax.bitcast_convert_type(w & jnp.uint32(0xFFFF0000), F32) * gk
            lo = lo_k if lo is None else lo + lo_k
            hi = hi_k if hi is None else hi + hi_k
        cols += [lo, hi]
    routed = jnp.concatenate(cols, axis=1)
    o_ref[...] = _layer_norm_rows(alpha * x + (shared + routed), g_ref[...], b_ref[...])


def _combine(dest_tok, ys, x, gate_tok, shared, g, b, alpha):
    n, d = x.shape
    tc = _tile(n, COMBINE_ROWS)
    steps = n // tc
    rp = d // PAIR
    vec = pl.BlockSpec((1, d), lambda i, dest: (0, 0))
    rows = pl.BlockSpec((tc, d), lambda i, dest: (i, 0))
    buf_rows = 2 * TOP_K * tc * rp
    vmem = _nbytes((buf_rows, LANES), U32) + 2 * 3 * _nbytes((tc, d), F32) + 6 * _nbytes((tc, d), F32)
    grid_spec = pltpu.PrefetchScalarGridSpec(
        num_scalar_prefetch=1, grid=(steps,),
        in_specs=[pl.BlockSpec(memory_space=pl.ANY), rows,
                  pl.BlockSpec((tc, TOP_K), lambda i, dest: (i, 0)), rows, vec, vec],
        out_specs=pl.BlockSpec((tc, d), lambda i, dest: (i, 0)),
        scratch_shapes=[pltpu.VMEM((buf_rows, LANES), U32), pltpu.SemaphoreType.DMA((2,))])
    return pl.pallas_call(
        functools.partial(_combine_kernel, tc=tc, alpha=alpha),
        out_shape=jax.ShapeDtypeStruct((n, d), F32),
        grid_spec=grid_spec,
        compiler_params=_params(("arbitrary",), vmem),
    )(dest_tok, ys, x, gate_tok, shared, g, b)


def _moe(x1, x1p, eid, rnk, gate, cnt, wg, wu, wd, layer, sg, su, sd, g, b, alpha):
    n, d = x1.shape
    ne = cnt.shape[0]
    counts = cnt[:, 0]
    padded = (counts + EXPERT_ROWS - 1) // EXPERT_ROWS * EXPERT_ROWS
    eidx = jnp.arange(ne, dtype=I32)
    pad_end = jnp.sum(jnp.where(eidx[None, :] <= eidx[:, None], padded[None, :], 0), axis=1)
    pad_start = pad_end - padded
    n_blocks = (n * TOP_K + ne * (EXPERT_ROWS - 1)) // EXPERT_ROWS
    rows = n_blocks * EXPERT_ROWS
    start_of = jnp.sum(jnp.where(eid[:, :, None] == eidx, pad_start, 0), axis=-1)
    dest_tok = (start_of + rnk).T.reshape(-1).astype(I32)
    pad_info = jnp.stack([pad_start + counts, padded - counts]).astype(I32)
    total = pad_end[ne - 1]
    n_used = (total // EXPERT_ROWS).astype(I32).reshape(1)
    blk_start = jnp.minimum(jnp.arange(n_blocks, dtype=I32) * EXPERT_ROWS, total - 1)
    blk_exp = jnp.sum((pad_end[None, :] <= blk_start[:, None]).astype(I32), axis=1)
    blk_exp = jnp.minimum(blk_exp, ne - 1).astype(I32)
    has = counts > 0
    later = jnp.where((eidx[None, :] > eidx[:, None]) & has[None, :], eidx[None, :], ne)
    next_e = jnp.min(later, axis=1)
    next_e = jnp.where(next_e == ne, -1, next_e)
    ordinal = jnp.sum(jnp.where((eidx[None, :] <= eidx[:, None]) & has[None, :], 1, 0), axis=1) - 1
    first = jnp.concatenate([jnp.ones((1,), I32), (blk_exp[1:] != blk_exp[:-1]).astype(I32)])
    plan = jnp.stack([blk_exp, first, next_e[blk_exp], ordinal[blk_exp] % 2]).astype(I32)
    xs, shared = _dispatch(pad_info, dest_tok, x1p, n, rows, sg, su, sd)
    ys = _experts(plan, n_used, xs, wg, wu, wd, layer)
    return _combine(dest_tok, ys, x1, gate.T, shared, g, b, alpha)


def kernel(x, mem, w_in, conv_w, w_conv_out, hg_lb_logits, hg_norm_g, w_hg_out, w_mem_k, w_mem_v,
           w_xa_out, w_o, ln1_g, ln1_b, router_w, router_b, exp_wg, exp_wu, exp_wd,
           sh_wg, sh_wu, sh_wd, ln2_g, ln2_b):
    bsz, seq, d = x.shape
    depth = w_in.shape[0]
    n_mem = mem.shape[1]
    cwid = conv_w.shape[2]
    hw = w_hg_out.shape[1]
    xw = w_mem_k.shape[2]
    alpha = float((2 * depth) ** 0.25)
    n = bsz * seq

    lb_all = jnp.cumsum(jax.nn.softmax(hg_lb_logits.astype(F32), axis=0), axis=0)
    lb_all = lb_all - lb_all[0:1]
    o_conv, o_hg, o_xa, o_gate = 0, 3 * cwid, 3 * cwid + 4 * hw, 3 * cwid + 4 * hw + xw

    h = x.reshape(n, d)
    memf = mem.reshape(bsz * n_mem, d)
    for l in range(depth):
        wl = w_in[l]
        w_conv_in = wl[:, o_conv:o_hg].astype(BF16)
        w_hg_in = wl[:, o_hg:o_xa].astype(BF16)
        w_xa_in = wl[:, o_xa:o_gate].astype(BF16)
        w_gates = wl[:, o_gate:].astype(BF16)
        vec = lambda a: a.astype(F32).reshape(1, -1)

        ya = _conv_branch(h, w_conv_in, conv_w[l].astype(F32), seq)
        lb = lb_all[l]
        yb = _hgrn_branch(h, w_hg_in, vec(jnp.log(lb)), vec(jnp.log1p(-lb)), vec(hg_norm_g[l]), bsz, seq)
        w_kv = jnp.concatenate([w_mem_k[l], w_mem_v[l]], axis=1).astype(BF16)
        kv = _matmul(memf, w_kv, BF16, KV_ROWS, xw)
        yx = _attn_branch(h, w_xa_in, kv, seq, n_mem)
        merged = _merge(h, ya, yb, yx, w_gates, w_conv_out[l].astype(BF16), w_hg_out[l].astype(BF16),
                        w_xa_out[l].astype(BF16))
        x1, x1p, eid, rnk, gate, cnt = _oproj_ln_route(
            merged, w_o[l].astype(BF16), h, vec(ln1_g[l]), vec(ln1_b[l]),
            router_w[l].astype(F32).T, router_b[l].astype(F32).reshape(-1, 1), alpha)
        h = _moe(x1, x1p, eid, rnk, gate, cnt, exp_wg, exp_wu, exp_wd, l,
                    sh_wg[l].astype(BF16), sh_wu[l].astype(BF16), sh_wd[l].astype(BF16),
                    vec(ln2_g[l]), vec(ln2_b[l]), alpha)
    return h.reshape(bsz, seq, d)
```

```python
import functools

import jax
import jax.numpy as jnp
import numpy as np
from jax import lax
from jax.experimental import pallas as pl
from jax.experimental.pallas import tpu as pltpu

F32 = jnp.float32
BF16 = jnp.bfloat16
I32 = jnp.int32
U32 = jnp.uint32

HG_DK = 128
XA_DH = 128
TOP_K = 8
N_GROUPS = 8
TOPK_GROUPS = 4
ROUTED_SCALE = 2.5
LN_EPS = 1e-5
RMS_EPS = 1e-6
HG_CHUNK = 64
HG_LEVELS = (1, 2, 4, 8, 16, 32)
EXPERT_ROWS = 512
COPY_UNROLL = 8

PROJ_ROWS = 512
HGRN_ROWS = 256
MERGE_ROWS, MERGE_COLS = 1024, 512
KV_ROWS = 1024
DISPATCH_ROWS = 512
COMBINE_ROWS = 128

V7X_VMEM_BYTES = 64 * 1024 * 1024
VMEM_CAP = V7X_VMEM_BYTES - 8 * 1024 * 1024


def _params(semantics, vmem_bytes):
    limit = int(min(VMEM_CAP, max(32 * 1024 * 1024, vmem_bytes * 5 // 4 + (4 << 20))))
    return pltpu.CompilerParams(dimension_semantics=semantics, vmem_limit_bytes=limit)


def _nbytes(shape, dtype):
    n = 1
    for s in shape:
        n *= s
    return n * jnp.dtype(dtype).itemsize


def _tile(n, want):
    t = min(n, want)
    while n % t:
        t //= 2
    return t


def _silu(x):
    return x * (1.0 / (1.0 + jnp.exp(-x)))


def _sigmoid(x):
    return 1.0 / (1.0 + jnp.exp(-x))


def _layer_norm_rows(y, g, b):
    mu = jnp.mean(y, axis=-1, keepdims=True)
    yc = y - mu
    var = jnp.mean(yc * yc, axis=-1, keepdims=True)
    return yc * lax.rsqrt(var + LN_EPS) * g + b


LANES = 128
PAIR = 2 * LANES


def _store_packed_rows(p_ref, y):
    rows, d = y.shape
    pieces = d // PAIR
    for s in range(pieces):
        lo = lax.bitcast_convert_type(y[:, s * PAIR:s * PAIR + LANES].astype(BF16).astype(F32), U32)
        hi = lax.bitcast_convert_type(y[:, s * PAIR + LANES:(s + 1) * PAIR].astype(BF16).astype(F32), U32)
        p_ref[pl.ds(s, rows, stride=pieces), :] = (hi & jnp.uint32(0xFFFF0000)) | (lo >> 16)


def _load_packed_rows(p_ref, rows, d):
    pieces = d // PAIR
    cols = []
    for s in range(pieces):
        w = p_ref[pl.ds(s, rows, stride=pieces), :]
        cols.append(lax.bitcast_convert_type(w << 16, F32).astype(BF16))
        cols.append(lax.bitcast_convert_type(w & jnp.uint32(0xFFFF0000), F32).astype(BF16))
    return jnp.concatenate(cols, axis=1)


def _matmul_kernel(x_ref, w_ref, o_ref):
    o_ref[...] = jnp.dot(x_ref[...].astype(BF16), w_ref[...],
                         preferred_element_type=F32).astype(o_ref.dtype)


def _matmul(x, w, out_dtype, tm, tn):
    m, k = x.shape
    n = w.shape[1]
    tm, tn = _tile(m, tm), _tile(n, tn)
    vmem = 2 * (_nbytes((tm, k), x.dtype) + _nbytes((k, tn), w.dtype) + _nbytes((tm, tn), out_dtype)) \
        + _nbytes((tm, k), BF16) + _nbytes((tm, tn), F32)
    return pl.pallas_call(
        _matmul_kernel,
        out_shape=jax.ShapeDtypeStruct((m, n), out_dtype),
        grid=(m // tm, n // tn),
        in_specs=[pl.BlockSpec((tm, k), lambda i, j: (i, 0)),
                  pl.BlockSpec((k, tn), lambda i, j: (0, j))],
        out_specs=pl.BlockSpec((tm, tn), lambda i, j: (i, j)),
        compiler_params=_params(("arbitrary", "arbitrary"), vmem),
    )(x, w)


def _conv_kernel(x_ref, w_ref, cw_ref, o_ref, carry_ref, *, tiles_per_seq):
    i = pl.program_id(0)
    tm, cwid = o_ref.shape

    @pl.when(i % tiles_per_seq == 0)
    def _():
        carry_ref[...] = jnp.zeros_like(carry_ref)

    p = jnp.dot(x_ref[...].astype(BF16), w_ref[...], preferred_element_type=F32)
    u = p[:, :cwid]
    b = p[:, cwid:2 * cwid]
    c = p[:, 2 * cwid:]
    cu = c * u
    prev = carry_ref[...]
    row = lax.broadcasted_iota(I32, (tm, cwid), 0)
    cu1 = jnp.where(row == 0, prev[7:8, :], pltpu.roll(cu, 1, 0))
    cu2 = pltpu.roll(cu, 2, 0)
    cu2 = jnp.where(row == 0, prev[6:7, :], jnp.where(row == 1, prev[7:8, :], cu2))
    cw = cw_ref[...]
    y = b * (cw[0:1, :] * cu2 + cw[1:2, :] * cu1 + cw[2:3, :] * cu)
    o_ref[...] = y.astype(o_ref.dtype)
    carry_ref[...] = cu[tm - 8:, :]


def _conv_branch(x, w_conv_in, conv_w, seq):
    n, d = x.shape
    cwid = conv_w.shape[1]
    assert conv_w.shape[0] == 3
    tm = _tile(seq, PROJ_ROWS)
    vmem = 2 * (_nbytes((tm, d), x.dtype) + _nbytes(w_conv_in.shape, BF16) + _nbytes((tm, cwid), BF16)) \
        + 6 * _nbytes((tm, 3 * cwid), F32)
    return pl.pallas_call(
        functools.partial(_conv_kernel, tiles_per_seq=seq // tm),
        out_shape=jax.ShapeDtypeStruct((n, cwid), BF16),
        grid=(n // tm,),
        in_specs=[pl.BlockSpec((tm, d), lambda i: (i, 0)),
                  pl.BlockSpec(w_conv_in.shape, lambda i: (0, 0)),
                  pl.BlockSpec(conv_w.shape, lambda i: (0, 0))],
        out_specs=pl.BlockSpec((tm, cwid), lambda i: (i, 0)),
        scratch_shapes=[pltpu.VMEM((8, cwid), F32)],
        compiler_params=_params(("arbitrary",), vmem),
    )(x, w_conv_in, conv_w)


def _head_sum(x):
    return jnp.sum(x, axis=-1, keepdims=True)


def _hgrn_tables(tt):
    t = np.arange(tt)[:, None]
    j = np.arange(tt)[None, :]
    tri = ((t // HG_CHUNK) == (j // HG_CHUNK)) & (j <= t)
    level = np.zeros((tt, tt), np.int32)
    for i, h in enumerate(HG_LEVELS):
        level[((t // (2 * h)) == (j // (2 * h))) & ((t % (2 * h)) >= h) & ((j % (2 * h)) < h)] = i + 1
    return tri.astype(np.float32), level


def _hgrn_kernel(x_ref, w_ref, tri_ref, lvl_ref, c1_ref, c2_ref, gain_ref, o_ref,
                 st_ref, p_ref, u_ref, qd_ref, kd_ref, vb_ref, dec_ref, *, heads):
    tt, hw = o_ref.shape
    dk = HG_DK
    ch = HG_CHUNK
    nlev = len(HG_LEVELS)

    @pl.when(pl.program_id(1) == 0)
    def _():
        st_ref[...] = jnp.zeros_like(st_ref)

    p_ref[...] = jnp.dot(x_ref[...].astype(BF16), w_ref[...], preferred_element_type=F32)
    q_ref, f_ref, v_ref, g_ref = (p_ref.at[:, pl.ds(j * hw, hw)] for j in range(4))

    f = f_ref[...]
    log_sig = jnp.minimum(f, 0.0) - jnp.log(1.0 + jnp.exp(-jnp.abs(f)))
    a1 = c1_ref[...]
    a2 = c2_ref[...] + log_sig
    log_f = jnp.maximum(a1, a2) + jnp.log(1.0 + jnp.exp(-jnp.abs(a1 - a2)))
    kin = 1.0 - jnp.exp(log_f)

    t_hi = log_f.astype(BF16)
    rem = log_f - t_hi.astype(F32)
    t_mid = rem.astype(BF16)
    t_lo = (rem - t_mid.astype(F32)).astype(BF16)

    tri = tri_ref[...]
    dot = functools.partial(jnp.dot, preferred_element_type=F32)
    bcum = dot(tri, t_hi) + dot(tri, t_mid) + dot(tri, t_lo)

    def group_row(a, size, idx):
        a3 = a.reshape(tt // size, size, hw)
        return jnp.broadcast_to(a3[:, idx:idx + 1, :], a3.shape).reshape(tt, hw)

    blast = group_row(bcum, ch, ch - 1)
    row = lax.broadcasted_iota(I32, (tt, hw), 0)

    def level_decay(h):
        upper = (row % (2 * h)) >= h
        if h == 1:
            return upper, jnp.where(upper, 1.0 - kin, 1.0)
        if h == 2:
            pos = row % 4
            nxt = pltpu.roll(log_f, tt - 1, 0)
            prv = pltpu.roll(log_f, 1, 0)
            z = jnp.where(pos == 0, nxt, jnp.where(pos == 1, 0.0, jnp.where(pos == 2, log_f, log_f + prv)))
            return upper, jnp.exp(z)
        refb = group_row(bcum, 2 * h, h - 1)
        return upper, jnp.exp(jnp.where(upper, bcum - refb, refb - bcum))

    q = q_ref[...]
    qd_ref[...] = (q * jnp.exp(bcum)).astype(BF16)
    kd_ref[...] = (kin * jnp.exp(blast - bcum)).astype(BF16)
    vb_ref[...] = v_ref[...].astype(BF16)
    dec_ref[...] = jnp.exp(blast)

    for i, h in enumerate(HG_LEVELS):
        upper, decay = level_decay(h)
        u_ref[i] = (jnp.where(upper, q, kin) * decay).astype(BF16)

    lvl = lvl_ref[...]
    gain = gain_ref[...]
    nt_dims = (((1,), (1,)), ((), ()))
    for h in range(heads):
        ls = slice(h * dk, (h + 1) * dk)
        scores = jnp.zeros((tt, tt), F32)
        for i in range(nlev):
            u = u_ref[i, :, ls]
            gram = lax.dot_general(u, u, nt_dims, preferred_element_type=F32)
            scores = scores + jnp.where(lvl == i + 1, gram, 0.0)
        acc = jnp.dot(scores.astype(BF16), vb_ref[:, ls], preferred_element_type=F32)
        acc = acc + _head_sum(q[:, ls] * kin[:, ls]) * v_ref[:, ls]
        inter = []
        for c in range(tt // ch):
            rs = slice(c * ch, (c + 1) * ch)
            st = st_ref[h]
            inter.append(lax.dot_general(qd_ref[rs, ls], st.astype(BF16), nt_dims, preferred_element_type=F32))
            upd = lax.dot_general(vb_ref[rs, ls], kd_ref[rs, ls], (((0,), (0,)), ((), ())),
                                  preferred_element_type=F32)
            st_ref[h] = st * dec_ref[c * ch:c * ch + 1, ls] + upd
        acc = acc + jnp.concatenate(inter, axis=0)
        ms = _head_sum(acc * acc) * (1.0 / dk)
        y = acc * lax.rsqrt(ms + RMS_EPS) * gain[:, ls] * _silu(g_ref[:, ls])
        o_ref[:, ls] = y.astype(o_ref.dtype)


def _hgrn_branch(x, w_hg_in, c1, c2, gain, bsz, seq):
    n, d = x.shape
    hw = w_hg_in.shape[1] // 4
    heads = hw // HG_DK
    tt = _tile(seq, HGRN_ROWS)
    assert tt % HG_CHUNK == 0
    nt = seq // tt
    nlev = len(HG_LEVELS)
    tri, level = _hgrn_tables(tt)
    tri = jnp.asarray(tri, BF16)
    level = jnp.asarray(level, I32)
    vec = pl.BlockSpec((1, hw), lambda b, t: (0, 0))
    vmem = 2 * (_nbytes((tt, d), x.dtype) + _nbytes(w_hg_in.shape, BF16) + _nbytes((tt, hw), BF16)
                + _nbytes(tri.shape, BF16) + _nbytes(level.shape, I32)) \
        + _nbytes((tt, d), BF16) + 16 * _nbytes((tt, hw), F32) + nlev * _nbytes((tt, hw), BF16) \
        + 4 * _nbytes((tt, tt), F32) + heads * HG_DK * HG_DK * 4
    return pl.pallas_call(
        functools.partial(_hgrn_kernel, heads=heads),
        out_shape=jax.ShapeDtypeStruct((n, hw), BF16),
        grid=(bsz, nt),
        in_specs=[pl.BlockSpec((tt, d), lambda b, t: (b * nt + t, 0)),
                  pl.BlockSpec(w_hg_in.shape, lambda b, t: (0, 0)),
                  pl.BlockSpec(tri.shape, lambda b, t: (0, 0)),
                  pl.BlockSpec(level.shape, lambda b, t: (0, 0)), vec, vec, vec],
        out_specs=pl.BlockSpec((tt, hw), lambda b, t: (b * nt + t, 0)),
        scratch_shapes=[pltpu.VMEM((heads, HG_DK, HG_DK), F32), pltpu.VMEM((tt, 4 * hw), F32),
                        pltpu.VMEM((nlev, tt, hw), BF16),
                        pltpu.VMEM((tt, hw), BF16), pltpu.VMEM((tt, hw), BF16),
                        pltpu.VMEM((tt, hw), BF16), pltpu.VMEM((tt, hw), F32)],
        compiler_params=_params(("arbitrary", "arbitrary"), vmem),
    )(x, w_hg_in, tri, level, c1, c2, gain)


def _attn_kernel(x_ref, wq_ref, k_ref, v_ref, o_ref, *, heads):
    dh = XA_DH
    q = jnp.dot(x_ref[...].astype(BF16), wq_ref[...], preferred_element_type=F32)
    scale = dh ** -0.5
    for h in range(heads):
        ls = slice(h * dh, (h + 1) * dh)
        s = lax.dot_general(q[:, ls].astype(BF16), k_ref[:, ls], (((1,), (1,)), ((), ())),
                            preferred_element_type=F32) * scale
        e = jnp.exp(s - jnp.max(s, axis=-1, keepdims=True))
        p = e / jnp.sum(e, axis=-1, keepdims=True)
        o = jnp.dot(p.astype(BF16), v_ref[:, ls], preferred_element_type=F32)
        o_ref[:, ls] = o.astype(o_ref.dtype)


def _attn_branch(x, wq, kv, seq, n_mem):
    n, d = x.shape
    xw = wq.shape[1]
    heads = xw // XA_DH
    tm = _tile(seq, PROJ_ROWS)
    per_seq = seq // tm
    vmem = 2 * (_nbytes((tm, d), x.dtype) + _nbytes(wq.shape, BF16) + 2 * _nbytes((n_mem, xw), BF16)
                + _nbytes((tm, xw), BF16)) + 4 * _nbytes((tm, xw), F32) + 4 * _nbytes((tm, n_mem), F32)
    return pl.pallas_call(
        functools.partial(_attn_kernel, heads=heads),
        out_shape=jax.ShapeDtypeStruct((n, xw), BF16),
        grid=(n // tm,),
        in_specs=[pl.BlockSpec((tm, d), lambda i: (i, 0)),
                  pl.BlockSpec(wq.shape, lambda i: (0, 0)),
                  pl.BlockSpec((n_mem, xw), lambda i: (i // per_seq, 0)),
                  pl.BlockSpec((n_mem, xw), lambda i: (i // per_seq, 1))],
        out_specs=pl.BlockSpec((tm, xw), lambda i: (i, 0)),
        compiler_params=_params(("arbitrary",), vmem),
    )(x, wq, kv, kv)


def _merge_kernel(x_ref, ya_ref, yb_ref, yx_ref, wga_ref, wgb_ref, wgx_ref, wa_ref, wb_ref, wx_ref,
                  o_ref, xb_ref):
    @pl.when(pl.program_id(1) == 0)
    def _():
        xb_ref[...] = x_ref[...].astype(BF16)

    xb = xb_ref[...]
    dot = functools.partial(jnp.dot, preferred_element_type=F32)
    m = _sigmoid(dot(xb, wga_ref[...])) * dot(ya_ref[...], wa_ref[...])
    m = m + _sigmoid(dot(xb, wgb_ref[...])) * dot(yb_ref[...], wb_ref[...])
    m = m + _sigmoid(dot(xb, wgx_ref[...])) * dot(yx_ref[...], wx_ref[...])
    o_ref[...] = m.astype(o_ref.dtype)


def _merge(x, ya, yb, yx, w_gates, wa, wb, wx):
    n, d = x.shape
    tm, tn = _tile(n, MERGE_ROWS), _tile(d, MERGE_COLS)
    nj = d // tn
    row = lambda a: pl.BlockSpec((tm, a.shape[1]), lambda i, j: (i, 0))
    gate = lambda g: pl.BlockSpec((d, tn), lambda i, j, g=g: (0, g * nj + j))
    outw = lambda a: pl.BlockSpec((a.shape[0], tn), lambda i, j: (0, j))
    vmem = 2 * (_nbytes((tm, d), x.dtype) + _nbytes((tm, ya.shape[1] + yb.shape[1] + yx.shape[1]), BF16)
                + 3 * _nbytes((d, tn), BF16) + _nbytes((wa.shape[0] + wb.shape[0] + wx.shape[0], tn), BF16)
                + _nbytes((tm, tn), BF16)) + _nbytes((tm, d), BF16) + 8 * _nbytes((tm, tn), F32)
    return pl.pallas_call(
        _merge_kernel,
        out_shape=jax.ShapeDtypeStruct((n, d), BF16),
        grid=(n // tm, nj),
        in_specs=[row(x), row(ya), row(yb), row(yx), gate(0), gate(1), gate(2),
                  outw(wa), outw(wb), outw(wx)],
        out_specs=pl.BlockSpec((tm, tn), lambda i, j: (i, j)),
        scratch_shapes=[pltpu.VMEM((tm, d), BF16)],
        compiler_params=_params(("arbitrary", "arbitrary"), vmem),
    )(x, ya, yb, yx, w_gates, w_gates, w_gates, wa, wb, wx)


def _oproj_ln_kernel(m_ref, w_ref, x_ref, g_ref, b_ref, rw_ref, rb_ref,
                     o_ref, p_ref, eid_ref, rnk_ref, gate_ref, cnt_ref, carry_ref, *, alpha):
    y = alpha * x_ref[...] + jnp.dot(m_ref[...], w_ref[...], preferred_element_type=F32)
    y = _layer_norm_rows(y, g_ref[...], b_ref[...])
    o_ref[...] = y
    _store_packed_rows(p_ref, y)
    _route(y, rw_ref, rb_ref, eid_ref, rnk_ref, gate_ref, cnt_ref, carry_ref)


def _oproj_ln_route(merged, w_o, x, g, b, rw_t, rb, alpha):
    n, d = x.shape
    assert d % PAIR == 0
    pieces = d // PAIR
    ne = rw_t.shape[0]
    tm = _tile(n, PROJ_ROWS)
    cur = lambda i: (i, 0)
    vec = pl.BlockSpec((1, d), lambda i: (0, 0))
    topk = lambda dt: jax.ShapeDtypeStruct((TOP_K, n), dt)
    kspec = pl.BlockSpec((TOP_K, tm), lambda i: (0, i))
    vmem = 2 * (_nbytes((tm, d), BF16) + _nbytes((d, d), BF16) + 2 * _nbytes((tm, d), F32)
                + _nbytes((tm, d // 2), U32) + _nbytes((ne, d), F32)) + 4 * _nbytes((tm, d), F32) \
        + 3 * _nbytes((tm, tm), F32) + 24 * _nbytes((ne, tm), F32)
    return pl.pallas_call(
        functools.partial(_oproj_ln_kernel, alpha=alpha),
        out_shape=(jax.ShapeDtypeStruct((n, d), F32), jax.ShapeDtypeStruct((n * pieces, LANES), U32),
                   topk(I32), topk(I32), topk(F32), jax.ShapeDtypeStruct((ne, LANES), I32)),
        grid=(n // tm,),
        in_specs=[pl.BlockSpec((tm, d), cur),
                  pl.BlockSpec((d, d), lambda i: (0, 0)),
                  pl.BlockSpec((tm, d), cur), vec, vec,
                  pl.BlockSpec((ne, d), lambda i: (0, 0)),
                  pl.BlockSpec((ne, 1), lambda i: (0, 0))],
        out_specs=(pl.BlockSpec((tm, d), cur),
                   pl.BlockSpec((tm * pieces, LANES), cur),
                   kspec, kspec, kspec, pl.BlockSpec((ne, LANES), lambda i: (0, 0))),
        scratch_shapes=[pltpu.VMEM((ne, 1), F32)],
        compiler_params=_params(("arbitrary",), vmem),
    )(merged, w_o, x, g, b, rw_t, rb)


def _rank_desc(vals):
    rows = vals.shape[0]
    iota = lax.broadcasted_iota(I32, vals.shape, 0)
    rank = jnp.zeros(vals.shape, I32)
    for j in range(rows):
        rowv = vals[j:j + 1, :]
        tie = jnp.where(iota > j, 1, 0)
        rank = rank + jnp.where(rowv > vals, 1, jnp.where(rowv == vals, tie, 0))
    return rank


def _route(x, rw_ref, rb_ref, eid_ref, rnk_ref, gate_ref, cnt_ref, carry_ref):
    ne = rw_ref.shape[0]
    tm = x.shape[0]
    gsz = ne // N_GROUPS

    @pl.when(pl.program_id(0) == 0)
    def _():
        carry_ref[...] = jnp.zeros_like(carry_ref)

    logits = lax.dot_general(rw_ref[...], x, (((1,), (1,)), ((), ())),
                             preferred_element_type=F32, precision=lax.Precision.HIGHEST)
    s = _sigmoid(logits)
    sel = s + rb_ref[...]

    grp = sel.reshape(N_GROUPS, gsz, tm)
    sub_iota = lax.broadcasted_iota(I32, (N_GROUPS, gsz, tm), 1)
    m1 = jnp.max(grp, axis=1, keepdims=True)
    first = jnp.min(jnp.where(grp == m1, sub_iota, gsz), axis=1, keepdims=True)
    m2 = jnp.max(jnp.where(sub_iota == first, -jnp.inf, grp), axis=1, keepdims=True)
    gscore = (m1 + m2).reshape(N_GROUPS, tm)

    gsel = _rank_desc(gscore) < TOPK_GROUPS
    emask = jnp.broadcast_to(gsel.reshape(N_GROUPS, 1, tm), (N_GROUPS, gsz, tm)).reshape(ne, tm)
    masked = jnp.where(emask, sel, -jnp.inf)

    eiota = lax.broadcasted_iota(I32, (ne, tm), 0)
    chosen = (_rank_desc(masked) < TOP_K) & emask

    w = jnp.where(chosen, s, 0.0)
    gate = w / jnp.sum(w, axis=0, keepdims=True) * ROUTED_SCALE

    ch = jnp.where(chosen, 1.0, 0.0).astype(BF16)
    tr = lax.broadcasted_iota(I32, (tm, tm), 0)
    tc = lax.broadcasted_iota(I32, (tm, tm), 1)
    before = jnp.where(tr < tc, 1.0, 0.0).astype(BF16)
    rank_tok = (carry_ref[...] + jnp.dot(ch, before, preferred_element_type=F32)).astype(I32)
    carry_ref[...] = carry_ref[...] + jnp.sum(jnp.where(chosen, 1.0, 0.0), axis=1, keepdims=True)
    cnt_ref[...] = jnp.broadcast_to(carry_ref[...], cnt_ref.shape).astype(I32)

    er = lax.broadcasted_iota(I32, (ne, ne), 0)
    ec = lax.broadcasted_iota(I32, (ne, ne), 1)
    lower = jnp.where(ec < er, 1.0, 0.0).astype(BF16)
    slot = jnp.dot(lower, ch, preferred_element_type=F32).astype(I32)
    for k in range(TOP_K):
        pick = chosen & (slot == k)
        eid_ref[k:k + 1, :] = jnp.sum(jnp.where(pick, eiota, 0), axis=0, keepdims=True)
        rnk_ref[k:k + 1, :] = jnp.sum(jnp.where(pick, rank_tok, 0), axis=0, keepdims=True)
        gate_ref[k:k + 1, :] = jnp.sum(jnp.where(pick, gate, 0.0), axis=0, keepdims=True)


def _pad_pieces(max_pad):
    pieces = []
    p = 1
    while p <= max_pad:
        pieces.append(p)
        p *= 2
    return pieces[::-1]


def _dispatch_kernel(pad_ref, dest_hbm, xp_ref, zero_hbm, sg_ref, su_ref, sd_ref, xs_hbm, sh_ref,
                     idx_ref, sem_idx, sem_row, sem_pad, *, tq, ne, rp):
    i = pl.program_id(0)

    def row_copy(src_row, dst_row):
        return pltpu.make_async_copy(xp_ref.at[pl.ds(pl.multiple_of(src_row * rp, rp), rp)],
                                     xs_hbm.at[pl.ds(pl.multiple_of(dst_row * rp, rp), rp)], sem_row)

    def pad_copies(fn):
        def per_expert(e, carry):
            row = pad_ref[0, e]
            npad = pad_ref[1, e]
            for p in _pad_pieces(EXPERT_ROWS - 1):
                @pl.when((npad & p) != 0)
                def _(row=row, p=p):
                    fn(pltpu.make_async_copy(zero_hbm.at[pl.ds(0, p * rp)],
                                             xs_hbm.at[pl.ds(pl.multiple_of(row * rp, rp), p * rp)], sem_pad))
                row = row + (npad & p)
            return carry
        lax.fori_loop(0, ne, per_expert, 0)

    @pl.when(i == 0)
    def _():
        pad_copies(lambda cp: cp.start())

    idx_cp = pltpu.make_async_copy(dest_hbm.at[i], idx_ref, sem_idx)
    idx_cp.start()
    idx_cp.wait()

    def per_token(t, carry):
        for k in range(TOP_K):
            row_copy(t, idx_ref[t * TOP_K + k]).start(priority=k % 2)
        return carry

    lax.fori_loop(0, tq, per_token, 0)

    xb = _load_packed_rows(xp_ref, tq, sh_ref.shape[1])
    dot = functools.partial(jnp.dot, preferred_element_type=F32)
    hmid = (_silu(dot(xb, sg_ref[...])) * dot(xb, su_ref[...])).astype(BF16)
    sh_ref[...] = dot(hmid, sd_ref[...])

    def drain(t, carry):
        for _ in range(TOP_K * COPY_UNROLL):
            row_copy(0, 0).wait()
        return carry

    lax.fori_loop(0, tq // COPY_UNROLL, drain, 0)

    @pl.when(i == 0)
    def _():
        pad_copies(lambda cp: cp.wait())


def _dispatch(pad_info, dest_tok, xp, n, rows, sg, su, sd):
    rp = xp.shape[0] // n
    d, hid = sg.shape
    ne = pad_info.shape[1]
    tq = _tile(n, DISPATCH_ROWS)
    steps = n // tq
    dest2 = dest_tok.reshape(steps, tq * TOP_K)
    zero = jnp.zeros((EXPERT_ROWS * rp, LANES), U32)
    any_spec = pl.BlockSpec(memory_space=pl.ANY)
    full = lambda a: pl.BlockSpec(a.shape, lambda i, pad: (0, 0))
    grid_spec = pltpu.PrefetchScalarGridSpec(
        num_scalar_prefetch=1, grid=(steps,),
        in_specs=[any_spec, pl.BlockSpec((tq * rp, LANES), lambda i, pad: (i, 0)), any_spec,
                  full(sg), full(su), full(sd)],
        out_specs=(any_spec, pl.BlockSpec((tq, d), lambda i, pad: (i, 0))),
        scratch_shapes=[pltpu.SMEM((tq * TOP_K,), I32), pltpu.SemaphoreType.DMA(()),
                        pltpu.SemaphoreType.DMA(()), pltpu.SemaphoreType.DMA(())])
    vmem = 2 * (_nbytes((tq * rp, LANES), U32) + 3 * _nbytes((d, hid), BF16) + _nbytes((tq, d), F32)) \
        + _nbytes((tq, d), BF16) + 3 * _nbytes((tq, hid), F32) + _nbytes((tq, d), F32)
    return pl.pallas_call(
        functools.partial(_dispatch_kernel, tq=tq, ne=ne, rp=rp),
        out_shape=(jax.ShapeDtypeStruct((rows * rp, LANES), U32), jax.ShapeDtypeStruct((n, d), F32)),
        grid_spec=grid_spec,
        compiler_params=_params(("arbitrary",), vmem),
    )(pad_info, dest2, xp, zero, sg, su, sd)


def _expert_kernel(plan_ref, nu_ref, xs_ref, wg_hbm, wu_hbm, wd_hbm, ys_ref,
                   sg_ref, su_ref, sd_ref, wgb_ref, wub_ref, wdb_ref, sem_ref, *, layer):
    b = pl.program_id(0)

    def fetch(e, slot):
        return (pltpu.make_async_copy(wg_hbm.at[layer, e], sg_ref.at[slot], sem_ref.at[slot]),
                pltpu.make_async_copy(wu_hbm.at[layer, e], su_ref.at[slot], sem_ref.at[slot]),
                pltpu.make_async_copy(wd_hbm.at[layer, e], sd_ref.at[slot], sem_ref.at[slot]))

    @pl.when(b < nu_ref[0])
    def _():
        e = plan_ref[0, b]
        nxt = plan_ref[2, b]
        slot = plan_ref[3, b]

        @pl.when(b == 0)
        def _():
            for cp in fetch(e, slot):
                cp.start()

        @pl.when(plan_ref[1, b] == 1)
        def _():
            @pl.when(nxt >= 0)
            def _():
                for cp in fetch(nxt, 1 - slot):
                    cp.start()

            for cp in fetch(e, slot):
                cp.wait()
            wgb_ref[...] = sg_ref[slot].astype(BF16)
            wub_ref[...] = su_ref[slot].astype(BF16)
            wdb_ref[...] = sd_ref[slot].astype(BF16)

        t = EXPERT_ROWS
        d = wgb_ref.shape[0]
        xb = _load_packed_rows(xs_ref, t, d)
        dot = functools.partial(jnp.dot, preferred_element_type=F32)
        hmid = (_silu(dot(xb, wgb_ref[...])) * dot(xb, wub_ref[...])).astype(BF16)
        _store_packed_rows(ys_ref, dot(hmid, wdb_ref[...]))


def _experts(plan, n_used, xs, wg, wu, wd, layer):
    _, ne, d, hid = wg.shape
    t = EXPERT_ROWS
    rp = d // PAIR
    rows = xs.shape[0] // rp
    nb = rows // t
    clamp = lambda b, plan, nu: (jnp.minimum(b, jnp.maximum(nu[0] - 1, 0)), 0)
    any_spec = pl.BlockSpec(memory_space=pl.ANY)
    grid_spec = pltpu.PrefetchScalarGridSpec(
        num_scalar_prefetch=2, grid=(nb,),
        in_specs=[pl.BlockSpec((t * rp, LANES), clamp), any_spec, any_spec, any_spec],
        out_specs=pl.BlockSpec((t * rp, LANES), clamp),
        scratch_shapes=[pltpu.VMEM((2, d, hid), F32), pltpu.VMEM((2, d, hid), F32), pltpu.VMEM((2, hid, d), F32),
                        pltpu.VMEM((d, hid), BF16), pltpu.VMEM((d, hid), BF16), pltpu.VMEM((hid, d), BF16),
                        pltpu.SemaphoreType.DMA((2,))])
    vmem = 2 * (3 * _nbytes((d, hid), F32) + 2 * _nbytes((t, d // 2), U32)) \
        + 3 * _nbytes((d, hid), BF16) + 3 * _nbytes((t, d), F32)
    return pl.pallas_call(
        functools.partial(_expert_kernel, layer=layer),
        out_shape=jax.ShapeDtypeStruct((rows * rp, LANES), U32),
        grid_spec=grid_spec,
        compiler_params=_params(("arbitrary",), vmem),
    )(plan, n_used, xs, wg, wu, wd)


def _combine_kernel(dest_ref, ys_hbm, x_ref, gate_ref, sh_ref, g_ref, b_ref, o_ref,
                    buf_ref, sem_ref, *, tc, alpha):
    i = pl.program_id(0)
    rp = x_ref.shape[1] // PAIR
    half = i % 2

    def row_copy(src_row, dst_row, h):
        return pltpu.make_async_copy(ys_hbm.at[pl.ds(pl.multiple_of(src_row * rp, rp), rp)],
                                     buf_ref.at[pl.ds(pl.multiple_of(dst_row * rp, rp), rp)], sem_ref.at[h])

    def fetch(step, h):
        def per_group(g, carry):
            src0 = (step * tc + g * COPY_UNROLL) * TOP_K
            dst0 = h * TOP_K * tc + g * COPY_UNROLL
            for u in range(COPY_UNROLL):
                for k in range(TOP_K):
                    row_copy(dest_ref[src0 + u * TOP_K + k], dst0 + k * tc + u, h).start(priority=k % 2)
            return carry
        lax.fori_loop(0, tc // COPY_UNROLL, per_group, 0)

    @pl.when(i == 0)
    def _():
        fetch(0, 0)

    @pl.when(i + 1 < pl.num_programs(0))
    def _():
        fetch(i + 1, 1 - half)

    x = x_ref[...]
    shared = sh_ref[...]

    def drain(g, carry):
        for _ in range(COPY_UNROLL * TOP_K):
            row_copy(0, 0, half).wait()
        return carry

    lax.fori_loop(0, tc // COPY_UNROLL, drain, 0)

    gate = gate_ref[...]
    base = half * (TOP_K * tc * rp)
    cols = []
    for s in range(rp):
        lo = hi = None
        for k in range(TOP_K):
            w = buf_ref[pl.ds(base + k * tc * rp + s, tc, stride=rp), :]
            gk = gate[:, k:k + 1]
            lo_k = lax.bitcast_convert_type(w << 16, F32) * gk
            hi_k = lax.bitcast_convert_type(w & jnp.uint32(0xFFFF0000), F32) * gk
            lo = lo_k if lo is None else lo + lo_k
            hi = hi_k if hi is None else hi + hi_k
        cols += [lo, hi]
    routed = jnp.concatenate(cols, axis=1)
    o_ref[...] = _layer_norm_rows(alpha * x + (shared + routed), g_ref[...], b_ref[...])


def _combine(dest_tok, ys, x, gate_tok, shared, g, b, alpha):
    n, d = x.shape
    tc = _tile(n, COMBINE_ROWS)
    steps = n // tc
    rp = d // PAIR
    vec = pl.BlockSpec((1, d), lambda i, dest: (0, 0))
    rows = pl.BlockSpec((tc, d), lambda i, dest: (i, 0))
    buf_rows = 2 * TOP_K * tc * rp
    vmem = _nbytes((buf_rows, LANES), U32) + 2 * 3 * _nbytes((tc, d), F32) + 6 * _nbytes((tc, d), F32)
    grid_spec = pltpu.PrefetchScalarGridSpec(
        num_scalar_prefetch=1, grid=(steps,),
        in_specs=[pl.BlockSpec(memory_space=pl.ANY), rows,
                  pl.BlockSpec((tc, TOP_K), lambda i, dest: (i, 0)), rows, vec, vec],
        out_specs=pl.BlockSpec((tc, d), lambda i, dest: (i, 0)),
        scratch_shapes=[pltpu.VMEM((buf_rows, LANES), U32), pltpu.SemaphoreType.DMA((2,))])
    return pl.pallas_call(
        functools.partial(_combine_kernel, tc=tc, alpha=alpha),
        out_shape=jax.ShapeDtypeStruct((n, d), F32),
        grid_spec=grid_spec,
        compiler_params=_params(("arbitrary",), vmem),
    )(dest_tok, ys, x, gate_tok, shared, g, b)


def _moe(x1, x1p, eid, rnk, gate, cnt, wg, wu, wd, layer, sg, su, sd, g, b, alpha):
    n, d = x1.shape
    ne = cnt.shape[0]
    counts = cnt[:, 0]
    padded = (counts + EXPERT_ROWS - 1) // EXPERT_ROWS * EXPERT_ROWS
    eidx = jnp.arange(ne, dtype=I32)
    pad_end = jnp.sum(jnp.where(eidx[None, :] <= eidx[:, None], padded[None, :], 0), axis=1)
    pad_start = pad_end - padded
    n_blocks = (n * TOP_K + ne * (EXPERT_ROWS - 1)) // EXPERT_ROWS
    rows = n_blocks * EXPERT_ROWS
    start_of = jnp.sum(jnp.where(eid[:, :, None] == eidx, pad_start, 0), axis=-1)
    dest_tok = (start_of + rnk).T.reshape(-1).astype(I32)
    pad_info = jnp.stack([pad_start + counts, padded - counts]).astype(I32)
    total = pad_end[ne - 1]
    n_used = (total // EXPERT_ROWS).astype(I32).reshape(1)
    blk_start = jnp.minimum(jnp.arange(n_blocks, dtype=I32) * EXPERT_ROWS, total - 1)
    blk_exp = jnp.sum((pad_end[None, :] <= blk_start[:, None]).astype(I32), axis=1)
    blk_exp = jnp.minimum(blk_exp, ne - 1).astype(I32)
    has = counts > 0
    later = jnp.where((eidx[None, :] > eidx[:, None]) & has[None, :], eidx[None, :], ne)
    next_e = jnp.min(later, axis=1)
    next_e = jnp.where(next_e == ne, -1, next_e)
    ordinal = jnp.sum(jnp.where((eidx[None, :] <= eidx[:, None]) & has[None, :], 1, 0), axis=1) - 1
    first = jnp.concatenate([jnp.ones((1,), I32), (blk_exp[1:] != blk_exp[:-1]).astype(I32)])
    plan = jnp.stack([blk_exp, first, next_e[blk_exp], ordinal[blk_exp] % 2]).astype(I32)
    xs, shared = _dispatch(pad_info, dest_tok, x1p, n, rows, sg, su, sd)
    ys = _experts(plan, n_used, xs, wg, wu, wd, layer)
    return _combine(dest_tok, ys, x1, gate.T, shared, g, b, alpha)


def kernel(x, mem, w_in, conv_w, w_conv_out, hg_lb_logits, hg_norm_g, w_hg_out, w_mem_k, w_mem_v,
           w_xa_out, w_o, ln1_g, ln1_b, router_w, router_b, exp_wg, exp_wu, exp_wd,
           sh_wg, sh_wu, sh_wd, ln2_g, ln2_b):
    bsz, seq, d = x.shape
    depth = w_in.shape[0]
    n_mem = mem.shape[1]
    cwid = conv_w.shape[2]
    hw = w_hg_out.shape[1]
    xw = w_mem_k.shape[2]
    alpha = float((2 * depth) ** 0.25)
    n = bsz * seq

    lb_all = jnp.cumsum(jax.nn.softmax(hg_lb_logits.astype(F32), axis=0), axis=0)
    lb_all = lb_all - lb_all[0:1]
    o_conv, o_hg, o_xa, o_gate = 0, 3 * cwid, 3 * cwid + 4 * hw, 3 * cwid + 4 * hw + xw

    h = x.reshape(n, d)
    memf = mem.reshape(bsz * n_mem, d)
    for l in range(depth):
        wl = w_in[l]
        w_conv_in = wl[:, o_conv:o_hg].astype(BF16)
        w_hg_in = wl[:, o_hg:o_xa].astype(BF16)
        w_xa_in = wl[:, o_xa:o_gate].astype(BF16)
        w_gates = wl[:, o_gate:].astype(BF16)
        vec = lambda a: a.astype(F32).reshape(1, -1)

        ya = _conv_branch(h, w_conv_in, conv_w[l].astype(F32), seq)
        lb = lb_all[l]
        yb = _hgrn_branch(h, w_hg_in, vec(jnp.log(lb)), vec(jnp.log1p(-lb)), vec(hg_norm_g[l]), bsz, seq)
        w_kv = jnp.concatenate([w_mem_k[l], w_mem_v[l]], axis=1).astype(BF16)
        kv = _matmul(memf, w_kv, BF16, KV_ROWS, xw)
        yx = _attn_branch(h, w_xa_in, kv, seq, n_mem)
        merged = _merge(h, ya, yb, yx, w_gates, w_conv_out[l].astype(BF16), w_hg_out[l].astype(BF16),
                        w_xa_out[l].astype(BF16))
        x1, x1p, eid, rnk, gate, cnt = _oproj_ln_route(
            merged, w_o[l].astype(BF16), h, vec(ln1_g[l]), vec(ln1_b[l]),
            router_w[l].astype(F32).T, router_b[l].astype(F32).reshape(-1, 1), alpha)
        h = _moe(x1, x1p, eid, rnk, gate, cnt, exp_wg, exp_wu, exp_wd, l,
                    sh_wg[l].astype(BF16), sh_wu[l].astype(BF16), sh_wd[l].astype(BF16),
                    vec(ln2_g[l]), vec(ln2_b[l]), alpha)
    return h.reshape(bsz, seq, d)
```

```python
import functools

import jax
import jax.numpy as jnp
import numpy as np
from jax import lax
from jax.experimental import pallas as pl
from jax.experimental.pallas import tpu as pltpu

F32 = jnp.float32
BF16 = jnp.bfloat16
I32 = jnp.int32
U32 = jnp.uint32

HG_DK = 128
XA_DH = 128
TOP_K = 8
N_GROUPS = 8
TOPK_GROUPS = 4
ROUTED_SCALE = 2.5
LN_EPS = 1e-5
RMS_EPS = 1e-6
HG_CHUNK = 64
HG_LEVELS = (1, 2, 4, 8, 16, 32)
EXPERT_ROWS = 512
COPY_UNROLL = 8

PROJ_ROWS = 512
HGRN_ROWS = 256
MERGE_ROWS, MERGE_COLS = 1024, 512
KV_ROWS = 1024
DISPATCH_ROWS = 512
COMBINE_ROWS = 128

V7X_VMEM_BYTES = 64 * 1024 * 1024
VMEM_CAP = V7X_VMEM_BYTES - 8 * 1024 * 1024


def _params(semantics, vmem_bytes):
    limit = int(min(VMEM_CAP, max(32 * 1024 * 1024, vmem_bytes * 5 // 4 + (4 << 20))))
    return pltpu.CompilerParams(dimension_semantics=semantics, vmem_limit_bytes=limit)


def _nbytes(shape, dtype):
    n = 1
    for s in shape:
        n *= s
    return n * jnp.dtype(dtype).itemsize


def _tile(n, want):
    t = min(n, want)
    while n % t:
        t //= 2
    return t


def _silu(x):
    return x * (1.0 / (1.0 + jnp.exp(-x)))


def _sigmoid(x):
    return 1.0 / (1.0 + jnp.exp(-x))


def _layer_norm_rows(y, g, b):
    mu = jnp.mean(y, axis=-1, keepdims=True)
    yc = y - mu
    var = jnp.mean(yc * yc, axis=-1, keepdims=True)
    return yc * lax.rsqrt(var + LN_EPS) * g + b


LANES = 128
PAIR = 2 * LANES


def _store_packed_rows(p_ref, y):
    rows, d = y.shape
    pieces = d // PAIR
    for s in range(pieces):
        lo = lax.bitcast_convert_type(y[:, s * PAIR:s * PAIR + LANES].astype(BF16).astype(F32), U32)
        hi = lax.bitcast_convert_type(y[:, s * PAIR + LANES:(s + 1) * PAIR].astype(BF16).astype(F32), U32)
        p_ref[pl.ds(s, rows, stride=pieces), :] = (hi & jnp.uint32(0xFFFF0000)) | (lo >> 16)


def _load_packed_rows(p_ref, rows, d):
    pieces = d // PAIR
    cols = []
    for s in range(pieces):
        w = p_ref[pl.ds(s, rows, stride=pieces), :]
        cols.append(lax.bitcast_convert_type(w << 16, F32).astype(BF16))
        cols.append(lax.bitcast_convert_type(w & jnp.uint32(0xFFFF0000), F32).astype(BF16))
    return jnp.concatenate(cols, axis=1)


def _matmul_kernel(x_ref, w_ref, o_ref):
    o_ref[...] = jnp.dot(x_ref[...].astype(BF16), w_ref[...],
                         preferred_element_type=F32).astype(o_ref.dtype)


def _matmul(x, w, out_dtype, tm, tn):
    m, k = x.shape
    n = w.shape[1]
    tm, tn = _tile(m, tm), _tile(n, tn)
    vmem = 2 * (_nbytes((tm, k), x.dtype) + _nbytes((k, tn), w.dtype) + _nbytes((tm, tn), out_dtype)) \
        + _nbytes((tm, k), BF16) + _nbytes((tm, tn), F32)
    return pl.pallas_call(
        _matmul_kernel,
        out_shape=jax.ShapeDtypeStruct((m, n), out_dtype),
        grid=(m // tm, n // tn),
        in_specs=[pl.BlockSpec((tm, k), lambda i, j: (i, 0)),
                  pl.BlockSpec((k, tn), lambda i, j: (0, j))],
        out_specs=pl.BlockSpec((tm, tn), lambda i, j: (i, j)),
        compiler_params=_params(("arbitrary", "arbitrary"), vmem),
    )(x, w)


def _conv_kernel(x_ref, w_ref, cw_ref, o_ref, carry_ref, *, tiles_per_seq):
    i = pl.program_id(0)
    tm, cwid = o_ref.shape

    @pl.when(i % tiles_per_seq == 0)
    def _():
        carry_ref[...] = jnp.zeros_like(carry_ref)

    p = jnp.dot(x_ref[...].astype(BF16), w_ref[...], preferred_element_type=F32)
    u = p[:, :cwid]
    b = p[:, cwid:2 * cwid]
    c = p[:, 2 * cwid:]
    cu = c * u
    prev = carry_ref[...]
    row = lax.broadcasted_iota(I32, (tm, cwid), 0)
    cu1 = jnp.where(row == 0, prev[7:8, :], pltpu.roll(cu, 1, 0))
    cu2 = pltpu.roll(cu, 2, 0)
    cu2 = jnp.where(row == 0, prev[6:7, :], jnp.where(row == 1, prev[7:8, :], cu2))
    cw = cw_ref[...]
    y = b * (cw[0:1, :] * cu2 + cw[1:2, :] * cu1 + cw[2:3, :] * cu)
    o_ref[...] = y.astype(o_ref.dtype)
    carry_ref[...] = cu[tm - 8:, :]


def _conv_branch(x, w_conv_in, conv_w, seq):
    n, d = x.shape
    cwid = conv_w.shape[1]
    assert conv_w.shape[0] == 3
    tm = _tile(seq, PROJ_ROWS)
    vmem = 2 * (_nbytes((tm, d), x.dtype) + _nbytes(w_conv_in.shape, BF16) + _nbytes((tm, cwid), BF16)) \
        + 6 * _nbytes((tm, 3 * cwid), F32)
    return pl.pallas_call(
        functools.partial(_conv_kernel, tiles_per_seq=seq // tm),
        out_shape=jax.ShapeDtypeStruct((n, cwid), BF16),
        grid=(n // tm,),
        in_specs=[pl.BlockSpec((tm, d), lambda i: (i, 0)),
                  pl.BlockSpec(w_conv_in.shape, lambda i: (0, 0)),
                  pl.BlockSpec(conv_w.shape, lambda i: (0, 0))],
        out_specs=pl.BlockSpec((tm, cwid), lambda i: (i, 0)),
        scratch_shapes=[pltpu.VMEM((8, cwid), F32)],
        compiler_params=_params(("arbitrary",), vmem),
    )(x, w_conv_in, conv_w)


def _head_sum(x):
    return jnp.sum(x, axis=-1, keepdims=True)


def _hgrn_tables(tt):
    t = np.arange(tt)[:, None]
    j = np.arange(tt)[None, :]
    tri = ((t // HG_CHUNK) == (j // HG_CHUNK)) & (j <= t)
    level = np.zeros((tt, tt), np.int32)
    for i, h in enumerate(HG_LEVELS):
        level[((t // (2 * h)) == (j // (2 * h))) & ((t % (2 * h)) >= h) & ((j % (2 * h)) < h)] = i + 1
    return tri.astype(np.float32), level


def _hgrn_kernel(x_ref, w_ref, tri_ref, lvl_ref, c1_ref, c2_ref, gain_ref, o_ref,
                 st_ref, p_ref, u_ref, qd_ref, kd_ref, vb_ref, dec_ref, *, heads):
    tt, hw = o_ref.shape
    dk = HG_DK
    ch = HG_CHUNK
    nlev = len(HG_LEVELS)

    @pl.when(pl.program_id(1) == 0)
    def _():
        st_ref[...] = jnp.zeros_like(st_ref)

    p_ref[...] = jnp.dot(x_ref[...].astype(BF16), w_ref[...], preferred_element_type=F32)
    q_ref, f_ref, v_ref, g_ref = (p_ref.at[:, pl.ds(j * hw, hw)] for j in range(4))

    f = f_ref[...]
    log_sig = jnp.minimum(f, 0.0) - jnp.log(1.0 + jnp.exp(-jnp.abs(f)))
    a1 = c1_ref[...]
    a2 = c2_ref[...] + log_sig
    log_f = jnp.maximum(a1, a2) + jnp.log(1.0 + jnp.exp(-jnp.abs(a1 - a2)))
    kin = 1.0 - jnp.exp(log_f)

    t_hi = log_f.astype(BF16)
    rem = log_f - t_hi.astype(F32)
    t_mid = rem.astype(BF16)
    t_lo = (rem - t_mid.astype(F32)).astype(BF16)

    tri = tri_ref[...]
    dot = functools.partial(jnp.dot, preferred_element_type=F32)
    bcum = dot(tri, t_hi) + dot(tri, t_mid) + dot(tri, t_lo)

    def group_row(a, size, idx):
        a3 = a.reshape(tt // size, size, hw)
        return jnp.broadcast_to(a3[:, idx:idx + 1, :], a3.shape).reshape(tt, hw)

    blast = group_row(bcum, ch, ch - 1)
    row = lax.broadcasted_iota(I32, (tt, hw), 0)

    def level_decay(h):
        upper = (row % (2 * h)) >= h
        if h == 1:
            return upper, jnp.where(upper, 1.0 - kin, 1.0)
        if h == 2:
            pos = row % 4
            nxt = pltpu.roll(log_f, tt - 1, 0)
            prv = pltpu.roll(log_f, 1, 0)
            z = jnp.where(pos == 0, nxt, jnp.where(pos == 1, 0.0, jnp.where(pos == 2, log_f, log_f + prv)))
            return upper, jnp.exp(z)
        refb = group_row(bcum, 2 * h, h - 1)
        return upper, jnp.exp(jnp.where(upper, bcum - refb, refb - bcum))

    q = q_ref[...]
    qd_ref[...] = (q * jnp.exp(bcum)).astype(BF16)
    kd_ref[...] = (kin * jnp.exp(blast - bcum)).astype(BF16)
    vb_ref[...] = v_ref[...].astype(BF16)
    dec_ref[...] = jnp.exp(blast)

    for i, h in enumerate(HG_LEVELS):
        upper, decay = level_decay(h)
        u_ref[i] = (jnp.where(upper, q, kin) * decay).astype(BF16)

    lvl = lvl_ref[...]
    gain = gain_ref[...]
    nt_dims = (((1,), (1,)), ((), ()))
    for h in range(heads):
        ls = slice(h * dk, (h + 1) * dk)
        scores = jnp.zeros((tt, tt), F32)
        for i in range(nlev):
            u = u_ref[i, :, ls]
            gram = lax.dot_general(u, u, nt_dims, preferred_element_type=F32)
            scores = jnp.where(lvl == i + 1, gram, scores)
        acc = jnp.dot(scores.astype(BF16), vb_ref[:, ls], preferred_element_type=F32)
        acc = acc + _head_sum(q[:, ls] * kin[:, ls]) * v_ref[:, ls]
        inter = []
        for c in range(tt // ch):
            rs = slice(c * ch, (c + 1) * ch)
            st = st_ref[h]
            inter.append(lax.dot_general(qd_ref[rs, ls], st.astype(BF16), nt_dims, preferred_element_type=F32))
            upd = lax.dot_general(vb_ref[rs, ls], kd_ref[rs, ls], (((0,), (0,)), ((), ())),
                                  preferred_element_type=F32)
            st_ref[h] = st * dec_ref[c * ch:c * ch + 1, ls] + upd
        acc = acc + jnp.concatenate(inter, axis=0)
        ms = _head_sum(acc * acc) * (1.0 / dk)
        y = acc * lax.rsqrt(ms + RMS_EPS) * gain[:, ls] * _silu(g_ref[:, ls])
        o_ref[:, ls] = y.astype(o_ref.dtype)


def _hgrn_branch(x, w_hg_in, c1, c2, gain, bsz, seq):
    n, d = x.shape
    hw = w_hg_in.shape[1] // 4
    heads = hw // HG_DK
    tt = _tile(seq, HGRN_ROWS)
    assert tt % HG_CHUNK == 0
    nt = seq // tt
    nlev = len(HG_LEVELS)
    tri, level = _hgrn_tables(tt)
    tri = jnp.asarray(tri, BF16)
    level = jnp.asarray(level, I32)
    vec = pl.BlockSpec((1, hw), lambda b, t: (0, 0))
    vmem = 2 * (_nbytes((tt, d), x.dtype) + _nbytes(w_hg_in.shape, BF16) + _nbytes((tt, hw), BF16)
                + _nbytes(tri.shape, BF16) + _nbytes(level.shape, I32)) \
        + _nbytes((tt, d), BF16) + 16 * _nbytes((tt, hw), F32) + nlev * _nbytes((tt, hw), BF16) \
        + 4 * _nbytes((tt, tt), F32) + heads * HG_DK * HG_DK * 4
    return pl.pallas_call(
        functools.partial(_hgrn_kernel, heads=heads),
        out_shape=jax.ShapeDtypeStruct((n, hw), BF16),
        grid=(bsz, nt),
        in_specs=[pl.BlockSpec((tt, d), lambda b, t: (b * nt + t, 0)),
                  pl.BlockSpec(w_hg_in.shape, lambda b, t: (0, 0)),
                  pl.BlockSpec(tri.shape, lambda b, t: (0, 0)),
                  pl.BlockSpec(level.shape, lambda b, t: (0, 0)), vec, vec, vec],
        out_specs=pl.BlockSpec((tt, hw), lambda b, t: (b * nt + t, 0)),
        scratch_shapes=[pltpu.VMEM((heads, HG_DK, HG_DK), F32), pltpu.VMEM((tt, 4 * hw), F32),
                        pltpu.VMEM((nlev, tt, hw), BF16),
                        pltpu.VMEM((tt, hw), BF16), pltpu.VMEM((tt, hw), BF16),
                        pltpu.VMEM((tt, hw), BF16), pltpu.VMEM((tt, hw), F32)],
        compiler_params=_params(("arbitrary", "arbitrary"), vmem),
    )(x, w_hg_in, tri, level, c1, c2, gain)


def _attn_kernel(x_ref, wq_ref, k_ref, v_ref, o_ref, *, heads):
    dh = XA_DH
    q = jnp.dot(x_ref[...].astype(BF16), wq_ref[...], preferred_element_type=F32)
    scale = dh ** -0.5
    for h in range(heads):
        ls = slice(h * dh, (h + 1) * dh)
        s = lax.dot_general(q[:, ls].astype(BF16), k_ref[:, ls], (((1,), (1,)), ((), ())),
                            preferred_element_type=F32) * scale
        e = jnp.exp(s - jnp.max(s, axis=-1, keepdims=True))
        p = e / jnp.sum(e, axis=-1, keepdims=True)
        o = jnp.dot(p.astype(BF16), v_ref[:, ls], preferred_element_type=F32)
        o_ref[:, ls] = o.astype(o_ref.dtype)


def _attn_branch(x, wq, kv, seq, n_mem):
    n, d = x.shape
    xw = wq.shape[1]
    heads = xw // XA_DH
    tm = _tile(seq, PROJ_ROWS)
    per_seq = seq // tm
    vmem = 2 * (_nbytes((tm, d), x.dtype) + _nbytes(wq.shape, BF16) + 2 * _nbytes((n_mem, xw), BF16)
                + _nbytes((tm, xw), BF16)) + 4 * _nbytes((tm, xw), F32) + 4 * _nbytes((tm, n_mem), F32)
    return pl.pallas_call(
        functools.partial(_attn_kernel, heads=heads),
        out_shape=jax.ShapeDtypeStruct((n, xw), BF16),
        grid=(n // tm,),
        in_specs=[pl.BlockSpec((tm, d), lambda i: (i, 0)),
                  pl.BlockSpec(wq.shape, lambda i: (0, 0)),
                  pl.BlockSpec((n_mem, xw), lambda i: (i // per_seq, 0)),
                  pl.BlockSpec((n_mem, xw), lambda i: (i // per_seq, 1))],
        out_specs=pl.BlockSpec((tm, xw), lambda i: (i, 0)),
        compiler_params=_params(("arbitrary",), vmem),
    )(x, wq, kv, kv)


def _merge_kernel(x_ref, ya_ref, yb_ref, yx_ref, wga_ref, wgb_ref, wgx_ref, wa_ref, wb_ref, wx_ref,
                  o_ref, xb_ref):
    @pl.when(pl.program_id(1) == 0)
    def _():
        xb_ref[...] = x_ref[...].astype(BF16)

    xb = xb_ref[...]
    dot = functools.partial(jnp.dot, preferred_element_type=F32)
    m = _sigmoid(dot(xb, wga_ref[...])) * dot(ya_ref[...], wa_ref[...])
    m = m + _sigmoid(dot(xb, wgb_ref[...])) * dot(yb_ref[...], wb_ref[...])
    m = m + _sigmoid(dot(xb, wgx_ref[...])) * dot(yx_ref[...], wx_ref[...])
    o_ref[...] = m.astype(o_ref.dtype)


def _merge(x, ya, yb, yx, w_gates, wa, wb, wx):
    n, d = x.shape
    tm, tn = _tile(n, MERGE_ROWS), _tile(d, MERGE_COLS)
    nj = d // tn
    row = lambda a: pl.BlockSpec((tm, a.shape[1]), lambda i, j: (i, 0))
    gate = lambda g: pl.BlockSpec((d, tn), lambda i, j, g=g: (0, g * nj + j))
    outw = lambda a: pl.BlockSpec((a.shape[0], tn), lambda i, j: (0, j))
    vmem = 2 * (_nbytes((tm, d), x.dtype) + _nbytes((tm, ya.shape[1] + yb.shape[1] + yx.shape[1]), BF16)
                + 3 * _nbytes((d, tn), BF16) + _nbytes((wa.shape[0] + wb.shape[0] + wx.shape[0], tn), BF16)
                + _nbytes((tm, tn), BF16)) + _nbytes((tm, d), BF16) + 8 * _nbytes((tm, tn), F32)
    return pl.pallas_call(
        _merge_kernel,
        out_shape=jax.ShapeDtypeStruct((n, d), BF16),
        grid=(n // tm, nj),
        in_specs=[row(x), row(ya), row(yb), row(yx), gate(0), gate(1), gate(2),
                  outw(wa), outw(wb), outw(wx)],
        out_specs=pl.BlockSpec((tm, tn), lambda i, j: (i, j)),
        scratch_shapes=[pltpu.VMEM((tm, d), BF16)],
        compiler_params=_params(("arbitrary", "arbitrary"), vmem),
    )(x, ya, yb, yx, w_gates, w_gates, w_gates, wa, wb, wx)


def _oproj_ln_kernel(m_ref, w_ref, x_ref, g_ref, b_ref, rw_ref, rb_ref,
                     o_ref, p_ref, eid_ref, rnk_ref, gate_ref, cnt_ref, carry_ref, *, alpha):
    y = alpha * x_ref[...] + jnp.dot(m_ref[...], w_ref[...], preferred_element_type=F32)
    y = _layer_norm_rows(y, g_ref[...], b_ref[...])
    o_ref[...] = y
    _store_packed_rows(p_ref, y)
    _route(y, rw_ref, rb_ref, eid_ref, rnk_ref, gate_ref, cnt_ref, carry_ref)


def _oproj_ln_route(merged, w_o, x, g, b, rw_t, rb, alpha):
    n, d = x.shape
    assert d % PAIR == 0
    pieces = d // PAIR
    ne = rw_t.shape[0]
    tm = _tile(n, PROJ_ROWS)
    cur = lambda i: (i, 0)
    vec = pl.BlockSpec((1, d), lambda i: (0, 0))
    topk = lambda dt: jax.ShapeDtypeStruct((TOP_K, n), dt)
    kspec = pl.BlockSpec((TOP_K, tm), lambda i: (0, i))
    vmem = 2 * (_nbytes((tm, d), BF16) + _nbytes((d, d), BF16) + 2 * _nbytes((tm, d), F32)
                + _nbytes((tm, d // 2), U32) + _nbytes((ne, d), F32)) + 4 * _nbytes((tm, d), F32) \
        + 3 * _nbytes((tm, tm), F32) + 24 * _nbytes((ne, tm), F32)
    return pl.pallas_call(
        functools.partial(_oproj_ln_kernel, alpha=alpha),
        out_shape=(jax.ShapeDtypeStruct((n, d), F32), jax.ShapeDtypeStruct((n * pieces, LANES), U32),
                   topk(I32), topk(I32), topk(F32), jax.ShapeDtypeStruct((ne, LANES), I32)),
        grid=(n // tm,),
        in_specs=[pl.BlockSpec((tm, d), cur),
                  pl.BlockSpec((d, d), lambda i: (0, 0)),
                  pl.BlockSpec((tm, d), cur), vec, vec,
                  pl.BlockSpec((ne, d), lambda i: (0, 0)),
                  pl.BlockSpec((ne, 1), lambda i: (0, 0))],
        out_specs=(pl.BlockSpec((tm, d), cur),
                   pl.BlockSpec((tm * pieces, LANES), cur),
                   kspec, kspec, kspec, pl.BlockSpec((ne, LANES), lambda i: (0, 0))),
        scratch_shapes=[pltpu.VMEM((ne, 1), F32)],
        compiler_params=_params(("arbitrary",), vmem),
    )(merged, w_o, x, g, b, rw_t, rb)


def _rank_desc(vals):
    rows = vals.shape[0]
    iota = lax.broadcasted_iota(I32, vals.shape, 0)
    rank = jnp.zeros(vals.shape, I32)
    for j in range(rows):
        rowv = vals[j:j + 1, :]
        tie = jnp.where(iota > j, 1, 0)
        rank = rank + jnp.where(rowv > vals, 1, jnp.where(rowv == vals, tie, 0))
    return rank


def _route(x, rw_ref, rb_ref, eid_ref, rnk_ref, gate_ref, cnt_ref, carry_ref):
    ne = rw_ref.shape[0]
    tm = x.shape[0]
    gsz = ne // N_GROUPS

    @pl.when(pl.program_id(0) == 0)
    def _():
        carry_ref[...] = jnp.zeros_like(carry_ref)

    w = rw_ref[...]
    w_hi = w.astype(BF16)
    w_lo = (w - w_hi.astype(F32)).astype(BF16)
    x_hi = x.astype(BF16)
    x_lo = (x - x_hi.astype(F32)).astype(BF16)
    nt_dot = functools.partial(lax.dot_general, dimension_numbers=(((1,), (1,)), ((), ())),
                               preferred_element_type=F32)
    logits = nt_dot(w_hi, x_hi) + nt_dot(w_lo, x_hi) + nt_dot(w_hi, x_lo)
    s = _sigmoid(logits)
    sel = s + rb_ref[...]

    grp = sel.reshape(N_GROUPS, gsz, tm)
    sub_iota = lax.broadcasted_iota(I32, (N_GROUPS, gsz, tm), 1)
    m1 = jnp.max(grp, axis=1, keepdims=True)
    first = jnp.min(jnp.where(grp == m1, sub_iota, gsz), axis=1, keepdims=True)
    m2 = jnp.max(jnp.where(sub_iota == first, -jnp.inf, grp), axis=1, keepdims=True)
    gscore = (m1 + m2).reshape(N_GROUPS, tm)

    gsel = _rank_desc(gscore) < TOPK_GROUPS
    emask = jnp.broadcast_to(gsel.reshape(N_GROUPS, 1, tm), (N_GROUPS, gsz, tm)).reshape(ne, tm)
    masked = jnp.where(emask, sel, -jnp.inf)

    eiota = lax.broadcasted_iota(I32, (ne, tm), 0)
    chosen = (_rank_desc(masked) < TOP_K) & emask

    w = jnp.where(chosen, s, 0.0)
    gate = w / jnp.sum(w, axis=0, keepdims=True) * ROUTED_SCALE

    ch = jnp.where(chosen, 1.0, 0.0).astype(BF16)
    tr = lax.broadcasted_iota(I32, (tm, tm), 0)
    tc = lax.broadcasted_iota(I32, (tm, tm), 1)
    before = jnp.where(tr < tc, 1.0, 0.0).astype(BF16)
    rank_tok = (carry_ref[...] + jnp.dot(ch, before, preferred_element_type=F32)).astype(I32)
    carry_ref[...] = carry_ref[...] + jnp.sum(jnp.where(chosen, 1.0, 0.0), axis=1, keepdims=True)
    cnt_ref[...] = jnp.broadcast_to(carry_ref[...], cnt_ref.shape).astype(I32)

    er = lax.broadcasted_iota(I32, (ne, ne), 0)
    ec = lax.broadcasted_iota(I32, (ne, ne), 1)
    lower = jnp.where(ec < er, 1.0, 0.0).astype(BF16)
    slot = jnp.dot(lower, ch, preferred_element_type=F32).astype(I32)
    for k in range(TOP_K):
        pick = chosen & (slot == k)
        eid_ref[k:k + 1, :] = jnp.sum(jnp.where(pick, eiota, 0), axis=0, keepdims=True)
        rnk_ref[k:k + 1, :] = jnp.sum(jnp.where(pick, rank_tok, 0), axis=0, keepdims=True)
        gate_ref[k:k + 1, :] = jnp.sum(jnp.where(pick, gate, 0.0), axis=0, keepdims=True)


def _pad_pieces(max_pad):
    pieces = []
    p = 1
    while p <= max_pad:
        pieces.append(p)
        p *= 2
    return pieces[::-1]


def _dispatch_kernel(pad_ref, dest_hbm, xp_ref, zero_hbm, sg_ref, su_ref, sd_ref, xs_hbm, sh_ref,
                     idx_ref, sem_idx, sem_row, sem_pad, *, tq, ne, rp):
    i = pl.program_id(0)

    def row_copy(src_row, dst_row):
        return pltpu.make_async_copy(xp_ref.at[pl.ds(pl.multiple_of(src_row * rp, rp), rp)],
                                     xs_hbm.at[pl.ds(pl.multiple_of(dst_row * rp, rp), rp)], sem_row)

    def pad_copies(fn):
        def per_expert(e, carry):
            row = pad_ref[0, e]
            npad = pad_ref[1, e]
            for p in _pad_pieces(EXPERT_ROWS - 1):
                @pl.when((npad & p) != 0)
                def _(row=row, p=p):
                    fn(pltpu.make_async_copy(zero_hbm.at[pl.ds(0, p * rp)],
                                             xs_hbm.at[pl.ds(pl.multiple_of(row * rp, rp), p * rp)], sem_pad))
                row = row + (npad & p)
            return carry
        lax.fori_loop(0, ne, per_expert, 0)

    @pl.when(i == 0)
    def _():
        pad_copies(lambda cp: cp.start())

    idx_cp = pltpu.make_async_copy(dest_hbm.at[i], idx_ref, sem_idx)
    idx_cp.start()
    idx_cp.wait()

    def per_token(t, carry):
        for k in range(TOP_K):
            row_copy(t, idx_ref[t * TOP_K + k]).start(priority=k % 2)
        return carry

    lax.fori_loop(0, tq, per_token, 0)

    xb = _load_packed_rows(xp_ref, tq, sh_ref.shape[1])
    dot = functools.partial(jnp.dot, preferred_element_type=F32)
    hmid = (_silu(dot(xb, sg_ref[...])) * dot(xb, su_ref[...])).astype(BF16)
    sh_ref[...] = dot(hmid, sd_ref[...])

    def drain(t, carry):
        for _ in range(TOP_K * COPY_UNROLL):
            row_copy(0, 0).wait()
        return carry

    lax.fori_loop(0, tq // COPY_UNROLL, drain, 0)

    @pl.when(i == 0)
    def _():
        pad_copies(lambda cp: cp.wait())


def _dispatch(pad_info, dest_tok, xp, n, rows, sg, su, sd):
    rp = xp.shape[0] // n
    d, hid = sg.shape
    ne = pad_info.shape[1]
    tq = _tile(n, DISPATCH_ROWS)
    steps = n // tq
    dest2 = dest_tok.reshape(steps, tq * TOP_K)
    zero = jnp.zeros((EXPERT_ROWS * rp, LANES), U32)
    any_spec = pl.BlockSpec(memory_space=pl.ANY)
    full = lambda a: pl.BlockSpec(a.shape, lambda i, pad: (0, 0))
    grid_spec = pltpu.PrefetchScalarGridSpec(
        num_scalar_prefetch=1, grid=(steps,),
        in_specs=[any_spec, pl.BlockSpec((tq * rp, LANES), lambda i, pad: (i, 0)), any_spec,
                  full(sg), full(su), full(sd)],
        out_specs=(any_spec, pl.BlockSpec((tq, d), lambda i, pad: (i, 0))),
        scratch_shapes=[pltpu.SMEM((tq * TOP_K,), I32), pltpu.SemaphoreType.DMA(()),
                        pltpu.SemaphoreType.DMA(()), pltpu.SemaphoreType.DMA(())])
    vmem = 2 * (_nbytes((tq * rp, LANES), U32) + 3 * _nbytes((d, hid), BF16) + _nbytes((tq, d), F32)) \
        + _nbytes((tq, d), BF16) + 3 * _nbytes((tq, hid), F32) + _nbytes((tq, d), F32)
    return pl.pallas_call(
        functools.partial(_dispatch_kernel, tq=tq, ne=ne, rp=rp),
        out_shape=(jax.ShapeDtypeStruct((rows * rp, LANES), U32), jax.ShapeDtypeStruct((n, d), F32)),
        grid_spec=grid_spec,
        compiler_params=_params(("arbitrary",), vmem),
    )(pad_info, dest2, xp, zero, sg, su, sd)


def _expert_kernel(plan_ref, nu_ref, xs_ref, wg_hbm, wu_hbm, wd_hbm, ys_ref,
                   sg_ref, su_ref, sd_ref, wgb_ref, wub_ref, wdb_ref, sem_ref, *, layer):
    b = pl.program_id(0)

    def fetch(e, slot):
        return (pltpu.make_async_copy(wg_hbm.at[layer, e], sg_ref.at[slot], sem_ref.at[slot]),
                pltpu.make_async_copy(wu_hbm.at[layer, e], su_ref.at[slot], sem_ref.at[slot]),
                pltpu.make_async_copy(wd_hbm.at[layer, e], sd_ref.at[slot], sem_ref.at[slot]))

    @pl.when(b < nu_ref[0])
    def _():
        e = plan_ref[0, b]
        nxt = plan_ref[2, b]
        slot = plan_ref[3, b]

        @pl.when(b == 0)
        def _():
            for cp in fetch(e, slot):
                cp.start()

        @pl.when(plan_ref[1, b] == 1)
        def _():
            @pl.when(nxt >= 0)
            def _():
                for cp in fetch(nxt, 1 - slot):
                    cp.start()

            for cp in fetch(e, slot):
                cp.wait()
            wgb_ref[...] = sg_ref[slot].astype(BF16)
            wub_ref[...] = su_ref[slot].astype(BF16)
            wdb_ref[...] = sd_ref[slot].astype(BF16)

        t = EXPERT_ROWS
        d = wgb_ref.shape[0]
        xb = _load_packed_rows(xs_ref, t, d)
        dot = functools.partial(jnp.dot, preferred_element_type=F32)
        hmid = (_silu(dot(xb, wgb_ref[...])) * dot(xb, wub_ref[...])).astype(BF16)
        _store_packed_rows(ys_ref, dot(hmid, wdb_ref[...]))


def _experts(plan, n_used, xs, wg, wu, wd, layer):
    _, ne, d, hid = wg.shape
    t = EXPERT_ROWS
    rp = d // PAIR
    rows = xs.shape[0] // rp
    nb = rows // t
    clamp = lambda b, plan, nu: (jnp.minimum(b, jnp.maximum(nu[0] - 1, 0)), 0)
    any_spec = pl.BlockSpec(memory_space=pl.ANY)
    grid_spec = pltpu.PrefetchScalarGridSpec(
        num_scalar_prefetch=2, grid=(nb,),
        in_specs=[pl.BlockSpec((t * rp, LANES), clamp), any_spec, any_spec, any_spec],
        out_specs=pl.BlockSpec((t * rp, LANES), clamp),
        scratch_shapes=[pltpu.VMEM((2, d, hid), F32), pltpu.VMEM((2, d, hid), F32), pltpu.VMEM((2, hid, d), F32),
                        pltpu.VMEM((d, hid), BF16), pltpu.VMEM((d, hid), BF16), pltpu.VMEM((hid, d), BF16),
                        pltpu.SemaphoreType.DMA((2,))])
    vmem = 2 * (3 * _nbytes((d, hid), F32) + 2 * _nbytes((t, d // 2), U32)) \
        + 3 * _nbytes((d, hid), BF16) + 3 * _nbytes((t, d), F32)
    return pl.pallas_call(
        functools.partial(_expert_kernel, layer=layer),
        out_shape=jax.ShapeDtypeStruct((rows * rp, LANES), U32),
        grid_spec=grid_spec,
        compiler_params=_params(("arbitrary",), vmem),
    )(plan, n_used, xs, wg, wu, wd)


def _combine_kernel(dest_ref, ys_hbm, x_ref, gate_ref, sh_ref, g_ref, b_ref, o_ref,
                    buf_ref, sem_ref, *, tc, alpha):
    i = pl.program_id(0)
    rp = x_ref.shape[1] // PAIR
    half = i % 2

    def row_copy(src_row, dst_row, h):
        return pltpu.make_async_copy(ys_hbm.at[pl.ds(pl.multiple_of(src_row * rp, rp), rp)],
                                     buf_ref.at[pl.ds(pl.multiple_of(dst_row * rp, rp), rp)], sem_ref.at[h])

    def fetch(step, h):
        def per_group(g, carry):
            src0 = (step * tc + g * COPY_UNROLL) * TOP_K
            dst0 = h * TOP_K * tc + g * COPY_UNROLL
            for u in range(COPY_UNROLL):
                for k in range(TOP_K):
                    row_copy(dest_ref[src0 + u * TOP_K + k], dst0 + k * tc + u, h).start(priority=k % 2)
            return carry
        lax.fori_loop(0, tc // COPY_UNROLL, per_group, 0)

    @pl.when(i == 0)
    def _():
        fetch(0, 0)

    @pl.when(i + 1 < pl.num_programs(0))
    def _():
        fetch(i + 1, 1 - half)

    x = x_ref[...]
    shared = sh_ref[...]

    def drain(g, carry):
        for _ in range(COPY_UNROLL * TOP_K):
            row_copy(0, 0, half).wait()
        return carry

    lax.fori_loop(0, tc // COPY_UNROLL, drain, 0)

    gate = gate_ref[...]
    base = half * (TOP_K * tc * rp)
    cols = []
    for s in range(rp):
        lo = hi = None
        for k in range(TOP_K):
            w = buf_ref[pl.ds(base + k * tc * rp + s, tc, stride=rp), :]
            gk = gate[:, k:k + 1]
            lo_k = lax.bitcast_convert_type(w << 16, F32) * gk
            hi_k = lax.bitcast_convert_type(w & jnp.uint32(0xFFFF0000), F32) * gk
            lo = lo_k if lo is None else lo + lo_k
            hi = hi_k if hi is None else hi + hi_k
        cols += [lo, hi]
    routed = jnp.concatenate(cols, axis=1)
    o_ref[...] = _layer_norm_rows(alpha * x + (shared + routed), g_ref[...], b_ref[...])


def _combine(dest_tok, ys, x, gate_tok, shared, g, b, alpha):
    n, d = x.shape
    tc = _tile(n, COMBINE_ROWS)
    steps = n // tc
    rp = d // PAIR
    vec = pl.BlockSpec((1, d), lambda i, dest: (0, 0))
    rows = pl.BlockSpec((tc, d), lambda i, dest: (i, 0))
    buf_rows = 2 * TOP_K * tc * rp
    vmem = _nbytes((buf_rows, LANES), U32) + 2 * 3 * _nbytes((tc, d), F32) + 6 * _nbytes((tc, d), F32)
    grid_spec = pltpu.PrefetchScalarGridSpec(
        num_scalar_prefetch=1, grid=(steps,),
        in_specs=[pl.BlockSpec(memory_space=pl.ANY), rows,
                  pl.BlockSpec((tc, TOP_K), lambda i, dest: (i, 0)), rows, vec, vec],
        out_specs=pl.BlockSpec((tc, d), lambda i, dest: (i, 0)),
        scratch_shapes=[pltpu.VMEM((buf_rows, LANES), U32), pltpu.SemaphoreType.DMA((2,))])
    return pl.pallas_call(
        functools.partial(_combine_kernel, tc=tc, alpha=alpha),
        out_shape=jax.ShapeDtypeStruct((n, d), F32),
        grid_spec=grid_spec,
        compiler_params=_params(("arbitrary",), vmem),
    )(dest_tok, ys, x, gate_tok, shared, g, b)


def _moe(x1, x1p, eid, rnk, gate, cnt, wg, wu, wd, layer, sg, su, sd, g, b, alpha):
    n, d = x1.shape
    ne = cnt.shape[0]
    counts = cnt[:, 0]
    padded = (counts + EXPERT_ROWS - 1) // EXPERT_ROWS * EXPERT_ROWS
    eidx = jnp.arange(ne, dtype=I32)
    pad_end = jnp.sum(jnp.where(eidx[None, :] <= eidx[:, None], padded[None, :], 0), axis=1)
    pad_start = pad_end - padded
    n_blocks = (n * TOP_K + ne * (EXPERT_ROWS - 1)) // EXPERT_ROWS
    rows = n_blocks * EXPERT_ROWS
    start_of = jnp.sum(jnp.where(eid[:, :, None] == eidx, pad_start, 0), axis=-1)
    dest_tok = (start_of + rnk).T.reshape(-1).astype(I32)
    pad_info = jnp.stack([pad_start + counts, padded - counts]).astype(I32)
    total = pad_end[ne - 1]
    n_used = (total // EXPERT_ROWS).astype(I32).reshape(1)
    blk_start = jnp.minimum(jnp.arange(n_blocks, dtype=I32) * EXPERT_ROWS, total - 1)
    blk_exp = jnp.sum((pad_end[None, :] <= blk_start[:, None]).astype(I32), axis=1)
    blk_exp = jnp.minimum(blk_exp, ne - 1).astype(I32)
    has = counts > 0
    later = jnp.where((eidx[None, :] > eidx[:, None]) & has[None, :], eidx[None, :], ne)
    next_e = jnp.min(later, axis=1)
    next_e = jnp.where(next_e == ne, -1, next_e)
    ordinal = jnp.sum(jnp.where((eidx[None, :] <= eidx[:, None]) & has[None, :], 1, 0), axis=1) - 1
    first = jnp.concatenate([jnp.ones((1,), I32), (blk_exp[1:] != blk_exp[:-1]).astype(I32)])
    plan = jnp.stack([blk_exp, first, next_e[blk_exp], ordinal[blk_exp] % 2]).astype(I32)
    xs, shared = _dispatch(pad_info, dest_tok, x1p, n, rows, sg, su, sd)
    ys = _experts(plan, n_used, xs, wg, wu, wd, layer)
    return _combine(dest_tok, ys, x1, gate.T, shared, g, b, alpha)


def kernel(x, mem, w_in, conv_w, w_conv_out, hg_lb_logits, hg_norm_g, w_hg_out, w_mem_k, w_mem_v,
           w_xa_out, w_o, ln1_g, ln1_b, router_w, router_b, exp_wg, exp_wu, exp_wd,
           sh_wg, sh_wu, sh_wd, ln2_g, ln2_b):
    bsz, seq, d = x.shape
    depth = w_in.shape[0]
    n_mem = mem.shape[1]
    cwid = conv_w.shape[2]
    hw = w_hg_out.shape[1]
    xw = w_mem_k.shape[2]
    alpha = float((2 * depth) ** 0.25)
    n = bsz * seq

    lb_all = jnp.cumsum(jax.nn.softmax(hg_lb_logits.astype(F32), axis=0), axis=0)
    lb_all = lb_all - lb_all[0:1]
    o_conv, o_hg, o_xa, o_gate = 0, 3 * cwid, 3 * cwid + 4 * hw, 3 * cwid + 4 * hw + xw

    h = x.reshape(n, d)
    memf = mem.reshape(bsz * n_mem, d)
    for l in range(depth):
        wl = w_in[l]
        w_conv_in = wl[:, o_conv:o_hg].astype(BF16)
        w_hg_in = wl[:, o_hg:o_xa].astype(BF16)
        w_xa_in = wl[:, o_xa:o_gate].astype(BF16)
        w_gates = wl[:, o_gate:].astype(BF16)
        vec = lambda a: a.astype(F32).reshape(1, -1)

        ya = _conv_branch(h, w_conv_in, conv_w[l].astype(F32), seq)
        lb = lb_all[l]
        yb = _hgrn_branch(h, w_hg_in, vec(jnp.log(lb)), vec(jnp.log1p(-lb)), vec(hg_norm_g[l]), bsz, seq)
        w_kv = jnp.concatenate([w_mem_k[l], w_mem_v[l]], axis=1).astype(BF16)
        kv = _matmul(memf, w_kv, BF16, KV_ROWS, xw)
        yx = _attn_branch(h, w_xa_in, kv, seq, n_mem)
        merged = _merge(h, ya, yb, yx, w_gates, w_conv_out[l].astype(BF16), w_hg_out[l].astype(BF16),
                        w_xa_out[l].astype(BF16))
        x1, x1p, eid, rnk, gate, cnt = _oproj_ln_route(
            merged, w_o[l].astype(BF16), h, vec(ln1_g[l]), vec(ln1_b[l]),
            router_w[l].astype(F32).T, router_b[l].astype(F32).reshape(-1, 1), alpha)
        h = _moe(x1, x1p, eid, rnk, gate, cnt, exp_wg, exp_wu, exp_wd, l,
                    sh_wg[l].astype(BF16), sh_wu[l].astype(BF16), sh_wd[l].astype(BF16),
                    vec(ln2_g[l]), vec(ln2_b[l]), alpha)
    return h.reshape(bsz, seq, d)
```

```python
import functools

import jax
import jax.numpy as jnp
import numpy as np
from jax import lax
from jax.experimental import pallas as pl
from jax.experimental.pallas import tpu as pltpu

F32 = jnp.float32
BF16 = jnp.bfloat16
I32 = jnp.int32
U32 = jnp.uint32

HG_DK = 128
XA_DH = 128
TOP_K = 8
N_GROUPS = 8
TOPK_GROUPS = 4
ROUTED_SCALE = 2.5
LN_EPS = 1e-5
RMS_EPS = 1e-6
HG_CHUNK = 64
HG_LEVELS = (1, 2, 4, 8, 16, 32)
EXPERT_ROWS = 512
COPY_UNROLL = 8

PROJ_ROWS = 512
HGRN_ROWS = 256
MERGE_ROWS, MERGE_COLS = 1024, 512
KV_ROWS = 1024
DISPATCH_ROWS = 512
COMBINE_ROWS = 128

V7X_VMEM_BYTES = 64 * 1024 * 1024
VMEM_CAP = V7X_VMEM_BYTES - 8 * 1024 * 1024


def _params(semantics, vmem_bytes):
    limit = int(min(VMEM_CAP, max(32 * 1024 * 1024, vmem_bytes * 5 // 4 + (4 << 20))))
    return pltpu.CompilerParams(dimension_semantics=semantics, vmem_limit_bytes=limit)


def _nbytes(shape, dtype):
    n = 1
    for s in shape:
        n *= s
    return n * jnp.dtype(dtype).itemsize


def _tile(n, want):
    t = min(n, want)
    while n % t:
        t //= 2
    return t


def _silu(x):
    return x * (1.0 / (1.0 + jnp.exp(-x)))


def _sigmoid(x):
    return 1.0 / (1.0 + jnp.exp(-x))


def _layer_norm_rows(y, g, b):
    mu = jnp.mean(y, axis=-1, keepdims=True)
    yc = y - mu
    var = jnp.mean(yc * yc, axis=-1, keepdims=True)
    return yc * lax.rsqrt(var + LN_EPS) * g + b


LANES = 128
PAIR = 2 * LANES


def _store_packed_rows(p_ref, y):
    rows, d = y.shape
    pieces = d // PAIR
    for s in range(pieces):
        lo = lax.bitcast_convert_type(y[:, s * PAIR:s * PAIR + LANES].astype(BF16).astype(F32), U32)
        hi = lax.bitcast_convert_type(y[:, s * PAIR + LANES:(s + 1) * PAIR].astype(BF16).astype(F32), U32)
        p_ref[pl.ds(s, rows, stride=pieces), :] = (hi & jnp.uint32(0xFFFF0000)) | (lo >> 16)


def _load_packed_rows(p_ref, rows, d):
    pieces = d // PAIR
    cols = []
    for s in range(pieces):
        w = p_ref[pl.ds(s, rows, stride=pieces), :]
        cols.append(lax.bitcast_convert_type(w << 16, F32).astype(BF16))
        cols.append(lax.bitcast_convert_type(w & jnp.uint32(0xFFFF0000), F32).astype(BF16))
    return jnp.concatenate(cols, axis=1)


def _matmul_kernel(x_ref, w_ref, o_ref):
    o_ref[...] = jnp.dot(x_ref[...].astype(BF16), w_ref[...],
                         preferred_element_type=F32).astype(o_ref.dtype)


def _matmul(x, w, out_dtype, tm, tn):
    m, k = x.shape
    n = w.shape[1]
    tm, tn = _tile(m, tm), _tile(n, tn)
    vmem = 2 * (_nbytes((tm, k), x.dtype) + _nbytes((k, tn), w.dtype) + _nbytes((tm, tn), out_dtype)) \
        + _nbytes((tm, k), BF16) + _nbytes((tm, tn), F32)
    return pl.pallas_call(
        _matmul_kernel,
        out_shape=jax.ShapeDtypeStruct((m, n), out_dtype),
        grid=(m // tm, n // tn),
        in_specs=[pl.BlockSpec((tm, k), lambda i, j: (i, 0)),
                  pl.BlockSpec((k, tn), lambda i, j: (0, j))],
        out_specs=pl.BlockSpec((tm, tn), lambda i, j: (i, j)),
        compiler_params=_params(("arbitrary", "arbitrary"), vmem),
    )(x, w)


def _conv_kernel(x_ref, w_ref, cw_ref, o_ref, carry_ref, *, tiles_per_seq):
    i = pl.program_id(0)
    tm, cwid = o_ref.shape

    @pl.when(i % tiles_per_seq == 0)
    def _():
        carry_ref[...] = jnp.zeros_like(carry_ref)

    p = jnp.dot(x_ref[...].astype(BF16), w_ref[...], preferred_element_type=F32)
    u = p[:, :cwid]
    b = p[:, cwid:2 * cwid]
    c = p[:, 2 * cwid:]
    cu = c * u
    prev = carry_ref[...]
    row = lax.broadcasted_iota(I32, (tm, cwid), 0)
    cu1 = jnp.where(row == 0, prev[7:8, :], pltpu.roll(cu, 1, 0))
    cu2 = pltpu.roll(cu, 2, 0)
    cu2 = jnp.where(row == 0, prev[6:7, :], jnp.where(row == 1, prev[7:8, :], cu2))
    cw = cw_ref[...]
    y = b * (cw[0:1, :] * cu2 + cw[1:2, :] * cu1 + cw[2:3, :] * cu)
    o_ref[...] = y.astype(o_ref.dtype)
    carry_ref[...] = cu[tm - 8:, :]


def _conv_branch(x, w_conv_in, conv_w, seq):
    n, d = x.shape
    cwid = conv_w.shape[1]
    assert conv_w.shape[0] == 3
    tm = _tile(seq, PROJ_ROWS)
    vmem = 2 * (_nbytes((tm, d), x.dtype) + _nbytes(w_conv_in.shape, BF16) + _nbytes((tm, cwid), BF16)) \
        + 6 * _nbytes((tm, 3 * cwid), F32)
    return pl.pallas_call(
        functools.partial(_conv_kernel, tiles_per_seq=seq // tm),
        out_shape=jax.ShapeDtypeStruct((n, cwid), BF16),
        grid=(n // tm,),
        in_specs=[pl.BlockSpec((tm, d), lambda i: (i, 0)),
                  pl.BlockSpec(w_conv_in.shape, lambda i: (0, 0)),
                  pl.BlockSpec(conv_w.shape, lambda i: (0, 0))],
        out_specs=pl.BlockSpec((tm, cwid), lambda i: (i, 0)),
        scratch_shapes=[pltpu.VMEM((8, cwid), F32)],
        compiler_params=_params(("arbitrary",), vmem),
    )(x, w_conv_in, conv_w)


def _head_sum(x):
    return jnp.sum(x, axis=-1, keepdims=True)


def _hgrn_tables(tt):
    t = np.arange(tt)[:, None]
    j = np.arange(tt)[None, :]
    tri = ((t // HG_CHUNK) == (j // HG_CHUNK)) & (j <= t)
    level = np.zeros((tt, tt), np.int32)
    for i, h in enumerate(HG_LEVELS):
        level[((t // (2 * h)) == (j // (2 * h))) & ((t % (2 * h)) >= h) & ((j % (2 * h)) < h)] = i + 1
    return tri.astype(np.float32), level


def _hgrn_kernel(x_ref, w_ref, tri_ref, lvl_ref, c1_ref, c2_ref, gain_ref, o_ref,
                 st_ref, p_ref, u_ref, qd_ref, kd_ref, vb_ref, dec_ref, *, heads):
    tt, hw = o_ref.shape
    dk = HG_DK
    ch = HG_CHUNK
    nlev = len(HG_LEVELS)

    @pl.when(pl.program_id(1) == 0)
    def _():
        st_ref[...] = jnp.zeros_like(st_ref)

    p_ref[...] = jnp.dot(x_ref[...].astype(BF16), w_ref[...], preferred_element_type=F32)
    q_ref, f_ref, v_ref, g_ref = (p_ref.at[:, pl.ds(j * hw, hw)] for j in range(4))

    f = f_ref[...]
    log_sig = jnp.minimum(f, 0.0) - jnp.log(1.0 + jnp.exp(-jnp.abs(f)))
    a1 = c1_ref[...]
    a2 = c2_ref[...] + log_sig
    log_f = jnp.maximum(a1, a2) + jnp.log(1.0 + jnp.exp(-jnp.abs(a1 - a2)))
    kin = 1.0 - jnp.exp(log_f)

    t_hi = log_f.astype(BF16)
    rem = log_f - t_hi.astype(F32)
    t_mid = rem.astype(BF16)
    t_lo = (rem - t_mid.astype(F32)).astype(BF16)

    tri = tri_ref[...]
    dot = functools.partial(jnp.dot, preferred_element_type=F32)
    bcum = dot(tri, t_hi) + dot(tri, t_mid) + dot(tri, t_lo)

    def group_row(a, size, idx):
        a3 = a.reshape(tt // size, size, hw)
        return jnp.broadcast_to(a3[:, idx:idx + 1, :], a3.shape).reshape(tt, hw)

    blast = group_row(bcum, ch, ch - 1)
    row = lax.broadcasted_iota(I32, (tt, hw), 0)

    def level_decay(h):
        upper = (row % (2 * h)) >= h
        if h == 1:
            return upper, jnp.where(upper, 1.0 - kin, 1.0)
        if h == 2:
            pos = row % 4
            nxt = pltpu.roll(log_f, tt - 1, 0)
            prv = pltpu.roll(log_f, 1, 0)
            z = jnp.where(pos == 0, nxt, jnp.where(pos == 1, 0.0, jnp.where(pos == 2, log_f, log_f + prv)))
            return upper, jnp.exp(z)
        refb = group_row(bcum, 2 * h, h - 1)
        return upper, jnp.exp(jnp.where(upper, bcum - refb, refb - bcum))

    q = q_ref[...]
    qd_ref[...] = (q * jnp.exp(bcum)).astype(BF16)
    kd_ref[...] = (kin * jnp.exp(blast - bcum)).astype(BF16)
    vb_ref[...] = v_ref[...].astype(BF16)
    dec_ref[...] = jnp.exp(blast)

    for i, h in enumerate(HG_LEVELS):
        upper, decay = level_decay(h)
        u_ref[i] = (jnp.where(upper, q, kin) * decay).astype(BF16)

    lvl = lvl_ref[...]
    gain = gain_ref[...]
    nt_dims = (((1,), (1,)), ((), ()))
    for h in range(heads):
        ls = slice(h * dk, (h + 1) * dk)
        scores = jnp.zeros((tt, tt), F32)
        for i in range(nlev):
            u = u_ref[i, :, ls]
            gram = lax.dot_general(u, u, nt_dims, preferred_element_type=F32)
            scores = jnp.where(lvl == i + 1, gram, scores)
        acc = jnp.dot(scores.astype(BF16), vb_ref[:, ls], preferred_element_type=F32)
        acc = acc + _head_sum(q[:, ls] * kin[:, ls]) * v_ref[:, ls]
        inter = []
        for c in range(tt // ch):
            rs = slice(c * ch, (c + 1) * ch)
            st = st_ref[h]
            inter.append(lax.dot_general(qd_ref[rs, ls], st.astype(BF16), nt_dims, preferred_element_type=F32))
            upd = lax.dot_general(vb_ref[rs, ls], kd_ref[rs, ls], (((0,), (0,)), ((), ())),
                                  preferred_element_type=F32)
            st_ref[h] = st * dec_ref[c * ch:c * ch + 1, ls] + upd
        acc = acc + jnp.concatenate(inter, axis=0)
        ms = _head_sum(acc * acc) * (1.0 / dk)
        y = acc * lax.rsqrt(ms + RMS_EPS) * gain[:, ls] * _silu(g_ref[:, ls])
        o_ref[:, ls] = y.astype(o_ref.dtype)


def _hgrn_branch(x, w_hg_in, c1, c2, gain, bsz, seq):
    n, d = x.shape
    hw = w_hg_in.shape[1] // 4
    heads = hw // HG_DK
    tt = _tile(seq, HGRN_ROWS)
    assert tt % HG_CHUNK == 0
    nt = seq // tt
    nlev = len(HG_LEVELS)
    tri, level = _hgrn_tables(tt)
    tri = jnp.asarray(tri, BF16)
    level = jnp.asarray(level, I32)
    vec = pl.BlockSpec((1, hw), lambda b, t: (0, 0))
    vmem = 2 * (_nbytes((tt, d), x.dtype) + _nbytes(w_hg_in.shape, BF16) + _nbytes((tt, hw), BF16)
                + _nbytes(tri.shape, BF16) + _nbytes(level.shape, I32)) \
        + _nbytes((tt, d), BF16) + 16 * _nbytes((tt, hw), F32) + nlev * _nbytes((tt, hw), BF16) \
        + 4 * _nbytes((tt, tt), F32) + heads * HG_DK * HG_DK * 4
    return pl.pallas_call(
        functools.partial(_hgrn_kernel, heads=heads),
        out_shape=jax.ShapeDtypeStruct((n, hw), BF16),
        grid=(bsz, nt),
        in_specs=[pl.BlockSpec((tt, d), lambda b, t: (b * nt + t, 0)),
                  pl.BlockSpec(w_hg_in.shape, lambda b, t: (0, 0)),
                  pl.BlockSpec(tri.shape, lambda b, t: (0, 0)),
                  pl.BlockSpec(level.shape, lambda b, t: (0, 0)), vec, vec, vec],
        out_specs=pl.BlockSpec((tt, hw), lambda b, t: (b * nt + t, 0)),
        scratch_shapes=[pltpu.VMEM((heads, HG_DK, HG_DK), F32), pltpu.VMEM((tt, 4 * hw), F32),
                        pltpu.VMEM((nlev, tt, hw), BF16),
                        pltpu.VMEM((tt, hw), BF16), pltpu.VMEM((tt, hw), BF16),
                        pltpu.VMEM((tt, hw), BF16), pltpu.VMEM((tt, hw), F32)],
        compiler_params=_params(("arbitrary", "arbitrary"), vmem),
    )(x, w_hg_in, tri, level, c1, c2, gain)


def _attn_kernel(x_ref, wq_ref, k_ref, v_ref, o_ref, *, heads):
    dh = XA_DH
    q = jnp.dot(x_ref[...].astype(BF16), wq_ref[...], preferred_element_type=F32)
    scale = dh ** -0.5
    for h in range(heads):
        ls = slice(h * dh, (h + 1) * dh)
        s = lax.dot_general(q[:, ls].astype(BF16), k_ref[:, ls], (((1,), (1,)), ((), ())),
                            preferred_element_type=F32) * scale
        e = jnp.exp(s - jnp.max(s, axis=-1, keepdims=True))
        p = e / jnp.sum(e, axis=-1, keepdims=True)
        o = jnp.dot(p.astype(BF16), v_ref[:, ls], preferred_element_type=F32)
        o_ref[:, ls] = o.astype(o_ref.dtype)


def _attn_branch(x, wq, kv, seq, n_mem):
    n, d = x.shape
    xw = wq.shape[1]
    heads = xw // XA_DH
    tm = _tile(seq, PROJ_ROWS)
    per_seq = seq // tm
    vmem = 2 * (_nbytes((tm, d), x.dtype) + _nbytes(wq.shape, BF16) + 2 * _nbytes((n_mem, xw), BF16)
                + _nbytes((tm, xw), BF16)) + 4 * _nbytes((tm, xw), F32) + 4 * _nbytes((tm, n_mem), F32)
    return pl.pallas_call(
        functools.partial(_attn_kernel, heads=heads),
        out_shape=jax.ShapeDtypeStruct((n, xw), BF16),
        grid=(n // tm,),
        in_specs=[pl.BlockSpec((tm, d), lambda i: (i, 0)),
                  pl.BlockSpec(wq.shape, lambda i: (0, 0)),
                  pl.BlockSpec((n_mem, xw), lambda i: (i // per_seq, 0)),
                  pl.BlockSpec((n_mem, xw), lambda i: (i // per_seq, 1))],
        out_specs=pl.BlockSpec((tm, xw), lambda i: (i, 0)),
        compiler_params=_params(("arbitrary",), vmem),
    )(x, wq, kv, kv)


def _merge_kernel(x_ref, ya_ref, yb_ref, yx_ref, wga_ref, wgb_ref, wgx_ref, wa_ref, wb_ref, wx_ref,
                  o_ref, xb_ref):
    @pl.when(pl.program_id(1) == 0)
    def _():
        xb_ref[...] = x_ref[...].astype(BF16)

    xb = xb_ref[...]
    dot = functools.partial(jnp.dot, preferred_element_type=F32)
    m = _sigmoid(dot(xb, wga_ref[...])) * dot(ya_ref[...], wa_ref[...])
    m = m + _sigmoid(dot(xb, wgb_ref[...])) * dot(yb_ref[...], wb_ref[...])
    m = m + _sigmoid(dot(xb, wgx_ref[...])) * dot(yx_ref[...], wx_ref[...])
    o_ref[...] = m.astype(o_ref.dtype)


def _merge(x, ya, yb, yx, w_gates, wa, wb, wx):
    n, d = x.shape
    tm, tn = _tile(n, MERGE_ROWS), _tile(d, MERGE_COLS)
    nj = d // tn
    row = lambda a: pl.BlockSpec((tm, a.shape[1]), lambda i, j: (i, 0))
    gate = lambda g: pl.BlockSpec((d, tn), lambda i, j, g=g: (0, g * nj + j))
    outw = lambda a: pl.BlockSpec((a.shape[0], tn), lambda i, j: (0, j))
    vmem = 2 * (_nbytes((tm, d), x.dtype) + _nbytes((tm, ya.shape[1] + yb.shape[1] + yx.shape[1]), BF16)
                + 3 * _nbytes((d, tn), BF16) + _nbytes((wa.shape[0] + wb.shape[0] + wx.shape[0], tn), BF16)
                + _nbytes((tm, tn), BF16)) + _nbytes((tm, d), BF16) + 8 * _nbytes((tm, tn), F32)
    return pl.pallas_call(
        _merge_kernel,
        out_shape=jax.ShapeDtypeStruct((n, d), BF16),
        grid=(n // tm, nj),
        in_specs=[row(x), row(ya), row(yb), row(yx), gate(0), gate(1), gate(2),
                  outw(wa), outw(wb), outw(wx)],
        out_specs=pl.BlockSpec((tm, tn), lambda i, j: (i, j)),
        scratch_shapes=[pltpu.VMEM((tm, d), BF16)],
        compiler_params=_params(("arbitrary", "arbitrary"), vmem),
    )(x, ya, yb, yx, w_gates, w_gates, w_gates, wa, wb, wx)


def _oproj_ln_kernel(m_ref, w_ref, x_ref, g_ref, b_ref, rw_ref, rb_ref,
                     o_ref, p_ref, eid_ref, rnk_ref, gate_ref, cnt_ref, carry_ref, *, alpha):
    y = alpha * x_ref[...] + jnp.dot(m_ref[...], w_ref[...], preferred_element_type=F32)
    y = _layer_norm_rows(y, g_ref[...], b_ref[...])
    o_ref[...] = y
    _store_packed_rows(p_ref, y)
    _route(y, rw_ref, rb_ref, eid_ref, rnk_ref, gate_ref, cnt_ref, carry_ref)


def _oproj_ln_route(merged, w_o, x, g, b, rw_t, rb, alpha):
    n, d = x.shape
    assert d % PAIR == 0
    pieces = d // PAIR
    ne = rw_t.shape[0]
    tm = _tile(n, PROJ_ROWS)
    cur = lambda i: (i, 0)
    vec = pl.BlockSpec((1, d), lambda i: (0, 0))
    topk = lambda dt: jax.ShapeDtypeStruct((TOP_K, n), dt)
    kspec = pl.BlockSpec((TOP_K, tm), lambda i: (0, i))
    vmem = 2 * (_nbytes((tm, d), BF16) + _nbytes((d, d), BF16) + 2 * _nbytes((tm, d), F32)
                + _nbytes((tm, d // 2), U32) + _nbytes((ne, d), F32)) + 4 * _nbytes((tm, d), F32) \
        + 3 * _nbytes((tm, tm), F32) + 24 * _nbytes((ne, tm), F32)
    return pl.pallas_call(
        functools.partial(_oproj_ln_kernel, alpha=alpha),
        out_shape=(jax.ShapeDtypeStruct((n, d), F32), jax.ShapeDtypeStruct((n * pieces, LANES), U32),
                   topk(I32), topk(I32), topk(F32), jax.ShapeDtypeStruct((ne, LANES), I32)),
        grid=(n // tm,),
        in_specs=[pl.BlockSpec((tm, d), cur),
                  pl.BlockSpec((d, d), lambda i: (0, 0)),
                  pl.BlockSpec((tm, d), cur), vec, vec,
                  pl.BlockSpec((ne, d), lambda i: (0, 0)),
                  pl.BlockSpec((ne, 1), lambda i: (0, 0))],
        out_specs=(pl.BlockSpec((tm, d), cur),
                   pl.BlockSpec((tm * pieces, LANES), cur),
                   kspec, kspec, kspec, pl.BlockSpec((ne, LANES), lambda i: (0, 0))),
        scratch_shapes=[pltpu.VMEM((ne, 1), F32)],
        compiler_params=_params(("arbitrary",), vmem),
    )(merged, w_o, x, g, b, rw_t, rb)


def _rank_desc(vals):
    rows = vals.shape[0]
    iota = lax.broadcasted_iota(I32, vals.shape, 0)
    rank = jnp.zeros(vals.shape, I32)
    for j in range(rows):
        rowv = vals[j:j + 1, :]
        tie = jnp.where(iota > j, 1, 0)
        rank = rank + jnp.where(rowv > vals, 1, jnp.where(rowv == vals, tie, 0))
    return rank


def _route(x, rw_ref, rb_ref, eid_ref, rnk_ref, gate_ref, cnt_ref, carry_ref):
    ne = rw_ref.shape[0]
    tm = x.shape[0]
    gsz = ne // N_GROUPS

    @pl.when(pl.program_id(0) == 0)
    def _():
        carry_ref[...] = jnp.zeros_like(carry_ref)

    w = rw_ref[...]
    w_hi = w.astype(BF16)
    w_lo = (w - w_hi.astype(F32)).astype(BF16)
    x_hi = x.astype(BF16)
    x_lo = (x - x_hi.astype(F32)).astype(BF16)
    nt_dot = functools.partial(lax.dot_general, dimension_numbers=(((1,), (1,)), ((), ())),
                               preferred_element_type=F32)
    logits = nt_dot(w_hi, x_hi) + nt_dot(w_lo, x_hi) + nt_dot(w_hi, x_lo)
    s = _sigmoid(logits)
    sel = s + rb_ref[...]

    grp = sel.reshape(N_GROUPS, gsz, tm)
    sub_iota = lax.broadcasted_iota(I32, (N_GROUPS, gsz, tm), 1)
    m1 = jnp.max(grp, axis=1, keepdims=True)
    first = jnp.min(jnp.where(grp == m1, sub_iota, gsz), axis=1, keepdims=True)
    m2 = jnp.max(jnp.where(sub_iota == first, -jnp.inf, grp), axis=1, keepdims=True)
    gscore = (m1 + m2).reshape(N_GROUPS, tm)

    gsel = _rank_desc(gscore) < TOPK_GROUPS
    emask = jnp.broadcast_to(gsel.reshape(N_GROUPS, 1, tm), (N_GROUPS, gsz, tm)).reshape(ne, tm)
    masked = jnp.where(emask, sel, -jnp.inf)

    eiota = lax.broadcasted_iota(I32, (ne, tm), 0)
    chosen = (_rank_desc(masked) < TOP_K) & emask

    w = jnp.where(chosen, s, 0.0)
    gate = w / jnp.sum(w, axis=0, keepdims=True) * ROUTED_SCALE

    ch = jnp.where(chosen, 1.0, 0.0).astype(BF16)
    tr = lax.broadcasted_iota(I32, (tm, tm), 0)
    tc = lax.broadcasted_iota(I32, (tm, tm), 1)
    before = jnp.where(tr < tc, 1.0, 0.0).astype(BF16)
    rank_tok = (carry_ref[...] + jnp.dot(ch, before, preferred_element_type=F32)).astype(I32)
    carry_ref[...] = carry_ref[...] + jnp.sum(jnp.where(chosen, 1.0, 0.0), axis=1, keepdims=True)
    cnt_ref[...] = jnp.broadcast_to(carry_ref[...], cnt_ref.shape).astype(I32)

    er = lax.broadcasted_iota(I32, (ne, ne), 0)
    ec = lax.broadcasted_iota(I32, (ne, ne), 1)
    lower = jnp.where(ec < er, 1.0, 0.0).astype(BF16)
    slot = jnp.dot(lower, ch, preferred_element_type=F32).astype(I32)
    for k in range(TOP_K):
        pick = chosen & (slot == k)
        eid_ref[k:k + 1, :] = jnp.sum(jnp.where(pick, eiota, 0), axis=0, keepdims=True)
        rnk_ref[k:k + 1, :] = jnp.sum(jnp.where(pick, rank_tok, 0), axis=0, keepdims=True)
        gate_ref[k:k + 1, :] = jnp.sum(jnp.where(pick, gate, 0.0), axis=0, keepdims=True)


def _pad_pieces(max_pad):
    pieces = []
    p = 1
    while p <= max_pad:
        pieces.append(p)
        p *= 2
    return pieces[::-1]


def _dispatch_kernel(pad_ref, dest_ref, xp_ref, zero_hbm, sg_ref, su_ref, sd_ref, xs_hbm, sh_ref,
                     sem_row, sem_pad, *, tq, ne, rp):
    i = pl.program_id(0)

    def row_copy(src_row, dst_row):
        return pltpu.make_async_copy(xp_ref.at[pl.ds(pl.multiple_of(src_row * rp, rp), rp)],
                                     xs_hbm.at[pl.ds(pl.multiple_of(dst_row * rp, rp), rp)], sem_row)

    def pad_copies(fn):
        def per_expert(e, carry):
            row = pad_ref[0, e]
            npad = pad_ref[1, e]
            for p in _pad_pieces(EXPERT_ROWS - 1):
                @pl.when((npad & p) != 0)
                def _(row=row, p=p):
                    fn(pltpu.make_async_copy(zero_hbm.at[pl.ds(0, p * rp)],
                                             xs_hbm.at[pl.ds(pl.multiple_of(row * rp, rp), p * rp)], sem_pad))
                row = row + (npad & p)
            return carry
        lax.fori_loop(0, ne, per_expert, 0)

    @pl.when(i == 0)
    def _():
        pad_copies(lambda cp: cp.start())


    def per_token(t, carry):
        for k in range(TOP_K):
            row_copy(t, dest_ref[(i * tq + t) * TOP_K + k]).start(priority=k % 2)
        return carry

    lax.fori_loop(0, tq, per_token, 0)

    xb = _load_packed_rows(xp_ref, tq, sh_ref.shape[1])
    dot = functools.partial(jnp.dot, preferred_element_type=F32)
    hmid = (_silu(dot(xb, sg_ref[...])) * dot(xb, su_ref[...])).astype(BF16)
    sh_ref[...] = dot(hmid, sd_ref[...])

    def drain(t, carry):
        for _ in range(TOP_K * COPY_UNROLL):
            row_copy(0, 0).wait()
        return carry

    lax.fori_loop(0, tq // COPY_UNROLL, drain, 0)

    @pl.when(i == 0)
    def _():
        pad_copies(lambda cp: cp.wait())


def _dispatch(pad_info, dest_tok, xp, n, rows, sg, su, sd):
    rp = xp.shape[0] // n
    d, hid = sg.shape
    ne = pad_info.shape[1]
    tq = _tile(n, DISPATCH_ROWS)
    steps = n // tq
    zero = jnp.zeros((EXPERT_ROWS * rp, LANES), U32)
    any_spec = pl.BlockSpec(memory_space=pl.ANY)
    full = lambda a: pl.BlockSpec(a.shape, lambda i, pad, dest: (0, 0))
    grid_spec = pltpu.PrefetchScalarGridSpec(
        num_scalar_prefetch=2, grid=(steps,),
        in_specs=[pl.BlockSpec((tq * rp, LANES), lambda i, pad, dest: (i, 0)), any_spec,
                  full(sg), full(su), full(sd)],
        out_specs=(any_spec, pl.BlockSpec((tq, d), lambda i, pad, dest: (i, 0))),
        scratch_shapes=[pltpu.SemaphoreType.DMA(()), pltpu.SemaphoreType.DMA(())])
    vmem = 2 * (_nbytes((tq * rp, LANES), U32) + 3 * _nbytes((d, hid), BF16) + _nbytes((tq, d), F32)) \
        + _nbytes((tq, d), BF16) + 3 * _nbytes((tq, hid), F32) + _nbytes((tq, d), F32)
    return pl.pallas_call(
        functools.partial(_dispatch_kernel, tq=tq, ne=ne, rp=rp),
        out_shape=(jax.ShapeDtypeStruct((rows * rp, LANES), U32), jax.ShapeDtypeStruct((n, d), F32)),
        grid_spec=grid_spec,
        compiler_params=_params(("arbitrary",), vmem),
    )(pad_info, dest_tok, xp, zero, sg, su, sd)


def _expert_kernel(plan_ref, nu_ref, xs_ref, wg_hbm, wu_hbm, wd_hbm, ys_ref,
                   sg_ref, su_ref, sd_ref, wgb_ref, wub_ref, wdb_ref, sem_ref, *, layer):
    b = pl.program_id(0)

    def fetch(e, slot):
        return (pltpu.make_async_copy(wg_hbm.at[layer, e], sg_ref.at[slot], sem_ref.at[slot]),
                pltpu.make_async_copy(wu_hbm.at[layer, e], su_ref.at[slot], sem_ref.at[slot]),
                pltpu.make_async_copy(wd_hbm.at[layer, e], sd_ref.at[slot], sem_ref.at[slot]))

    @pl.when(b < nu_ref[0])
    def _():
        e = plan_ref[0, b]
        nxt = plan_ref[2, b]
        slot = plan_ref[3, b]

        @pl.when(b == 0)
        def _():
            for cp in fetch(e, slot):
                cp.start()

        @pl.when(plan_ref[1, b] == 1)
        def _():
            @pl.when(nxt >= 0)
            def _():
                for cp in fetch(nxt, 1 - slot):
                    cp.start()

            for cp in fetch(e, slot):
                cp.wait()
            wgb_ref[...] = sg_ref[slot].astype(BF16)
            wub_ref[...] = su_ref[slot].astype(BF16)
            wdb_ref[...] = sd_ref[slot].astype(BF16)

        t = EXPERT_ROWS
        d = wgb_ref.shape[0]
        xb = _load_packed_rows(xs_ref, t, d)
        dot = functools.partial(jnp.dot, preferred_element_type=F32)
        hmid = (_silu(dot(xb, wgb_ref[...])) * dot(xb, wub_ref[...])).astype(BF16)
        _store_packed_rows(ys_ref, dot(hmid, wdb_ref[...]))


def _experts(plan, n_used, xs, wg, wu, wd, layer):
    _, ne, d, hid = wg.shape
    t = EXPERT_ROWS
    rp = d // PAIR
    rows = xs.shape[0] // rp
    nb = rows // t
    clamp = lambda b, plan, nu: (jnp.minimum(b, jnp.maximum(nu[0] - 1, 0)), 0)
    any_spec = pl.BlockSpec(memory_space=pl.ANY)
    grid_spec = pltpu.PrefetchScalarGridSpec(
        num_scalar_prefetch=2, grid=(nb,),
        in_specs=[pl.BlockSpec((t * rp, LANES), clamp), any_spec, any_spec, any_spec],
        out_specs=pl.BlockSpec((t * rp, LANES), clamp),
        scratch_shapes=[pltpu.VMEM((2, d, hid), F32), pltpu.VMEM((2, d, hid), F32), pltpu.VMEM((2, hid, d), F32),
                        pltpu.VMEM((d, hid), BF16), pltpu.VMEM((d, hid), BF16), pltpu.VMEM((hid, d), BF16),
                        pltpu.SemaphoreType.DMA((2,))])
    vmem = 2 * (3 * _nbytes((d, hid), F32) + 2 * _nbytes((t, d // 2), U32)) \
        + 3 * _nbytes((d, hid), BF16) + 3 * _nbytes((t, d), F32)
    return pl.pallas_call(
        functools.partial(_expert_kernel, layer=layer),
        out_shape=jax.ShapeDtypeStruct((rows * rp, LANES), U32),
        grid_spec=grid_spec,
        compiler_params=_params(("arbitrary",), vmem),
    )(plan, n_used, xs, wg, wu, wd)


def _combine_kernel(dest_ref, ys_hbm, x_ref, gate_ref, sh_ref, g_ref, b_ref, o_ref,
                    buf_ref, sem_ref, *, tc, alpha):
    i = pl.program_id(0)
    rp = x_ref.shape[1] // PAIR
    half = i % 2

    def row_copy(src_row, dst_row, h):
        return pltpu.make_async_copy(ys_hbm.at[pl.ds(pl.multiple_of(src_row * rp, rp), rp)],
                                     buf_ref.at[pl.ds(pl.multiple_of(dst_row * rp, rp), rp)], sem_ref.at[h])

    def fetch(step, h):
        def per_group(g, carry):
            src0 = (step * tc + g * COPY_UNROLL) * TOP_K
            dst0 = h * TOP_K * tc + g * COPY_UNROLL
            for u in range(COPY_UNROLL):
                for k in range(TOP_K):
                    row_copy(dest_ref[src0 + u * TOP_K + k], dst0 + k * tc + u, h).start(priority=k % 2)
            return carry
        lax.fori_loop(0, tc // COPY_UNROLL, per_group, 0)

    @pl.when(i == 0)
    def _():
        fetch(0, 0)

    @pl.when(i + 1 < pl.num_programs(0))
    def _():
        fetch(i + 1, 1 - half)

    x = x_ref[...]
    shared = sh_ref[...]

    def drain(g, carry):
        for _ in range(COPY_UNROLL * TOP_K):
            row_copy(0, 0, half).wait()
        return carry

    lax.fori_loop(0, tc // COPY_UNROLL, drain, 0)

    gate = gate_ref[...]
    base = half * (TOP_K * tc * rp)
    cols = []
    for s in range(rp):
        lo = hi = None
        for k in range(TOP_K):
            w = buf_ref[pl.ds(base + k * tc * rp + s, tc, stride=rp), :]
            gk = gate[:, k:k + 1]
            lo_k = lax.bitcast_convert_type(w << 16, F32) * gk
            hi_k = lax.bitcast_convert_type(w & jnp.uint32(0xFFFF0000), F32) * gk
            lo = lo_k if lo is None else lo + lo_k
            hi = hi_k if hi is None else hi + hi_k
        cols += [lo, hi]
    routed = jnp.concatenate(cols, axis=1)
    o_ref[...] = _layer_norm_rows(alpha * x + (shared + routed), g_ref[...], b_ref[...])


def _combine(dest_tok, ys, x, gate_tok, shared, g, b, alpha):
    n, d = x.shape
    tc = _tile(n, COMBINE_ROWS)
    steps = n // tc
    rp = d // PAIR
    vec = pl.BlockSpec((1, d), lambda i, dest: (0, 0))
    rows = pl.BlockSpec((tc, d), lambda i, dest: (i, 0))
    buf_rows = 2 * TOP_K * tc * rp
    vmem = _nbytes((buf_rows, LANES), U32) + 2 * 3 * _nbytes((tc, d), F32) + 6 * _nbytes((tc, d), F32)
    grid_spec = pltpu.PrefetchScalarGridSpec(
        num_scalar_prefetch=1, grid=(steps,),
        in_specs=[pl.BlockSpec(memory_space=pl.ANY), rows,
                  pl.BlockSpec((tc, TOP_K), lambda i, dest: (i, 0)), rows, vec, vec],
        out_specs=pl.BlockSpec((tc, d), lambda i, dest: (i, 0)),
        scratch_shapes=[pltpu.VMEM((buf_rows, LANES), U32), pltpu.SemaphoreType.DMA((2,))])
    return pl.pallas_call(
        functools.partial(_combine_kernel, tc=tc, alpha=alpha),
        out_shape=jax.ShapeDtypeStruct((n, d), F32),
        grid_spec=grid_spec,
        compiler_params=_params(("arbitrary",), vmem),
    )(dest_tok, ys, x, gate_tok, shared, g, b)


def _moe(x1, x1p, eid, rnk, gate, cnt, wg, wu, wd, layer, sg, su, sd, g, b, alpha):
    n, d = x1.shape
    ne = cnt.shape[0]
    counts = cnt[:, 0]
    padded = (counts + EXPERT_ROWS - 1) // EXPERT_ROWS * EXPERT_ROWS
    eidx = jnp.arange(ne, dtype=I32)
    pad_end = jnp.sum(jnp.where(eidx[None, :] <= eidx[:, None], padded[None, :], 0), axis=1)
    pad_start = pad_end - padded
    n_blocks = (n * TOP_K + ne * (EXPERT_ROWS - 1)) // EXPERT_ROWS
    rows = n_blocks * EXPERT_ROWS
    start_of = jnp.sum(jnp.where(eid[:, :, None] == eidx, pad_start, 0), axis=-1)
    dest_tok = (start_of + rnk).T.reshape(-1).astype(I32)
    pad_info = jnp.stack([pad_start + counts, padded - counts]).astype(I32)
    total = pad_end[ne - 1]
    n_used = (total // EXPERT_ROWS).astype(I32).reshape(1)
    blk_start = jnp.minimum(jnp.arange(n_blocks, dtype=I32) * EXPERT_ROWS, total - 1)
    blk_exp = jnp.sum((pad_end[None, :] <= blk_start[:, None]).astype(I32), axis=1)
    blk_exp = jnp.minimum(blk_exp, ne - 1).astype(I32)
    has = counts > 0
    later = jnp.where((eidx[None, :] > eidx[:, None]) & has[None, :], eidx[None, :], ne)
    next_e = jnp.min(later, axis=1)
    next_e = jnp.where(next_e == ne, -1, next_e)
    ordinal = jnp.sum(jnp.where((eidx[None, :] <= eidx[:, None]) & has[None, :], 1, 0), axis=1) - 1
    first = jnp.concatenate([jnp.ones((1,), I32), (blk_exp[1:] != blk_exp[:-1]).astype(I32)])
    plan = jnp.stack([blk_exp, first, next_e[blk_exp], ordinal[blk_exp] % 2]).astype(I32)
    xs, shared = _dispatch(pad_info, dest_tok, x1p, n, rows, sg, su, sd)
    ys = _experts(plan, n_used, xs, wg, wu, wd, layer)
    return _combine(dest_tok, ys, x1, gate.T, shared, g, b, alpha)


def kernel(x, mem, w_in, conv_w, w_conv_out, hg_lb_logits, hg_norm_g, w_hg_out, w_mem_k, w_mem_v,
           w_xa_out, w_o, ln1_g, ln1_b, router_w, router_b, exp_wg, exp_wu, exp_wd,
           sh_wg, sh_wu, sh_wd, ln2_g, ln2_b):
    bsz, seq, d = x.shape
    depth = w_in.shape[0]
    n_mem = mem.shape[1]
    cwid = conv_w.shape[2]
    hw = w_hg_out.shape[1]
    xw = w_mem_k.shape[2]
    alpha = float((2 * depth) ** 0.25)
    n = bsz * seq

    lb_all = jnp.cumsum(jax.nn.softmax(hg_lb_logits.astype(F32), axis=0), axis=0)
    lb_all = lb_all - lb_all[0:1]
    o_conv, o_hg, o_xa, o_gate = 0, 3 * cwid, 3 * cwid + 4 * hw, 3 * cwid + 4 * hw + xw

    h = x.reshape(n, d)
    memf = mem.reshape(bsz * n_mem, d)
    for l in range(depth):
        wl = w_in[l]
        w_conv_in = wl[:, o_conv:o_hg].astype(BF16)
        w_hg_in = wl[:, o_hg:o_xa].astype(BF16)
        w_xa_in = wl[:, o_xa:o_gate].astype(BF16)
        w_gates = wl[:, o_gate:].astype(BF16)
        vec = lambda a: a.astype(F32).reshape(1, -1)

        ya = _conv_branch(h, w_conv_in, conv_w[l].astype(F32), seq)
        lb = lb_all[l]
        yb = _hgrn_branch(h, w_hg_in, vec(jnp.log(lb)), vec(jnp.log1p(-lb)), vec(hg_norm_g[l]), bsz, seq)
        w_kv = jnp.concatenate([w_mem_k[l], w_mem_v[l]], axis=1).astype(BF16)
        kv = _matmul(memf, w_kv, BF16, KV_ROWS, xw)
        yx = _attn_branch(h, w_xa_in, kv, seq, n_mem)
        merged = _merge(h, ya, yb, yx, w_gates, w_conv_out[l].astype(BF16), w_hg_out[l].astype(BF16),
                        w_xa_out[l].astype(BF16))
        x1, x1p, eid, rnk, gate, cnt = _oproj_ln_route(
            merged, w_o[l].astype(BF16), h, vec(ln1_g[l]), vec(ln1_b[l]),
            router_w[l].astype(F32).T, router_b[l].astype(F32).reshape(-1, 1), alpha)
        h = _moe(x1, x1p, eid, rnk, gate, cnt, exp_wg, exp_wu, exp_wd, l,
                    sh_wg[l].astype(BF16), sh_wu[l].astype(BF16), sh_wd[l].astype(BF16),
                    vec(ln2_g[l]), vec(ln2_b[l]), alpha)
    return h.reshape(bsz, seq, d)
```
